```python
import jax, jax.numpy as jnp
from jax import lax
import numpy as np

D_MODEL = 1024
BATCH = 4
SEQ = 4096
DEPTH = 1
DEC_BATCH = 128
DEC_SEQ = 8
PAST_LEN = 2048
PAGE_SIZE = 128

SWA_GROUPS = ((128, 1), (512, 4), (2048, 16))
N_GROUPS = 3
A_HEADS = 4
A_HEAD_DIM = 64
A_WIDTH = A_HEADS * A_HEAD_DIM
BAND = 128
G_HEADS = 4
G_DK = D_MODEL // 2 // G_HEADS
G_DV = D_MODEL // G_HEADS
G_KW = G_HEADS * G_DK
G_VW = G_HEADS * G_DV
G_LOWRANK = 16
G_TAU = 16.0
G_CHUNK = 32
D_FF = 2816
EPS = 1e-6
SPLITS = (N_GROUPS * A_WIDTH, N_GROUPS * A_WIDTH, N_GROUPS * A_WIDTH,
          G_KW, G_KW, G_VW, G_VW, G_LOWRANK, D_MODEL, D_MODEL)
D_IN = sum(SPLITS)

kernel_name = "macaron_dilated_swa_gla_gated_hybrid_step"


def rmsnorm(x, g):
    xf = x.astype(jnp.float32)
    y = xf * lax.rsqrt(jnp.mean(xf * xf, axis=-1, keepdims=True) + EPS)
    return (y * g.astype(jnp.float32)).astype(x.dtype)


def head_rmsnorm(x, g):
    xf = x.astype(jnp.float32)
    return xf * lax.rsqrt(jnp.mean(xf * xf, axis=-1, keepdims=True) + EPS) * g.astype(jnp.float32)


def swiglu(x, w_gate, w_up, w_down):
    return (jax.nn.silu(x @ w_gate) * (x @ w_up)) @ w_down


def dilated_band_attn(q, k, v, d):
    B, L, H, Dh = q.shape
    n = L // d
    nb = -(-n // BAND)
    n_pad = nb * BAND

    def to_sub(t):
        t = t.reshape(B, n, d, H, Dh).transpose(0, 2, 1, 3, 4).reshape(B * d, n, H, Dh)
        return jnp.pad(t, ((0, 0), (0, n_pad - n), (0, 0), (0, 0)))

    def with_prev(t):
        tb = t.reshape(B * d, nb, BAND, H, Dh)
        prev = jnp.pad(tb, ((0, 0), (1, 0), (0, 0), (0, 0), (0, 0)))[:, :-1]
        return jnp.concatenate([prev, tb], axis=2)

    qb = to_sub(q).reshape(B * d, nb, BAND, H, Dh)
    kb = with_prev(to_sub(k))
    vb = with_prev(to_sub(v))
    s = jnp.einsum('znqhd,znkhd->znhqk', qb, kb) * (Dh ** -0.5)
    dist = (jnp.arange(BAND)[:, None] + BAND) - jnp.arange(2 * BAND)[None, :]
    band = (dist >= 0) & (dist <= BAND)
    has_prev = (jnp.arange(nb)[:, None, None] > 0) | (jnp.arange(2 * BAND)[None, None, :] >= BAND)
    valid = band[None] & has_prev
    s = jnp.where(valid[None, :, None], s, -jnp.inf)
    m = jnp.max(s, axis=-1, keepdims=True)
    p = jnp.exp(s - m)
    den = jnp.sum(p, axis=-1, keepdims=True)
    o = jnp.einsum('znhqk,znkhd->znqhd', p / den, vb)
    lse = jnp.swapaxes((m + jnp.log(den))[..., 0], 2, 3)

    def from_sub(t):
        rest = t.shape[3:]
        t = t.reshape(B, d, n_pad, *rest)[:, :, :n]
        return jnp.moveaxis(t, 1, 2).reshape(B, L, *rest)

    return from_sub(o), from_sub(lse)


def dilated_cached_attn(q, k_all, v_all, d, n_past):
    T = q.shape[1]
    Dh = q.shape[-1]
    idx = n_past + jnp.arange(T)[:, None] - d * jnp.arange(BAND + 1)[None, :]
    valid = idx >= 0
    idx = jnp.maximum(idx, 0)
    kg = k_all[:, idx]
    vg = v_all[:, idx]
    s = jnp.einsum('bthd,btjhd->bthj', q, kg) * (Dh ** -0.5)
    s = jnp.where(valid[None, :, None, :], s, -jnp.inf)
    m = jnp.max(s, axis=-1, keepdims=True)
    p = jnp.exp(s - m)
    den = jnp.sum(p, axis=-1, keepdims=True)
    o = jnp.einsum('bthj,btjhd->bthd', p / den, vg)
    return o, (m + jnp.log(den))[..., 0]


def merge_groups(outs, lses):
    w = jax.nn.softmax(jnp.stack(lses, 0), axis=0)
    return jnp.einsum('gblh,gblhd->blhd', w, jnp.stack(outs, 0))


def swa_prompt(qa, ka, va):
    L = qa.shape[1]
    outs, lses, rows = [], [], []
    for g, (w, d) in enumerate(SWA_GROUPS):
        o, lse = dilated_band_attn(qa[:, :, g], ka[:, :, g], va[:, :, g], d)
        outs.append(o)
        lses.append(lse)
        keep = min(w, L)
        rows.append(jnp.stack([ka[:, L - keep:, g], va[:, L - keep:, g]], axis=2))
    return merge_groups(outs, lses), rows


def make_swa_sample(caches):
    def fn(qa, ka, va):
        outs, lses, rows = [], [], []
        for g, ((w, d), buf) in enumerate(zip(SWA_GROUPS, caches)):
            buf = buf.astype(jnp.float32)
            k_all = jnp.concatenate([buf[:, :, 0], ka[:, :, g]], axis=1)
            v_all = jnp.concatenate([buf[:, :, 1], va[:, :, g]], axis=1)
            o, lse = dilated_cached_attn(qa[:, :, g], k_all, v_all, d, buf.shape[1])
            outs.append(o)
            lses.append(lse)
            rows.append(jnp.stack([ka[:, :, g], va[:, :, g]], axis=2))
        return merge_groups(outs, lses), rows
    return fn


def gla_chunked(q, k, v, log_a, S0):
    B, L, H, Dk = q.shape
    Dv = v.shape[-1]
    C = min(G_CHUNK, L)
    n = -(-L // C)
    pad = n * C - L

    def chunks(t):
        t = jnp.pad(t, ((0, 0), (0, pad), (0, 0), (0, 0)))
        return jnp.moveaxis(t.reshape(B, n, C, H, t.shape[-1]), 1, 0)

    causal = jnp.tril(jnp.ones((C, C), dtype=bool))

    def step(S, inp):
        qc, kc, vc, ac = inp
        b = jnp.cumsum(ac, axis=1)
        qe = qc * jnp.exp(b)
        ke = kc * jnp.exp(-b)
        att = jnp.where(causal, jnp.einsum('bthk,bshk->bhts', qe, ke), 0.0)
        o = jnp.einsum('bhts,bshv->bthv', att, vc) + jnp.einsum('bthk,bhkv->bthv', qe, S)
        b_last = b[:, -1]
        S = S * jnp.exp(b_last)[..., None] + jnp.einsum(
            'bshk,bshv->bhkv', kc * jnp.exp(b_last[:, None] - b), vc)
        return S, o

    S, o = lax.scan(step, S0, (chunks(q), chunks(k), chunks(v), chunks(log_a)))
    o = jnp.moveaxis(o, 0, 1).reshape(B, n * C, H, Dv)[:, :L]
    return o, S


def hybrid_layer(x, mixer_a, S0, p):
    B, L, _ = x.shape
    dt = x.dtype
    x = x + 0.5 * swiglu(rmsnorm(x, p['norm_ffn1']), p['ffn1_gate'], p['ffn1_up'], p['ffn1_down'])
    h = rmsnorm(x, p['norm_mix'])
    proj = h @ p['w_in']
    split_at = [int(i) for i in np.cumsum(SPLITS)[:-1]]
    qa, ka, va, qg, kg, vg, rg, lr, ga, gb = jnp.split(proj, split_at, axis=-1)
    shp = (B, L, N_GROUPS, A_HEADS, A_HEAD_DIM)
    qa = head_rmsnorm(qa.reshape(shp), p['a_q_norm'][:, None, :])
    ka = head_rmsnorm(ka.reshape(shp), p['a_k_norm'][:, None, :])
    va = va.reshape(shp).astype(jnp.float32)
    a_out, a_rows = mixer_a(qa, ka, va)
    a_out = a_out.reshape(B, L, A_WIDTH).astype(dt) @ p['w_a_out']
    q = qg.reshape(B, L, G_HEADS, G_DK).astype(jnp.float32) * (G_DK ** -0.5)
    k = kg.reshape(B, L, G_HEADS, G_DK).astype(jnp.float32)
    v = vg.reshape(B, L, G_HEADS, G_DV).astype(jnp.float32)
    log_a = jax.nn.log_sigmoid((lr @ p['g_alpha_up'] + p['g_alpha_bias']).astype(jnp.float32)) / G_TAU
    o, S = gla_chunked(q, k, v, log_a.reshape(B, L, G_HEADS, G_DK), S0.astype(jnp.float32))
    o = head_rmsnorm(o, p['g_out_norm']).reshape(B, L, G_VW) * jax.nn.silu(rg.astype(jnp.float32))
    b_out = o.astype(dt) @ p['w_b_out']
    m = jax.nn.sigmoid(ga) * a_out + jax.nn.sigmoid(gb) * b_out
    x = x + m @ p['w_out']
    x = x + 0.5 * swiglu(rmsnorm(x, p['norm_ffn2']), p['ffn2_gate'], p['ffn2_up'], p['ffn2_down'])
    return x, [r.astype(dt) for r in a_rows], S.astype(dt)


def setup_inputs(seed: int = 0) -> dict:
    key = jax.random.key(seed)
    ks = jax.random.split(key, 32)
    f32 = jnp.float32

    def nrm(k, shape, scale):
        return jax.random.normal(k, shape, f32) * scale

    def gain(k, shape):
        return 1.0 + 0.01 * jax.random.normal(k, shape, f32)

    cache_shape = lambda w: (DEC_BATCH, min(w, PAST_LEN), 2, A_HEADS, A_HEAD_DIM)
    return {
        'x_prompt': nrm(ks[0], (BATCH, SEQ, D_MODEL), 1.0),
        'x_sample': nrm(ks[1], (DEC_BATCH, DEC_SEQ, D_MODEL), 1.0),
        'cache_swa128_kv': nrm(ks[2], cache_shape(SWA_GROUPS[0][0]), 1.0),
        'cache_swa512_kv': nrm(ks[3], cache_shape(SWA_GROUPS[1][0]), 1.0),
        'cache_swa2048_kv': nrm(ks[4], cache_shape(SWA_GROUPS[2][0]), 1.0),
        'state_gla': nrm(ks[5], (DEC_BATCH, G_HEADS, G_DK, G_DV), 0.3),
        'norm_ffn1': gain(ks[6], (D_MODEL,)),
        'ffn1_gate': nrm(ks[7], (D_MODEL, D_FF), D_MODEL ** -0.5),
        'ffn1_up': nrm(ks[8], (D_MODEL, D_FF), D_MODEL ** -0.5),
        'ffn1_down': nrm(ks[9], (D_FF, D_MODEL), D_FF ** -0.5),
        'norm_mix': gain(ks[10], (D_MODEL,)),
        'w_in': nrm(ks[11], (D_MODEL, D_IN), D_MODEL ** -0.5),
        'a_q_norm': gain(ks[12], (N_GROUPS, A_HEAD_DIM)),
        'a_k_norm': gain(ks[13], (N_GROUPS, A_HEAD_DIM)),
        'g_alpha_up': nrm(ks[14], (G_LOWRANK, G_KW), G_LOWRANK ** -0.5),
        'g_alpha_bias': nrm(ks[15], (G_KW,), 0.1),
        'g_out_norm': gain(ks[16], (G_DV,)),
        'w_a_out': nrm(ks[17], (A_WIDTH, D_MODEL), A_WIDTH ** -0.5),
        'w_b_out': nrm(ks[18], (G_VW, D_MODEL), G_VW ** -0.5),
        'w_out': nrm(ks[19], (D_MODEL, D_MODEL), D_MODEL ** -0.5),
        'norm_ffn2': gain(ks[20], (D_MODEL,)),
        'ffn2_gate': nrm(ks[21], (D_MODEL, D_FF), D_MODEL ** -0.5),
        'ffn2_up': nrm(ks[22], (D_MODEL, D_FF), D_MODEL ** -0.5),
        'ffn2_down': nrm(ks[23], (D_FF, D_MODEL), D_FF ** -0.5),
    }


def reference(x_prompt, x_sample, cache_swa128_kv, cache_swa512_kv, cache_swa2048_kv, state_gla,
              norm_ffn1, ffn1_gate, ffn1_up, ffn1_down, norm_mix, w_in, a_q_norm, a_k_norm,
              g_alpha_up, g_alpha_bias, g_out_norm, w_a_out, w_b_out, w_out,
              norm_ffn2, ffn2_gate, ffn2_up, ffn2_down):
    p = {'norm_ffn1': norm_ffn1, 'ffn1_gate': ffn1_gate, 'ffn1_up': ffn1_up, 'ffn1_down': ffn1_down,
         'norm_mix': norm_mix, 'w_in': w_in, 'a_q_norm': a_q_norm, 'a_k_norm': a_k_norm,
         'g_alpha_up': g_alpha_up, 'g_alpha_bias': g_alpha_bias, 'g_out_norm': g_out_norm,
         'w_a_out': w_a_out, 'w_b_out': w_b_out, 'w_out': w_out, 'norm_ffn2': norm_ffn2,
         'ffn2_gate': ffn2_gate, 'ffn2_up': ffn2_up, 'ffn2_down': ffn2_down}
    yp, yp_rows, yp_gla = x_prompt, None, None
    ys, ys_rows, ys_gla = x_sample, None, None
    for _ in range(DEPTH):
        S0_prompt = jnp.zeros((x_prompt.shape[0], G_HEADS, G_DK, G_DV), jnp.float32)
        yp, yp_rows, yp_gla = hybrid_layer(yp, swa_prompt, S0_prompt, p)
        ys, ys_rows, ys_gla = hybrid_layer(
            ys, make_swa_sample((cache_swa128_kv, cache_swa512_kv, cache_swa2048_kv)), state_gla, p)
    return (yp, ys, yp_rows[0], yp_rows[1], yp_rows[2], yp_gla,
            ys_rows[0], ys_rows[1], ys_rows[2], ys_gla)
```

```python
import functools

import jax
import jax.numpy as jnp
from jax import lax
from jax.experimental import pallas as pl
from jax.experimental.pallas import tpu as pltpu

F32 = jnp.float32
BF16 = jnp.bfloat16

D_MODEL = 1024
D_FF = 2816
SWA_DILATIONS = (1, 4, 16)
N_GROUPS = 3
A_HEADS = 4
A_HEAD_DIM = 64
A_WIDTH = A_HEADS * A_HEAD_DIM
BAND = 128
G_HEADS = 4
G_DK = 128
G_DV = 256
G_KW = G_HEADS * G_DK
G_VW = G_HEADS * G_DV
G_LOWRANK = 16
G_TAU = 16.0
EPS = 1e-6

LANES = 128
GLA_BLOCK = 128
GLA_SUB = 32
FF_CHUNK = 256
N_CHUNK = 512
VMEM_LIMIT = 58 * 1024 * 1024


def _resident(shape):
    nd = len(shape)
    return pl.BlockSpec(shape, lambda *_: (0,) * nd, pipeline_mode=pl.Buffered(1))


def _params(n_axes):
    return pltpu.CompilerParams(
        dimension_semantics=("arbitrary",) * n_axes, vmem_limit_bytes=VMEM_LIMIT)


def _dot(a, b):
    return jnp.dot(a, b, preferred_element_type=F32)


def _dot_nt(a, b):
    return lax.dot_general(a, b, (((1,), (1,)), ((), ())), preferred_element_type=F32)


def _rmsnorm(x, g):
    return x * lax.rsqrt(jnp.mean(x * x, axis=-1, keepdims=True) + EPS) * g


def _swiglu_residual(x, g_ref, wg_ref, wu_ref, wd_ref, act_ref):
    h = _rmsnorm(x, g_ref[...]).astype(BF16)
    for c in range(D_FF // FF_CHUNK):
        sl = slice(c * FF_CHUNK, (c + 1) * FF_CHUNK)
        gate = _dot(h, wg_ref[:, sl])
        up = _dot(h, wu_ref[:, sl])
        act_ref[:, sl] = (gate * jax.nn.sigmoid(gate) * up).astype(BF16)
    return x + 0.5 * _dot(act_ref[...], wd_ref[...])


def _ffn_kernel(x_ref, g_ref, wg_ref, wu_ref, wd_ref, o_ref, act_ref):
    o_ref[...] = _swiglu_residual(x_ref[...], g_ref, wg_ref, wu_ref, wd_ref, act_ref)


def _ffn(x, g, wg, wu, wd, tm):
    t = x.shape[0]
    return pl.pallas_call(
        _ffn_kernel,
        grid=(t // tm,),
        in_specs=[
            pl.BlockSpec((tm, D_MODEL), lambda i: (i, 0)),
            _resident((1, D_MODEL)),
            _resident((D_MODEL, D_FF)),
            _resident((D_MODEL, D_FF)),
            _resident((D_FF, D_MODEL)),
        ],
        out_specs=pl.BlockSpec((tm, D_MODEL), lambda i: (i, 0)),
        out_shape=jax.ShapeDtypeStruct((t, D_MODEL), F32),
        scratch_shapes=[pltpu.VMEM((tm, D_FF), BF16)],
        compiler_params=_params(1),
        name="ffn1",
    )(x, g, wg, wu, wd)


def _log_sigmoid(z):
    return jnp.minimum(z, 0.0) - jnp.log1p(jnp.exp(-jnp.abs(z)))


def _proj_kernel(x_ref, g_ref, wa_ref, wg_ref, wlr_ref, wgate_ref, up_ref, bias_ref,
                 qgain_ref, kgain_ref, hmean_ref,
                 qa_ref, ka_ref, va_ref, qg_ref, kg_ref, vg_ref, rg_ref, la_ref, gate_ref):
    h = _rmsnorm(x_ref[...], g_ref[...]).astype(BF16)
    hmean = hmean_ref[...]

    def head_norm(x, gain):
        ms = _dot((x * x).astype(BF16), hmean)
        return x * lax.rsqrt(ms + EPS) * gain

    aw = N_GROUPS * A_WIDTH
    for g in range(N_GROUPS):
        sl = slice(g * A_WIDTH, (g + 1) * A_WIDTH)
        q = _dot(h, wa_ref[:, g * A_WIDTH:(g + 1) * A_WIDTH])
        k = _dot(h, wa_ref[:, aw + g * A_WIDTH:aw + (g + 1) * A_WIDTH])
        v = _dot(h, wa_ref[:, 2 * aw + g * A_WIDTH:2 * aw + (g + 1) * A_WIDTH])
        qa_ref[:, sl] = (head_norm(q, qgain_ref[:, sl]) * (A_HEAD_DIM ** -0.5)).astype(BF16)
        ka_ref[:, sl] = head_norm(k, kgain_ref[:, sl])
        va_ref[:, sl] = v

    qg_ref[...] = (_dot(h, wg_ref[:, 0:G_KW]) * (G_DK ** -0.5)).astype(BF16)
    kg_ref[...] = _dot(h, wg_ref[:, G_KW:2 * G_KW]).astype(BF16)
    for c in range(G_VW // N_CHUNK):
        sl = slice(c * N_CHUNK, (c + 1) * N_CHUNK)
        off = 2 * G_KW + c * N_CHUNK
        vg_ref[:, sl] = _dot(h, wg_ref[:, off:off + N_CHUNK]).astype(BF16)
        rg_ref[:, sl] = _dot(h, wg_ref[:, G_VW + off:G_VW + off + N_CHUNK]).astype(BF16)

    lr = _dot(h, wlr_ref[...]).astype(BF16)
    z = _dot(lr, up_ref[...]) + bias_ref[...]
    la_ref[...] = _log_sigmoid(z) * (1.0 / G_TAU)

    for c in range(2 * D_MODEL // N_CHUNK):
        sl = slice(c * N_CHUNK, (c + 1) * N_CHUNK)
        gate_ref[:, sl] = jax.nn.sigmoid(_dot(h, wgate_ref[:, sl])).astype(BF16)


def _proj(x1, g, wa, wg, wlr, wgate, up, bias, qgain, kgain, hmean, tm):
    t = x1.shape[0]
    widths = [(N_GROUPS * A_WIDTH, BF16), (N_GROUPS * A_WIDTH, F32), (N_GROUPS * A_WIDTH, F32),
              (G_KW, BF16), (G_KW, BF16), (G_VW, BF16), (G_VW, BF16), (G_KW, F32),
              (2 * D_MODEL, BF16)]
    consts = [g, wa, wg, wlr, wgate, up, bias, qgain, kgain, hmean]
    return pl.pallas_call(
        _proj_kernel,
        grid=(t // tm,),
        in_specs=[pl.BlockSpec((tm, D_MODEL), lambda i: (i, 0))]
        + [_resident(c.shape) for c in consts],
        out_specs=[pl.BlockSpec((tm, w), lambda i: (i, 0)) for w, _ in widths],
        out_shape=[jax.ShapeDtypeStruct((t, w), dt) for w, dt in widths],
        compiler_params=_params(1),
        name="in_proj",
    )(x1, *consts)


def _log2(n):
    assert n > 0 and n & (n - 1) == 0, n
    return n.bit_length() - 1


def _head_masks(rows):
    lane = lax.broadcasted_iota(jnp.int32, (rows, A_WIDTH), 1)
    return [(lane >> _log2(A_HEAD_DIM)) == h for h in range(A_HEADS)]


def _attn_prompt_kernel(q_ref, k_ref, v_ref, o_ref, lse_ref, *, nb):
    def block(q, kb, vb, first):
        hmask = _head_masks(BAND)
        n_keys = kb.shape[0]
        row = lax.broadcasted_iota(jnp.int32, (BAND, n_keys), 0)
        col = lax.broadcasted_iota(jnp.int32, (BAND, n_keys), 1)
        if first:
            valid = col <= row
        else:
            valid = ((col < BAND) & (col >= row)) | ((col >= BAND) & (col - BAND <= row))
        o_acc = jnp.zeros((BAND, A_WIDTH), F32)
        lse_acc = jnp.zeros((BAND, A_WIDTH), F32)
        for h in range(A_HEADS):
            qh = q * hmask[h].astype(F32).astype(BF16)
            s = jnp.where(valid, _dot_nt(qh, kb), -jnp.inf)
            m = jnp.max(s, axis=-1, keepdims=True)
            p = jnp.exp(s - m)
            den = jnp.sum(p, axis=-1, keepdims=True)
            oh = _dot(p.astype(BF16), vb) * (1.0 / den)
            o_acc = jnp.where(hmask[h], oh, o_acc)
            lse_acc = jnp.where(hmask[h], m + jnp.log(den), lse_acc)
        return o_acc, lse_acc

    o0, l0 = block(q_ref[0:BAND, :], k_ref[0:BAND, :].astype(BF16),
                   v_ref[0:BAND, :].astype(BF16), True)
    o_ref[0:BAND, :] = o0
    lse_ref[0:BAND, :] = l0

    def step(j, carry):
        cur = pl.multiple_of(j * BAND, BAND)
        prev = pl.multiple_of((j - 1) * BAND, BAND)
        o, l = block(q_ref[pl.ds(cur, BAND), :],
                     k_ref[pl.ds(prev, 2 * BAND), :].astype(BF16),
                     v_ref[pl.ds(prev, 2 * BAND), :].astype(BF16), False)
        o_ref[pl.ds(cur, BAND), :] = o
        lse_ref[pl.ds(cur, BAND), :] = l
        return carry

    if nb > 1:
        lax.fori_loop(1, nb, step, 0)


def _attn_prompt_group(qa, ka, va, g, d, batch, seq):
    n = seq // d
    w3 = N_GROUPS * A_WIDTH
    qv = qa.reshape(batch, n, d * w3)
    kv = ka.reshape(batch, n, d * w3)
    vv = va.reshape(batch, n, d * w3)
    in_spec = pl.BlockSpec((None, n, A_WIDTH), lambda b, r: (b, 0, r * N_GROUPS + g))
    out_spec = pl.BlockSpec((None, n, A_WIDTH), lambda b, r: (b, 0, r))
    o, lse = pl.pallas_call(
        functools.partial(_attn_prompt_kernel, nb=n // BAND),
        grid=(batch, d),
        in_specs=[in_spec, in_spec, in_spec],
        out_specs=[out_spec, out_spec],
        out_shape=[jax.ShapeDtypeStruct((batch, n, d * A_WIDTH), F32)] * 2,
        compiler_params=_params(2),
        name=f"attn_prompt_d{d}",
    )(qv, kv, vv)
    return o.reshape(batch * seq, A_WIDTH), lse.reshape(batch * seq, A_WIDTH)


def _attn_sample_kernel(q_ref, k_ref, v_ref, c1_ref, c4_ref, c16_ref, o_ref, *, t_new):
    hmask = _head_masks(t_new)
    rows = A_HEADS * t_new
    row_t = lax.broadcasted_iota(jnp.int32, (rows, LANES), 0) & (t_new - 1)
    col = lax.broadcasted_iota(jnp.int32, (rows, LANES), 1)
    pad = jnp.zeros((LANES - t_new, A_WIDTH), F32)

    num = jnp.zeros((t_new, A_WIDTH), F32)
    den_all = jnp.zeros((t_new, A_WIDTH), F32)
    m_all = jnp.full((t_new, A_WIDTH), -jnp.inf, F32)
    for g, (d, c_ref) in enumerate(zip(SWA_DILATIONS, (c1_ref, c4_ref, c16_ref))):
        sl = slice(g * A_WIDTH, (g + 1) * A_WIDTH)
        qm = jnp.concatenate(
            [q_ref[:, sl].astype(F32) * hmask[h].astype(F32) for h in range(A_HEADS)],
            axis=0).astype(BF16)
        k_new = jnp.concatenate([k_ref[:, sl], pad], axis=0).astype(BF16)
        v_new = jnp.concatenate([v_ref[:, sl], pad], axis=0).astype(BF16)
        n_cls = min(d, t_new)
        scores, values = [], []
        for c in range(n_cls):
            kc = c_ref[:, c * 2 * A_WIDTH:c * 2 * A_WIDTH + A_WIDTH].astype(BF16)
            vc = c_ref[:, c * 2 * A_WIDTH + A_WIDTH:(c + 1) * 2 * A_WIDTH].astype(BF16)
            ok = ((row_t & (d - 1)) == c) & (col >= (row_t >> _log2(d)))
            scores.append(jnp.where(ok, _dot_nt(qm, kc), -jnp.inf))
            values.append(vc)
        ok_new = (col <= row_t) & (((row_t - col) & (d - 1)) == 0)
        scores.append(jnp.where(ok_new, _dot_nt(qm, k_new), -jnp.inf))
        values.append(v_new)
        m = functools.reduce(jnp.maximum, [jnp.max(s, axis=-1, keepdims=True) for s in scores])
        ps = [jnp.exp(s - m) for s in scores]
        den = functools.reduce(jnp.add, [jnp.sum(p, axis=-1, keepdims=True) for p in ps])
        acc = functools.reduce(jnp.add, [_dot(p.astype(BF16), v) for p, v in zip(ps, values)])
        o_g = jnp.zeros((t_new, A_WIDTH), F32)
        lse_g = jnp.zeros((t_new, A_WIDTH), F32)
        for h in range(A_HEADS):
            rs = slice(h * t_new, (h + 1) * t_new)
            o_g = jnp.where(hmask[h], acc[rs] * (1.0 / den[rs]), o_g)
            lse_g = jnp.where(hmask[h], m[rs] + jnp.log(den[rs]), lse_g)
        m_next = jnp.maximum(m_all, lse_g)
        scale_old = jnp.exp(m_all - m_next)
        w = jnp.exp(lse_g - m_next)
        num = num * scale_old + w * o_g
        den_all = den_all * scale_old + w
        m_all = m_next
    o_ref[...] = num * (1.0 / den_all)


def _attn_sample(qa, ka, va, c128, c512, c2048, batch, t_new):
    w3 = N_GROUPS * A_WIDTH
    row = 2 * A_WIDTH
    tok_spec = pl.BlockSpec((None, t_new, w3), lambda b: (b, 0, 0))
    cache_views = [c128.reshape(batch, BAND, row), c512.reshape(batch, BAND, 4 * row),
                   c2048.reshape(batch, BAND, 16 * row)]
    cache_specs = [pl.BlockSpec((None, BAND, min(d, t_new) * row), lambda b: (b, 0, 0))
                   for d in SWA_DILATIONS]
    out = pl.pallas_call(
        functools.partial(_attn_sample_kernel, t_new=t_new),
        grid=(batch,),
        in_specs=[tok_spec, tok_spec, tok_spec] + cache_specs,
        out_specs=pl.BlockSpec((None, t_new, A_WIDTH), lambda b: (b, 0, 0)),
        out_shape=jax.ShapeDtypeStruct((batch, t_new, A_WIDTH), F32),
        compiler_params=_params(1),
        name="attn_sample",
    )(qa.reshape(batch, t_new, w3), ka.reshape(batch, t_new, w3), va.reshape(batch, t_new, w3),
      *cache_views)
    return out.reshape(batch * t_new, A_WIDTH)


def _split3(x):
    hi = x.astype(BF16)
    r = x - hi.astype(F32)
    mid = r.astype(BF16)
    lo = (r - mid.astype(F32)).astype(BF16)
    return hi, mid, lo


def _gla_out(o, rg, gain):
    on = o * lax.rsqrt(jnp.mean(o * o, axis=-1, keepdims=True) + EPS) * gain
    return (on * rg * jax.nn.sigmoid(rg)).astype(BF16)


def _col_bcast(vec_row, width):
    sq = jnp.broadcast_to(vec_row, (LANES, LANES)).T
    return jnp.concatenate([sq] * (width // LANES), axis=1)


def _gla_prompt_kernel(q_ref, k_ref, v_ref, la_ref, rg_ref, gn_ref, o_ref, s_ref, state_ref,
                       *, n_blocks):
    cb, sub = GLA_BLOCK, GLA_SUB
    row = lax.broadcasted_iota(jnp.int32, (cb, cb), 0)
    col = lax.broadcasted_iota(jnp.int32, (cb, cb), 1)
    tri = (row >= col).astype(BF16)
    causal = col <= row
    n_sub = cb // sub
    state_ref[...] = jnp.zeros_like(state_ref)

    def step(i, carry):
        st = pl.multiple_of(i * cb, cb)
        la = la_ref[pl.ds(st, cb), :]
        hi, mid, lo = _split3(la)
        b = _dot(tri, hi) + _dot(tri, mid) + _dot(tri, lo)
        q = q_ref[pl.ds(st, cb), :].astype(F32)
        k = k_ref[pl.ds(st, cb), :].astype(F32)
        v = v_ref[pl.ds(st, cb), :]
        b_last = b[cb - 1:cb, :]
        state = state_ref[...]

        out = _dot((q * jnp.exp(b)).astype(BF16), state.astype(BF16))

        sub_end = [b[(j + 1) * sub - 1:(j + 1) * sub, :] for j in range(n_sub)]
        own_end = jnp.concatenate(
            [jnp.broadcast_to(e, (sub, G_DK)) for e in sub_end], axis=0)
        kk = k * jnp.exp(own_end - b)
        q_parts, k_parts = [], []
        for j in range(n_sub):
            lo_row = j * sub
            qj = q[lo_row:, :] * jnp.exp(b[lo_row:, :] - sub_end[j])
            q_parts.append(jnp.concatenate(
                [jnp.zeros((lo_row, G_DK), F32), qj], axis=0).astype(BF16) if lo_row else
                qj.astype(BF16))
            in_sub = (row >= lo_row) & (row < lo_row + sub)
            k_parts.append(jnp.where(in_sub, kk, 0.0).astype(BF16))
        att = _dot_nt(jnp.concatenate(q_parts, axis=1), jnp.concatenate(k_parts, axis=1))
        att = jnp.where(causal, att, 0.0)
        out = out + _dot(att.astype(BF16), v)

        kd_t = (k * jnp.exp(b_last - b)).T.astype(BF16)
        state_ref[...] = state * _col_bcast(jnp.exp(b_last), G_DV) + _dot(kd_t, v)

        o_ref[pl.ds(st, cb), :] = _gla_out(out, rg_ref[pl.ds(st, cb), :].astype(F32), gn_ref[...])
        return carry

    lax.fori_loop(0, n_blocks, step, 0)
    s_ref[...] = state_ref[...]


def _gla_prompt(qg, kg, vg, la, rg, gn, batch, seq):
    def spec(width):
        return pl.BlockSpec((None, seq, width), lambda b, h: (b, 0, h))
    o, s = pl.pallas_call(
        functools.partial(_gla_prompt_kernel, n_blocks=seq // GLA_BLOCK),
        grid=(batch, G_HEADS),
        in_specs=[spec(G_DK), spec(G_DK), spec(G_DV), spec(G_DK), spec(G_DV),
                  _resident((1, G_DV))],
        out_specs=[spec(G_DV),
                   pl.BlockSpec((None, None, G_DK, G_DV), lambda b, h: (b, h, 0, 0))],
        out_shape=[jax.ShapeDtypeStruct((batch, seq, G_VW), BF16),
                   jax.ShapeDtypeStruct((batch, G_HEADS, G_DK, G_DV), F32)],
        scratch_shapes=[pltpu.VMEM((G_DK, G_DV), F32)],
        compiler_params=_params(2),
        name="gla_prompt",
    )(qg.reshape(batch, seq, G_KW), kg.reshape(batch, seq, G_KW), vg.reshape(batch, seq, G_VW),
      la.reshape(batch, seq, G_KW), rg.reshape(batch, seq, G_VW), gn)
    return o.reshape(batch * seq, G_VW), s


def _gla_sample_kernel(q_ref, k_ref, v_ref, la_ref, rg_ref, gn_ref, s0_ref, o_ref, s_ref,
                       *, bb, t_new):
    trow = lax.broadcasted_iota(jnp.int32, (t_new, G_DK), 0)
    arow = lax.broadcasted_iota(jnp.int32, (t_new, LANES), 0)
    acol = lax.broadcasted_iota(jnp.int32, (t_new, LANES), 1)
    causal = acol <= arow
    zk = jnp.zeros((LANES - t_new, G_DK), F32)
    zv = jnp.zeros((LANES - t_new, G_DV), F32)

    def seq_step(i, carry):
        for h in range(G_HEADS):
            ks = slice(h * G_DK, (h + 1) * G_DK)
            vs = slice(h * G_DV, (h + 1) * G_DV)
            b = la_ref[i, :, ks]
            shift = 1
            while shift < t_new:
                b = b + jnp.where(trow >= shift, pltpu.roll(b, shift, 0), 0.0)
                shift *= 2
            q = q_ref[i, :, ks].astype(F32)
            k = k_ref[i, :, ks].astype(F32)
            v = jnp.concatenate([v_ref[i, :, vs].astype(F32), zv], axis=0).astype(BF16)
            b_last = b[t_new - 1:t_new, :]
            s0 = s0_ref[i, h]
            qe = (q * jnp.exp(b)).astype(BF16)
            ke = jnp.concatenate([k * jnp.exp(-b), zk], axis=0).astype(BF16)
            att = jnp.where(causal, _dot_nt(qe, ke), 0.0)
            out = _dot(att.astype(BF16), v) + _dot(qe, s0.astype(BF16))
            kd = jnp.concatenate([k * jnp.exp(b_last - b), zk], axis=0)
            s_ref[i, h] = s0 * _col_bcast(jnp.exp(b_last), G_DV) + _dot(kd.T.astype(BF16), v)
            o_ref[i, :, vs] = _gla_out(out, rg_ref[i, :, vs].astype(F32), gn_ref[...])
        return carry

    lax.fori_loop(0, bb, seq_step, 0)


def _gla_sample(qg, kg, vg, la, rg, gn, s0, batch, t_new, bb):
    def spec(width):
        return pl.BlockSpec((bb, t_new, width), lambda b: (b, 0, 0))
    st_spec = pl.BlockSpec((bb, G_HEADS, G_DK, G_DV), lambda b: (b, 0, 0, 0))
    o, s = pl.pallas_call(
        functools.partial(_gla_sample_kernel, bb=bb, t_new=t_new),
        grid=(batch // bb,),
        in_specs=[spec(G_KW), spec(G_KW), spec(G_VW), spec(G_KW), spec(G_VW),
                  _resident((1, G_DV)), st_spec],
        out_specs=[spec(G_VW), st_spec],
        out_shape=[jax.ShapeDtypeStruct((batch, t_new, G_VW), BF16),
                   jax.ShapeDtypeStruct((batch, G_HEADS, G_DK, G_DV), F32)],
        compiler_params=_params(1),
        name="gla_sample",
    )(qg.reshape(batch, t_new, G_KW), kg.reshape(batch, t_new, G_KW),
      vg.reshape(batch, t_new, G_VW), la.reshape(batch, t_new, G_KW),
      rg.reshape(batch, t_new, G_VW), gn, s0)
    return o.reshape(batch * t_new, G_VW), s


def _out_kernel(*refs, n_attn):
    x_ref = refs[0]
    attn_refs = refs[1:1 + 2 * n_attn] if n_attn > 1 else refs[1:2]
    rest = refs[1 + (2 * n_attn if n_attn > 1 else 1):]
    (gb_ref, gate_ref, wa_ref, wb_ref, wo_ref, g_ref, wg_ref, wu_ref, wd_ref,
     o_ref, act_ref) = rest

    if n_attn > 1:
        outs = [r[...] for r in attn_refs[:n_attn]]
        lses = [r[...] for r in attn_refs[n_attn:]]
        m = functools.reduce(jnp.maximum, lses)
        ws = [jnp.exp(l - m) for l in lses]
        a = functools.reduce(jnp.add, [w * o for w, o in zip(ws, outs)])
        a = a * (1.0 / functools.reduce(jnp.add, ws))
    else:
        a = attn_refs[0][...]

    a_out = _dot(a.astype(BF16), wa_ref[...])
    b_out = _dot(gb_ref[...], wb_ref[...])
    mix = (gate_ref[:, 0:D_MODEL].astype(F32) * a_out
           + gate_ref[:, D_MODEL:2 * D_MODEL].astype(F32) * b_out)
    x2 = x_ref[...] + _dot(mix.astype(BF16), wo_ref[...])
    o_ref[...] = _swiglu_residual(x2, g_ref, wg_ref, wu_ref, wd_ref, act_ref)


def _out(x1, attn, gb, gates, wa, wb, wo, g, wg, wu, wd, tm):
    t = x1.shape[0]
    n_attn = len(attn) // 2 if len(attn) > 1 else 1

    def tok(width):
        return pl.BlockSpec((tm, width), lambda i: (i, 0))
    consts = [wa, wb, wo, g, wg, wu, wd]
    return pl.pallas_call(
        functools.partial(_out_kernel, n_attn=n_attn),
        grid=(t // tm,),
        in_specs=[tok(D_MODEL)] + [tok(A_WIDTH)] * len(attn) + [tok(G_VW), tok(2 * D_MODEL)]
        + [_resident(c.shape) for c in consts],
        out_specs=tok(D_MODEL),
        out_shape=jax.ShapeDtypeStruct((t, D_MODEL), F32),
        scratch_shapes=[pltpu.VMEM((tm, D_FF), BF16)],
        compiler_params=_params(1),
        name="out_ffn2",
    )(x1, *attn, gb, gates, *consts)


def _cache_rows(ka, va, g, batch, seq, keep):
    k = ka.reshape(batch, seq, N_GROUPS, A_HEADS, A_HEAD_DIM)[:, seq - keep:, g]
    v = va.reshape(batch, seq, N_GROUPS, A_HEADS, A_HEAD_DIM)[:, seq - keep:, g]
    return jnp.stack([k, v], axis=2)


def kernel(x_prompt, x_sample, cache_swa128_kv, cache_swa512_kv, cache_swa2048_kv, state_gla,
           norm_ffn1, ffn1_gate, ffn1_up, ffn1_down, norm_mix, w_in, a_q_norm, a_k_norm,
           g_alpha_up, g_alpha_bias, g_out_norm, w_a_out, w_b_out, w_out,
           norm_ffn2, ffn2_gate, ffn2_up, ffn2_down):
    batch, seq, _ = x_prompt.shape
    dec_batch, dec_seq, _ = x_sample.shape

    aw = N_GROUPS * A_WIDTH
    o_g = 3 * aw
    o_lr = o_g + 2 * G_KW + 2 * G_VW
    o_gate = o_lr + G_LOWRANK
    w_attn = w_in[:, :o_g].astype(BF16)
    w_gla = w_in[:, o_g:o_lr].astype(BF16)
    w_lr = jnp.pad(w_in[:, o_lr:o_gate], ((0, 0), (0, LANES - G_LOWRANK))).astype(BF16)
    w_gate = w_in[:, o_gate:].astype(BF16)
    up = jnp.pad(g_alpha_up, ((0, LANES - G_LOWRANK), (0, 0))).astype(BF16)
    bias = g_alpha_bias.reshape(1, G_KW)
    qgain = jnp.tile(a_q_norm, (1, A_HEADS)).reshape(1, aw)
    kgain = jnp.tile(a_k_norm, (1, A_HEADS)).reshape(1, aw)
    lane = jnp.arange(A_WIDTH) // A_HEAD_DIM
    hmean = jnp.where(lane[:, None] == lane[None, :], 1.0 / A_HEAD_DIM, 0.0).astype(BF16)
    gn = g_out_norm.reshape(1, G_DV)
    ffn1 = (norm_ffn1.reshape(1, D_MODEL), ffn1_gate.astype(BF16), ffn1_up.astype(BF16),
            ffn1_down.astype(BF16))
    ffn2 = (norm_ffn2.reshape(1, D_MODEL), ffn2_gate.astype(BF16), ffn2_up.astype(BF16),
            ffn2_down.astype(BF16))
    proj_w = (norm_mix.reshape(1, D_MODEL), w_attn, w_gla, w_lr, w_gate, up, bias,
              qgain, kgain, hmean)
    out_w = (w_a_out.astype(BF16), w_b_out.astype(BF16), w_out.astype(BF16))

    def front(x, tm):
        x1 = _ffn(x, *ffn1, tm=tm)
        return (x1,) + tuple(_proj(x1, *proj_w, tm=tm))

    x1p, qa, ka, va, qg, kg, vg, rg, la, gates = front(x_prompt.reshape(batch * seq, D_MODEL), 512)
    attn = [_attn_prompt_group(qa, ka, va, g, d, batch, seq)
            for g, d in enumerate(SWA_DILATIONS)]
    attn = [o for o, _ in attn] + [l for _, l in attn]
    gbp, state_p = _gla_prompt(qg, kg, vg, la, rg, gn, batch, seq)
    y_prompt = _out(x1p, attn, gbp, gates, *out_w, *ffn2, tm=256).reshape(batch, seq, D_MODEL)
    rows_p = [_cache_rows(ka, va, g, batch, seq, min(BAND * d, seq))
              for g, d in enumerate(SWA_DILATIONS)]

    x1s, qa, ka, va, qg, kg, vg, rg, la, gates = front(
        x_sample.reshape(dec_batch * dec_seq, D_MODEL), 512)
    a_s = _attn_sample(qa, ka, va, cache_swa128_kv, cache_swa512_kv, cache_swa2048_kv,
                       dec_batch, dec_seq)
    gbs, state_s = _gla_sample(qg, kg, vg, la, rg, gn, state_gla, dec_batch, dec_seq, bb=8)
    y_sample = _out(x1s, [a_s], gbs, gates, *out_w, *ffn2, tm=256).reshape(
        dec_batch, dec_seq, D_MODEL)
    rows_s = [_cache_rows(ka, va, g, dec_batch, dec_seq, dec_seq) for g in range(N_GROUPS)]

    return (y_prompt, y_sample, rows_p[0], rows_p[1], rows_p[2], state_p,
            rows_s[0], rows_s[1], rows_s[2], state_s)
```

```python
import functools

import jax
import jax.numpy as jnp
from jax import lax
from jax.experimental import pallas as pl
from jax.experimental.pallas import tpu as pltpu

F32 = jnp.float32
BF16 = jnp.bfloat16

D_MODEL = 1024
D_FF = 2816
SWA_DILATIONS = (1, 4, 16)
N_GROUPS = 3
A_HEADS = 4
A_HEAD_DIM = 64
A_WIDTH = A_HEADS * A_HEAD_DIM
BAND = 128
G_HEADS = 4
G_DK = 128
G_DV = 256
G_KW = G_HEADS * G_DK
G_VW = G_HEADS * G_DV
G_LOWRANK = 16
G_TAU = 16.0
EPS = 1e-6

LANES = 128
GLA_BLOCK = 128
GLA_SUB = 32
FF_CHUNK = 256
N_CHUNK = 512
VMEM_LIMIT = 58 * 1024 * 1024


def _resident(shape):
    nd = len(shape)
    return pl.BlockSpec(shape, lambda *_: (0,) * nd, pipeline_mode=pl.Buffered(1))


def _params(n_axes):
    return pltpu.CompilerParams(
        dimension_semantics=("arbitrary",) * n_axes, vmem_limit_bytes=VMEM_LIMIT)


def _dot(a, b):
    return jnp.dot(a, b, preferred_element_type=F32)


def _dot_nt(a, b):
    return lax.dot_general(a, b, (((1,), (1,)), ((), ())), preferred_element_type=F32)


def _rmsnorm(x, g):
    return x * lax.rsqrt(jnp.mean(x * x, axis=-1, keepdims=True) + EPS) * g


def _swiglu_residual(x, g_ref, wg_ref, wu_ref, wd_ref, act_ref):
    h = _rmsnorm(x, g_ref[...]).astype(BF16)
    for c in range(D_FF // FF_CHUNK):
        sl = slice(c * FF_CHUNK, (c + 1) * FF_CHUNK)
        gate = _dot(h, wg_ref[:, sl])
        up = _dot(h, wu_ref[:, sl])
        act_ref[:, sl] = (gate * jax.nn.sigmoid(gate) * up).astype(BF16)
    return x + 0.5 * _dot(act_ref[...], wd_ref[...])


def _ffn_kernel(x_ref, g_ref, wg_ref, wu_ref, wd_ref, o_ref, act_ref):
    o_ref[...] = _swiglu_residual(x_ref[...], g_ref, wg_ref, wu_ref, wd_ref, act_ref)


def _ffn(x, g, wg, wu, wd, tm):
    t = x.shape[0]
    return pl.pallas_call(
        _ffn_kernel,
        grid=(t // tm,),
        in_specs=[
            pl.BlockSpec((tm, D_MODEL), lambda i: (i, 0)),
            _resident((1, D_MODEL)),
            _resident((D_MODEL, D_FF)),
            _resident((D_MODEL, D_FF)),
            _resident((D_FF, D_MODEL)),
        ],
        out_specs=pl.BlockSpec((tm, D_MODEL), lambda i: (i, 0)),
        out_shape=jax.ShapeDtypeStruct((t, D_MODEL), F32),
        scratch_shapes=[pltpu.VMEM((tm, D_FF), BF16)],
        compiler_params=_params(1),
        name="ffn1",
    )(x, g, wg, wu, wd)


def _log_sigmoid(z):
    return jnp.minimum(z, 0.0) - jnp.log1p(jnp.exp(-jnp.abs(z)))


def _proj_kernel(x_ref, g_ref, wa_ref, wg_ref, wlr_ref, wgate_ref, up_ref, bias_ref,
                 qgain_ref, kgain_ref, hmean_ref,
                 qa_ref, ka_ref, va_ref, qg_ref, kg_ref, vg_ref, rg_ref, la_ref, gate_ref):
    h = _rmsnorm(x_ref[...], g_ref[...]).astype(BF16)
    hmean = hmean_ref[...]

    def head_norm(x, gain):
        ms = _dot((x * x).astype(BF16), hmean)
        return x * lax.rsqrt(ms + EPS) * gain

    aw = N_GROUPS * A_WIDTH
    for g in range(N_GROUPS):
        sl = slice(g * A_WIDTH, (g + 1) * A_WIDTH)
        q = _dot(h, wa_ref[:, g * A_WIDTH:(g + 1) * A_WIDTH])
        k = _dot(h, wa_ref[:, aw + g * A_WIDTH:aw + (g + 1) * A_WIDTH])
        v = _dot(h, wa_ref[:, 2 * aw + g * A_WIDTH:2 * aw + (g + 1) * A_WIDTH])
        qa_ref[:, sl] = (head_norm(q, qgain_ref[:, sl]) * (A_HEAD_DIM ** -0.5)).astype(BF16)
        ka_ref[:, sl] = head_norm(k, kgain_ref[:, sl])
        va_ref[:, sl] = v

    qg_ref[...] = (_dot(h, wg_ref[:, 0:G_KW]) * (G_DK ** -0.5)).astype(BF16)
    kg_ref[...] = _dot(h, wg_ref[:, G_KW:2 * G_KW]).astype(BF16)
    for c in range(G_VW // N_CHUNK):
        sl = slice(c * N_CHUNK, (c + 1) * N_CHUNK)
        off = 2 * G_KW + c * N_CHUNK
        vg_ref[:, sl] = _dot(h, wg_ref[:, off:off + N_CHUNK]).astype(BF16)
        rg_ref[:, sl] = _dot(h, wg_ref[:, G_VW + off:G_VW + off + N_CHUNK]).astype(BF16)

    lr = _dot(h, wlr_ref[...]).astype(BF16)
    z = _dot(lr, up_ref[...]) + bias_ref[...]
    la_ref[...] = _log_sigmoid(z) * (1.0 / G_TAU)

    for c in range(2 * D_MODEL // N_CHUNK):
        sl = slice(c * N_CHUNK, (c + 1) * N_CHUNK)
        gate_ref[:, sl] = jax.nn.sigmoid(_dot(h, wgate_ref[:, sl])).astype(BF16)


def _proj(x1, g, wa, wg, wlr, wgate, up, bias, qgain, kgain, hmean, tm):
    t = x1.shape[0]
    widths = [(N_GROUPS * A_WIDTH, BF16), (N_GROUPS * A_WIDTH, F32), (N_GROUPS * A_WIDTH, F32),
              (G_KW, BF16), (G_KW, BF16), (G_VW, BF16), (G_VW, BF16), (G_KW, F32),
              (2 * D_MODEL, BF16)]
    consts = [g, wa, wg, wlr, wgate, up, bias, qgain, kgain, hmean]
    return pl.pallas_call(
        _proj_kernel,
        grid=(t // tm,),
        in_specs=[pl.BlockSpec((tm, D_MODEL), lambda i: (i, 0))]
        + [_resident(c.shape) for c in consts],
        out_specs=[pl.BlockSpec((tm, w), lambda i: (i, 0)) for w, _ in widths],
        out_shape=[jax.ShapeDtypeStruct((t, w), dt) for w, dt in widths],
        compiler_params=_params(1),
        name="in_proj",
    )(x1, *consts)


def _log2(n):
    assert n > 0 and n & (n - 1) == 0, n
    return n.bit_length() - 1


def _head_masks(rows):
    lane = lax.broadcasted_iota(jnp.int32, (rows, A_WIDTH), 1)
    return [(lane >> _log2(A_HEAD_DIM)) == h for h in range(A_HEADS)]


def _attn_prompt_kernel(q_ref, k_ref, v_ref, o_ref, lse_ref, *, nb):
    def block(q, kb, vb, first):
        hmask = _head_masks(BAND)
        n_keys = kb.shape[0]
        row = lax.broadcasted_iota(jnp.int32, (BAND, n_keys), 0)
        col = lax.broadcasted_iota(jnp.int32, (BAND, n_keys), 1)
        if first:
            valid = col <= row
        else:
            valid = ((col < BAND) & (col >= row)) | ((col >= BAND) & (col - BAND <= row))
        o_acc = jnp.zeros((BAND, A_WIDTH), F32)
        lse_acc = jnp.zeros((BAND, A_WIDTH), F32)
        for h in range(A_HEADS):
            qh = q * hmask[h].astype(F32).astype(BF16)
            s = jnp.where(valid, _dot_nt(qh, kb), -jnp.inf)
            m = jnp.max(s, axis=-1, keepdims=True)
            p = jnp.exp(s - m)
            den = jnp.sum(p, axis=-1, keepdims=True)
            oh = _dot(p.astype(BF16), vb) * (1.0 / den)
            o_acc = jnp.where(hmask[h], oh, o_acc)
            lse_acc = jnp.where(hmask[h], m + jnp.log(den), lse_acc)
        return o_acc, lse_acc

    o0, l0 = block(q_ref[0:BAND, :], k_ref[0:BAND, :].astype(BF16),
                   v_ref[0:BAND, :].astype(BF16), True)
    o_ref[0:BAND, :] = o0
    lse_ref[0:BAND, :] = l0

    def step(j, carry):
        cur = pl.multiple_of(j * BAND, BAND)
        prev = pl.multiple_of((j - 1) * BAND, BAND)
        o, l = block(q_ref[pl.ds(cur, BAND), :],
                     k_ref[pl.ds(prev, 2 * BAND), :].astype(BF16),
                     v_ref[pl.ds(prev, 2 * BAND), :].astype(BF16), False)
        o_ref[pl.ds(cur, BAND), :] = o
        lse_ref[pl.ds(cur, BAND), :] = l
        return carry

    if nb > 1:
        lax.fori_loop(1, nb, step, 0)


def _attn_prompt_group(qa, ka, va, g, d, batch, seq):
    n = seq // d
    w3 = N_GROUPS * A_WIDTH
    qv = qa.reshape(batch, n, d * w3)
    kv = ka.reshape(batch, n, d * w3)
    vv = va.reshape(batch, n, d * w3)
    in_spec = pl.BlockSpec((None, n, A_WIDTH), lambda b, r: (b, 0, r * N_GROUPS + g))
    out_spec = pl.BlockSpec((None, n, A_WIDTH), lambda b, r: (b, 0, r))
    o, lse = pl.pallas_call(
        functools.partial(_attn_prompt_kernel, nb=n // BAND),
        grid=(batch, d),
        in_specs=[in_spec, in_spec, in_spec],
        out_specs=[out_spec, out_spec],
        out_shape=[jax.ShapeDtypeStruct((batch, n, d * A_WIDTH), F32)] * 2,
        compiler_params=_params(2),
        name=f"attn_prompt_d{d}",
    )(qv, kv, vv)
    return o.reshape(batch * seq, A_WIDTH), lse.reshape(batch * seq, A_WIDTH)


def _attn_sample_kernel(q_ref, k_ref, v_ref, c1_ref, c4_ref, c16_ref, o_ref, *, t_new):
    hmask = _head_masks(t_new)
    rows = A_HEADS * t_new
    row_t = lax.broadcasted_iota(jnp.int32, (rows, LANES), 0) & (t_new - 1)
    col = lax.broadcasted_iota(jnp.int32, (rows, LANES), 1)
    pad = jnp.zeros((LANES - t_new, A_WIDTH), F32)

    num = jnp.zeros((t_new, A_WIDTH), F32)
    den_all = jnp.zeros((t_new, A_WIDTH), F32)
    m_all = jnp.full((t_new, A_WIDTH), -jnp.inf, F32)
    for g, (d, c_ref) in enumerate(zip(SWA_DILATIONS, (c1_ref, c4_ref, c16_ref))):
        sl = slice(g * A_WIDTH, (g + 1) * A_WIDTH)
        qm = jnp.concatenate(
            [q_ref[:, sl].astype(F32) * hmask[h].astype(F32) for h in range(A_HEADS)],
            axis=0).astype(BF16)
        k_new = jnp.concatenate([k_ref[:, sl], pad], axis=0).astype(BF16)
        v_new = jnp.concatenate([v_ref[:, sl], pad], axis=0).astype(BF16)
        window = c_ref.shape[-1]
        assert window == BAND * d, (window, d)
        pos_t =lax.broadcasted_iota(jnp.int32, (rows, window), 0) & (t_new - 1)
        pos = lax.broadcasted_iota(jnp.int32, (rows, window), 1)
        ok_old = (pos >= pos_t) & (((pos - pos_t) & (d - 1)) == 0)
        s_old = jnp.where(ok_old, _dot(qm, c_ref[0].astype(BF16)), -jnp.inf)
        ok_new = (col <= row_t) & (((row_t - col) & (d - 1)) == 0)
        s_new = jnp.where(ok_new, _dot_nt(qm, k_new), -jnp.inf)
        m = jnp.maximum(jnp.max(s_old, axis=-1, keepdims=True),
                        jnp.max(s_new, axis=-1, keepdims=True))
        p_old = jnp.exp(s_old - m)
        p_new = jnp.exp(s_new - m)
        den = jnp.sum(p_old, axis=-1, keepdims=True) + jnp.sum(p_new, axis=-1, keepdims=True)
        acc = (_dot_nt(p_old.astype(BF16), c_ref[1].astype(BF16))
               + _dot(p_new.astype(BF16), v_new))
        o_g = jnp.zeros((t_new, A_WIDTH), F32)
        lse_g = jnp.zeros((t_new, A_WIDTH), F32)
        for h in range(A_HEADS):
            rs = slice(h * t_new, (h + 1) * t_new)
            o_g = jnp.where(hmask[h], acc[rs] * (1.0 / den[rs]), o_g)
            lse_g = jnp.where(hmask[h], m[rs] + jnp.log(den[rs]), lse_g)
        m_next = jnp.maximum(m_all, lse_g)
        scale_old = jnp.exp(m_all - m_next)
        w = jnp.exp(lse_g - m_next)
        num = num * scale_old + w * o_g
        den_all = den_all * scale_old + w
        m_all = m_next
    o_ref[...] = num * (1.0 / den_all)


def _attn_sample(qa, ka, va, c128, c512, c2048, batch, t_new):
    w3 = N_GROUPS * A_WIDTH
    tok_spec = pl.BlockSpec((None, t_new, w3), lambda b: (b, 0, 0))
    cache_views = [jnp.transpose(c, (0, 2, 3, 4, 1)).reshape(batch, 2, A_WIDTH, c.shape[1])
                   for c in (c128, c512, c2048)]
    cache_specs = [pl.BlockSpec((None, 2, A_WIDTH, c.shape[-1]), lambda b: (b, 0, 0, 0))
                   for c in cache_views]
    out = pl.pallas_call(
        functools.partial(_attn_sample_kernel, t_new=t_new),
        grid=(batch,),
        in_specs=[tok_spec, tok_spec, tok_spec] + cache_specs,
        out_specs=pl.BlockSpec((None, t_new, A_WIDTH), lambda b: (b, 0, 0)),
        out_shape=jax.ShapeDtypeStruct((batch, t_new, A_WIDTH), F32),
        compiler_params=_params(1),
        name="attn_sample",
    )(qa.reshape(batch, t_new, w3), ka.reshape(batch, t_new, w3), va.reshape(batch, t_new, w3),
      *cache_views)
    return out.reshape(batch * t_new, A_WIDTH)


def _split3(x):
    hi = x.astype(BF16)
    r = x - hi.astype(F32)
    mid = r.astype(BF16)
    lo = (r - mid.astype(F32)).astype(BF16)
    return hi, mid, lo


def _gla_out(o, rg, gain):
    on = o * lax.rsqrt(jnp.mean(o * o, axis=-1, keepdims=True) + EPS) * gain
    return (on * rg * jax.nn.sigmoid(rg)).astype(BF16)


def _col_bcast(vec_row, width):
    sq = jnp.broadcast_to(vec_row, (LANES, LANES)).T
    return jnp.concatenate([sq] * (width // LANES), axis=1)


def _gla_prompt_kernel(q_ref, k_ref, v_ref, la_ref, rg_ref, gn_ref, o_ref, s_ref, state_ref,
                       *, n_blocks):
    cb, sub = GLA_BLOCK, GLA_SUB
    row = lax.broadcasted_iota(jnp.int32, (cb, cb), 0)
    col = lax.broadcasted_iota(jnp.int32, (cb, cb), 1)
    tri = (row >= col).astype(BF16)
    causal = col <= row
    n_sub = cb // sub
    state_ref[...] = jnp.zeros_like(state_ref)

    def step(i, carry):
        st = pl.multiple_of(i * cb, cb)
        la = la_ref[pl.ds(st, cb), :]
        hi, mid, lo = _split3(la)
        b = _dot(tri, hi) + _dot(tri, mid) + _dot(tri, lo)
        q = q_ref[pl.ds(st, cb), :].astype(F32)
        k = k_ref[pl.ds(st, cb), :].astype(F32)
        v = v_ref[pl.ds(st, cb), :]
        b_last = b[cb - 1:cb, :]
        state = state_ref[...]

        out = _dot((q * jnp.exp(b)).astype(BF16), state.astype(BF16))

        sub_end = [b[(j + 1) * sub - 1:(j + 1) * sub, :] for j in range(n_sub)]
        own_end = jnp.concatenate(
            [jnp.broadcast_to(e, (sub, G_DK)) for e in sub_end], axis=0)
        kk = k * jnp.exp(own_end - b)
        q_parts, k_parts = [], []
        for j in range(n_sub):
            lo_row = j * sub
            qj = q[lo_row:, :] * jnp.exp(b[lo_row:, :] - sub_end[j])
            q_parts.append(jnp.concatenate(
                [jnp.zeros((lo_row, G_DK), F32), qj], axis=0).astype(BF16) if lo_row else
                qj.astype(BF16))
            in_sub = (row >= lo_row) & (row < lo_row + sub)
            k_parts.append(jnp.where(in_sub, kk, 0.0).astype(BF16))
        att = _dot_nt(jnp.concatenate(q_parts, axis=1), jnp.concatenate(k_parts, axis=1))
        att = jnp.where(causal, att, 0.0)
        out = out + _dot(att.astype(BF16), v)

        kd_t = (k * jnp.exp(b_last - b)).T.astype(BF16)
        state_ref[...] = state * _col_bcast(jnp.exp(b_last), G_DV) + _dot(kd_t, v)

        o_ref[pl.ds(st, cb), :] = _gla_out(out, rg_ref[pl.ds(st, cb), :].astype(F32), gn_ref[...])
        return carry

    lax.fori_loop(0, n_blocks, step, 0)
    s_ref[...] = state_ref[...]


def _gla_prompt(qg, kg, vg, la, rg, gn, batch, seq):
    def spec(width):
        return pl.BlockSpec((None, seq, width), lambda b, h: (b, 0, h))
    o, s = pl.pallas_call(
        functools.partial(_gla_prompt_kernel, n_blocks=seq // GLA_BLOCK),
        grid=(batch, G_HEADS),
        in_specs=[spec(G_DK), spec(G_DK), spec(G_DV), spec(G_DK), spec(G_DV),
                  _resident((1, G_DV))],
        out_specs=[spec(G_DV),
                   pl.BlockSpec((None, None, G_DK, G_DV), lambda b, h: (b, h, 0, 0))],
        out_shape=[jax.ShapeDtypeStruct((batch, seq, G_VW), BF16),
                   jax.ShapeDtypeStruct((batch, G_HEADS, G_DK, G_DV), F32)],
        scratch_shapes=[pltpu.VMEM((G_DK, G_DV), F32)],
        compiler_params=_params(2),
        name="gla_prompt",
    )(qg.reshape(batch, seq, G_KW), kg.reshape(batch, seq, G_KW), vg.reshape(batch, seq, G_VW),
      la.reshape(batch, seq, G_KW), rg.reshape(batch, seq, G_VW), gn)
    return o.reshape(batch * seq, G_VW), s


def _gla_sample_kernel(q_ref, k_ref, v_ref, la_ref, rg_ref, gn_ref, s0_ref, o_ref, s_ref,
                       *, bb, t_new):
    trow = lax.broadcasted_iota(jnp.int32, (t_new, G_DK), 0)
    arow = lax.broadcasted_iota(jnp.int32, (t_new, LANES), 0)
    acol = lax.broadcasted_iota(jnp.int32, (t_new, LANES), 1)
    causal = acol <= arow
    zk = jnp.zeros((LANES - t_new, G_DK), F32)
    zv = jnp.zeros((LANES - t_new, G_DV), F32)

    def seq_step(i, carry):
        for h in range(G_HEADS):
            ks = slice(h * G_DK, (h + 1) * G_DK)
            vs = slice(h * G_DV, (h + 1) * G_DV)
            b = la_ref[i, :, ks]
            shift = 1
            while shift < t_new:
                b = b + jnp.where(trow >= shift, pltpu.roll(b, shift, 0), 0.0)
                shift *= 2
            q = q_ref[i, :, ks].astype(F32)
            k = k_ref[i, :, ks].astype(F32)
            v = jnp.concatenate([v_ref[i, :, vs].astype(F32), zv], axis=0).astype(BF16)
            b_last = b[t_new - 1:t_new, :]
            s0 = s0_ref[i, h]
            qe = (q * jnp.exp(b)).astype(BF16)
            ke = jnp.concatenate([k * jnp.exp(-b), zk], axis=0).astype(BF16)
            att = jnp.where(causal, _dot_nt(qe, ke), 0.0)
            out = _dot(att.astype(BF16), v) + _dot(qe, s0.astype(BF16))
            kd = jnp.concatenate([k * jnp.exp(b_last - b), zk], axis=0)
            s_ref[i, h] = s0 * _col_bcast(jnp.exp(b_last), G_DV) + _dot(kd.T.astype(BF16), v)
            o_ref[i, :, vs] = _gla_out(out, rg_ref[i, :, vs].astype(F32), gn_ref[...])
        return carry

    lax.fori_loop(0, bb, seq_step, 0)


def _gla_sample(qg, kg, vg, la, rg, gn, s0, batch, t_new, bb):
    def spec(width):
        return pl.BlockSpec((bb, t_new, width), lambda b: (b, 0, 0))
    st_spec = pl.BlockSpec((bb, G_HEADS, G_DK, G_DV), lambda b: (b, 0, 0, 0))
    o, s = pl.pallas_call(
        functools.partial(_gla_sample_kernel, bb=bb, t_new=t_new),
        grid=(batch // bb,),
        in_specs=[spec(G_KW), spec(G_KW), spec(G_VW), spec(G_KW), spec(G_VW),
                  _resident((1, G_DV)), st_spec],
        out_specs=[spec(G_VW), st_spec],
        out_shape=[jax.ShapeDtypeStruct((batch, t_new, G_VW), BF16),
                   jax.ShapeDtypeStruct((batch, G_HEADS, G_DK, G_DV), F32)],
        compiler_params=_params(1),
        name="gla_sample",
    )(qg.reshape(batch, t_new, G_KW), kg.reshape(batch, t_new, G_KW),
      vg.reshape(batch, t_new, G_VW), la.reshape(batch, t_new, G_KW),
      rg.reshape(batch, t_new, G_VW), gn, s0)
    return o.reshape(batch * t_new, G_VW), s


def _out_kernel(*refs, n_attn):
    x_ref = refs[0]
    attn_refs = refs[1:1 + 2 * n_attn] if n_attn > 1 else refs[1:2]
    rest = refs[1 + (2 * n_attn if n_attn > 1 else 1):]
    (gb_ref, gate_ref, wa_ref, wb_ref, wo_ref, g_ref, wg_ref, wu_ref, wd_ref,
     o_ref, act_ref) = rest

    if n_attn > 1:
        outs = [r[...] for r in attn_refs[:n_attn]]
        lses = [r[...] for r in attn_refs[n_attn:]]
        m = functools.reduce(jnp.maximum, lses)
        ws = [jnp.exp(l - m) for l in lses]
        a = functools.reduce(jnp.add, [w * o for w, o in zip(ws, outs)])
        a = a * (1.0 / functools.reduce(jnp.add, ws))
    else:
        a = attn_refs[0][...]

    a_out = _dot(a.astype(BF16), wa_ref[...])
    b_out = _dot(gb_ref[...], wb_ref[...])
    mix = (gate_ref[:, 0:D_MODEL].astype(F32) * a_out
           + gate_ref[:, D_MODEL:2 * D_MODEL].astype(F32) * b_out)
    x2 = x_ref[...] + _dot(mix.astype(BF16), wo_ref[...])
    o_ref[...] = _swiglu_residual(x2, g_ref, wg_ref, wu_ref, wd_ref, act_ref)


def _out(x1, attn, gb, gates, wa, wb, wo, g, wg, wu, wd, tm):
    t = x1.shape[0]
    n_attn = len(attn) // 2 if len(attn) > 1 else 1

    def tok(width):
        return pl.BlockSpec((tm, width), lambda i: (i, 0))
    consts = [wa, wb, wo, g, wg, wu, wd]
    return pl.pallas_call(
        functools.partial(_out_kernel, n_attn=n_attn),
        grid=(t // tm,),
        in_specs=[tok(D_MODEL)] + [tok(A_WIDTH)] * len(attn) + [tok(G_VW), tok(2 * D_MODEL)]
        + [_resident(c.shape) for c in consts],
        out_specs=tok(D_MODEL),
        out_shape=jax.ShapeDtypeStruct((t, D_MODEL), F32),
        scratch_shapes=[pltpu.VMEM((tm, D_FF), BF16)],
        compiler_params=_params(1),
        name="out_ffn2",
    )(x1, *attn, gb, gates, *consts)


KV_TILE = 512


def _kv_rows_kernel(k_ref, v_ref, *out_refs, n_tiles):
    j = pl.program_id(1)
    wide = N_GROUPS - 1
    for kv, src in enumerate((k_ref, v_ref)):
        out_refs[wide][kv] = src[:, wide * A_WIDTH:(wide + 1) * A_WIDTH].T

    @pl.when(j == n_tiles - 1)
    def _():
        for g in range(wide):
            keep = out_refs[g].shape[-1]
            for kv, src in enumerate((k_ref, v_ref)):
                out_refs[g][kv] = src[KV_TILE - keep:, g * A_WIDTH:(g + 1) * A_WIDTH].T


def _kv_rows(ka, va, batch, seq):
    windows = [BAND * d for d in SWA_DILATIONS]
    assert windows[-1] <= seq and windows[-1] % KV_TILE == 0 and windows[-2] <= KV_TILE
    n_tiles = windows[-1] // KV_TILE
    first = (seq - windows[-1]) // KV_TILE
    per_seq = seq // KV_TILE
    tok = pl.BlockSpec((KV_TILE, N_GROUPS * A_WIDTH), lambda b, j: (b * per_seq + first + j, 0))
    out_specs = [pl.BlockSpec((None, 2, A_WIDTH, w), lambda b, j: (b, 0, 0, 0))
                 for w in windows[:-1]]
    out_specs.append(pl.BlockSpec((None, 2, A_WIDTH, KV_TILE), lambda b, j: (b, 0, 0, j)))
    outs = pl.pallas_call(
        functools.partial(_kv_rows_kernel, n_tiles=n_tiles),
        grid=(batch, n_tiles),
        in_specs=[tok, tok],
        out_specs=out_specs,
        out_shape=[jax.ShapeDtypeStruct((batch, 2, A_WIDTH, w), F32) for w in windows],
        compiler_params=_params(2),
        name="kv_rows",
    )(ka, va)
    return [jnp.transpose(o.reshape(batch, 2, A_HEADS, A_HEAD_DIM, w), (0, 4, 1, 2, 3))
            for o, w in zip(outs, windows)]


def _cache_rows(ka, va, g, batch, seq, keep):
    k = ka.reshape(batch, seq, N_GROUPS, A_HEADS, A_HEAD_DIM)[:, seq - keep:, g]
    v = va.reshape(batch, seq, N_GROUPS, A_HEADS, A_HEAD_DIM)[:, seq - keep:, g]
    return jnp.stack([k, v], axis=2)


def kernel(x_prompt, x_sample, cache_swa128_kv, cache_swa512_kv, cache_swa2048_kv, state_gla,
           norm_ffn1, ffn1_gate, ffn1_up, ffn1_down, norm_mix, w_in, a_q_norm, a_k_norm,
           g_alpha_up, g_alpha_bias, g_out_norm, w_a_out, w_b_out, w_out,
           norm_ffn2, ffn2_gate, ffn2_up, ffn2_down):
    batch, seq, _ = x_prompt.shape
    dec_batch, dec_seq, _ = x_sample.shape

    aw = N_GROUPS * A_WIDTH
    o_g = 3 * aw
    o_lr = o_g + 2 * G_KW + 2 * G_VW
    o_gate = o_lr + G_LOWRANK
    w_attn = w_in[:, :o_g].astype(BF16)
    w_gla = w_in[:, o_g:o_lr].astype(BF16)
    w_lr = jnp.pad(w_in[:, o_lr:o_gate], ((0, 0), (0, LANES - G_LOWRANK))).astype(BF16)
    w_gate = w_in[:, o_gate:].astype(BF16)
    up = jnp.pad(g_alpha_up, ((0, LANES - G_LOWRANK), (0, 0))).astype(BF16)
    bias = g_alpha_bias.reshape(1, G_KW)
    qgain = jnp.tile(a_q_norm, (1, A_HEADS)).reshape(1, aw)
    kgain = jnp.tile(a_k_norm, (1, A_HEADS)).reshape(1, aw)
    lane = jnp.arange(A_WIDTH) // A_HEAD_DIM
    hmean = jnp.where(lane[:, None] == lane[None, :], 1.0 / A_HEAD_DIM, 0.0).astype(BF16)
    gn = g_out_norm.reshape(1, G_DV)
    ffn1 = (norm_ffn1.reshape(1, D_MODEL), ffn1_gate.astype(BF16), ffn1_up.astype(BF16),
            ffn1_down.astype(BF16))
    ffn2 = (norm_ffn2.reshape(1, D_MODEL), ffn2_gate.astype(BF16), ffn2_up.astype(BF16),
            ffn2_down.astype(BF16))
    proj_w = (norm_mix.reshape(1, D_MODEL), w_attn, w_gla, w_lr, w_gate, up, bias,
              qgain, kgain, hmean)
    out_w = (w_a_out.astype(BF16), w_b_out.astype(BF16), w_out.astype(BF16))

    def front(x, tm):
        x1 = _ffn(x, *ffn1, tm=tm)
        return (x1,) + tuple(_proj(x1, *proj_w, tm=tm))

    x1p, qa, ka, va, qg, kg, vg, rg, la, gates = front(x_prompt.reshape(batch * seq, D_MODEL), 512)
    attn = [_attn_prompt_group(qa, ka, va, g, d, batch, seq)
            for g, d in enumerate(SWA_DILATIONS)]
    attn = [o for o, _ in attn] + [l for _, l in attn]
    gbp, state_p = _gla_prompt(qg, kg, vg, la, rg, gn, batch, seq)
    y_prompt = _out(x1p, attn, gbp, gates, *out_w, *ffn2, tm=256).reshape(batch, seq, D_MODEL)
    rows_p = _kv_rows(ka, va, batch, seq)

    x1s, qa, ka, va, qg, kg, vg, rg, la, gates = front(
        x_sample.reshape(dec_batch * dec_seq, D_MODEL), 512)
    a_s = _attn_sample(qa, ka, va, cache_swa128_kv, cache_swa512_kv, cache_swa2048_kv,
                       dec_batch, dec_seq)
    gbs, state_s = _gla_sample(qg, kg, vg, la, rg, gn, state_gla, dec_batch, dec_seq, bb=8)
    y_sample = _out(x1s, [a_s], gbs, gates, *out_w, *ffn2, tm=256).reshape(
        dec_batch, dec_seq, D_MODEL)
    rows_s = [_cache_rows(ka, va, g, dec_batch, dec_seq, dec_seq) for g in range(N_GROUPS)]

    return (y_prompt, y_sample, rows_p[0], rows_p[1], rows_p[2], state_p,
            rows_s[0], rows_s[1], rows_s[2], state_s)
```

```python
import functools

import jax
import jax.numpy as jnp
from jax import lax
from jax.experimental import pallas as pl
from jax.experimental.pallas import tpu as pltpu

F32 = jnp.float32
BF16 = jnp.bfloat16

D_MODEL = 1024
D_FF = 2816
SWA_DILATIONS = (1, 4, 16)
N_GROUPS = 3
A_HEADS = 4
A_HEAD_DIM = 64
A_WIDTH = A_HEADS * A_HEAD_DIM
BAND = 128
G_HEADS = 4
G_DK = 128
G_DV = 256
G_KW = G_HEADS * G_DK
G_VW = G_HEADS * G_DV
G_LOWRANK = 16
G_TAU = 16.0
EPS = 1e-6

LANES = 128
GLA_BLOCK = 128
GLA_SUB = 32
FF_CHUNK = 256
N_CHUNK = 512
VMEM_LIMIT = 58 * 1024 * 1024


def _resident(shape):
    nd = len(shape)
    return pl.BlockSpec(shape, lambda *_: (0,) * nd, pipeline_mode=pl.Buffered(1))


def _params(n_axes):
    return pltpu.CompilerParams(
        dimension_semantics=("arbitrary",) * n_axes, vmem_limit_bytes=VMEM_LIMIT)


def _dot(a, b):
    return jnp.dot(a, b, preferred_element_type=F32)


def _dot_nt(a, b):
    return lax.dot_general(a, b, (((1,), (1,)), ((), ())), preferred_element_type=F32)


def _rmsnorm(x, g):
    return x * lax.rsqrt(jnp.mean(x * x, axis=-1, keepdims=True) + EPS) * g


def _swiglu_residual(x, g_ref, wg_ref, wu_ref, wd_ref, act_ref):
    h = _rmsnorm(x, g_ref[...]).astype(BF16)
    for c in range(D_FF // FF_CHUNK):
        sl = slice(c * FF_CHUNK, (c + 1) * FF_CHUNK)
        gate = _dot(h, wg_ref[:, sl])
        up = _dot(h, wu_ref[:, sl])
        act_ref[:, sl] = (gate * jax.nn.sigmoid(gate) * up).astype(BF16)
    return x + 0.5 * _dot(act_ref[...], wd_ref[...])


def _ffn_kernel(x_ref, g_ref, wg_ref, wu_ref, wd_ref, o_ref, act_ref):
    o_ref[...] = _swiglu_residual(x_ref[...], g_ref, wg_ref, wu_ref, wd_ref, act_ref)


def _ffn(x, g, wg, wu, wd, tm):
    t = x.shape[0]
    return pl.pallas_call(
        _ffn_kernel,
        grid=(t // tm,),
        in_specs=[
            pl.BlockSpec((tm, D_MODEL), lambda i: (i, 0)),
            _resident((1, D_MODEL)),
            _resident((D_MODEL, D_FF)),
            _resident((D_MODEL, D_FF)),
            _resident((D_FF, D_MODEL)),
        ],
        out_specs=pl.BlockSpec((tm, D_MODEL), lambda i: (i, 0)),
        out_shape=jax.ShapeDtypeStruct((t, D_MODEL), F32),
        scratch_shapes=[pltpu.VMEM((tm, D_FF), BF16)],
        compiler_params=_params(1),
        name="ffn1",
    )(x, g, wg, wu, wd)


def _log_sigmoid(z):
    return jnp.minimum(z, 0.0) - jnp.log1p(jnp.exp(-jnp.abs(z)))


def _store_blocked(x, d, out_ref, perm_ref):
    tm = x.shape[0]
    if d == 1:
        for j in range(tm // BAND):
            out_ref[j] = x[j * BAND:(j + 1) * BAND].astype(BF16)
        return
    for s in range(A_WIDTH // LANES):
        perm_ref[s] = x[:, s * LANES:(s + 1) * LANES]
    for r in range(d):
        rows = jnp.concatenate(
            [perm_ref[s, pl.ds(r, tm // d, stride=d), :] for s in range(A_WIDTH // LANES)], axis=1)
        out_ref[r] = rows.astype(BF16)


def _proj_kernel(*refs, blocked):
    (x_ref, g_ref, wa_ref, wg_ref, wlr_ref, wgate_ref, up_ref, bias_ref,
     qgain_ref, kgain_ref, hmean_ref) = refs[:11]
    outs = refs[11:]
    if blocked:
        qb_refs, kb_refs, vb_refs = outs[0:3], outs[3:6], outs[6:9]
        ka_ref, va_ref, qg_ref, kg_ref, vg_ref, rg_ref, la_ref, gate_ref, perm_ref = outs[9:]
    else:
        qa_ref, ka_ref, va_ref, qg_ref, kg_ref, vg_ref, rg_ref, la_ref, gate_ref = outs

    h = _rmsnorm(x_ref[...], g_ref[...]).astype(BF16)
    hmean = hmean_ref[...]

    def head_norm(x, gain):
        ms = _dot((x * x).astype(BF16), hmean)
        return x * lax.rsqrt(ms + EPS) * gain

    aw = N_GROUPS * A_WIDTH
    for g, d in enumerate(SWA_DILATIONS):
        sl = slice(g * A_WIDTH, (g + 1) * A_WIDTH)
        q = _dot(h, wa_ref[:, g * A_WIDTH:(g + 1) * A_WIDTH])
        k = _dot(h, wa_ref[:, aw + g * A_WIDTH:aw + (g + 1) * A_WIDTH])
        v = _dot(h, wa_ref[:, 2 * aw + g * A_WIDTH:2 * aw + (g + 1) * A_WIDTH])
        q = head_norm(q, qgain_ref[:, sl]) * (A_HEAD_DIM ** -0.5)
        k = head_norm(k, kgain_ref[:, sl])
        ka_ref[:, sl] = k
        va_ref[:, sl] = v
        if blocked:
            for i, (x, o_refs) in enumerate(((q, qb_refs), (k, kb_refs), (v, vb_refs))):
                _store_blocked(x, d, o_refs[g], perm_ref.at[(g - 1) * 3 + i] if d > 1 else None)
        else:
            qa_ref[:, sl] = q.astype(BF16)

    qg_ref[...] = (_dot(h, wg_ref[:, 0:G_KW]) * (G_DK ** -0.5)).astype(BF16)
    kg_ref[...] = _dot(h, wg_ref[:, G_KW:2 * G_KW]).astype(BF16)
    for c in range(G_VW // N_CHUNK):
        sl = slice(c * N_CHUNK, (c + 1) * N_CHUNK)
        off = 2 * G_KW + c * N_CHUNK
        vg_ref[:, sl] = _dot(h, wg_ref[:, off:off + N_CHUNK]).astype(BF16)
        rg_ref[:, sl] = _dot(h, wg_ref[:, G_VW + off:G_VW + off + N_CHUNK]).astype(BF16)

    lr = _dot(h, wlr_ref[...]).astype(BF16)
    z = _dot(lr, up_ref[...]) + bias_ref[...]
    la_ref[...] = _log_sigmoid(z) * (1.0 / G_TAU)

    for c in range(2 * D_MODEL // N_CHUNK):
        sl = slice(c * N_CHUNK, (c + 1) * N_CHUNK)
        gate_ref[:, sl] = jax.nn.sigmoid(_dot(h, wgate_ref[:, sl])).astype(BF16)


def _unit_spec(d, tm):
    per_class = tm // d
    if per_class >= BAND:
        return pl.BlockSpec((tm // BAND, BAND, A_WIDTH), lambda i: (i, 0, 0))
    parts = BAND // per_class
    return pl.BlockSpec((d, per_class, A_WIDTH), lambda i: (i // parts, i % parts, 0))


def _proj(x1, g, wa, wg, wlr, wgate, up, bias, qgain, kgain, hmean, tm, blocked):
    t = x1.shape[0]
    aw = N_GROUPS * A_WIDTH
    widths = [(aw, F32), (aw, F32), (G_KW, BF16), (G_KW, BF16), (G_VW, BF16), (G_VW, BF16),
              (G_KW, F32), (2 * D_MODEL, BF16)]
    out_specs = [pl.BlockSpec((tm, w), lambda i: (i, 0)) for w, _ in widths]
    out_shape = [jax.ShapeDtypeStruct((t, w), dt) for w, dt in widths]
    scratch = []
    if blocked:
        assert tm == 4 * BAND and SWA_DILATIONS == (1, 4, 16)
        out_specs = [_unit_spec(d, tm) for _ in range(3) for d in SWA_DILATIONS] + out_specs
        out_shape = [jax.ShapeDtypeStruct((t // BAND, BAND, A_WIDTH), BF16)] * 9 + out_shape
        scratch = [pltpu.VMEM((3 * (N_GROUPS - 1), A_WIDTH // LANES, tm, LANES), F32)]
    else:
        out_specs = [pl.BlockSpec((tm, aw), lambda i: (i, 0))] + out_specs
        out_shape = [jax.ShapeDtypeStruct((t, aw), BF16)] + out_shape
    consts = [g, wa, wg, wlr, wgate, up, bias, qgain, kgain, hmean]
    return pl.pallas_call(
        functools.partial(_proj_kernel, blocked=blocked),
        grid=(t // tm,),
        in_specs=[pl.BlockSpec((tm, D_MODEL), lambda i: (i, 0))]
        + [_resident(c.shape) for c in consts],
        out_specs=out_specs,
        out_shape=out_shape,
        scratch_shapes=scratch,
        compiler_params=_params(1),
        name="in_proj",
    )(x1, *consts)


def _log2(n):
    assert n > 0 and n & (n - 1) == 0, n
    return n.bit_length() - 1


def _head_masks(rows):
    lane = lax.broadcasted_iota(jnp.int32, (rows, A_WIDTH), 1)
    return [(lane >> _log2(A_HEAD_DIM)) == h for h in range(A_HEADS)]


ATTN_SPAN = 2048


def _attn_prompt_kernel(*refs, units):
    q_refs, k_refs, v_refs = refs[0:3], refs[3:6], refs[6:9]
    o_ref, osc_ref, lsc_ref = refs[9:]
    span = pl.program_id(1)
    n_slab = A_WIDTH // LANES

    for g, d in enumerate(SWA_DILATIONS):
        q_ref, k_ref, v_ref = q_refs[g], k_refs[g], v_refs[g]

        def scores(ul, d=d, q_ref=q_ref, k_ref=k_ref):
            u = span * units + ul
            up = jnp.maximum(u - d, 0)
            kb = jnp.concatenate([k_ref[up], k_ref[u]], axis=0)
            q = q_ref[ul]
            qm = jnp.concatenate(
                [q * m.astype(F32).astype(BF16) for m in _head_masks(BAND)], axis=0)
            rows_all = A_HEADS * BAND
            row = lax.broadcasted_iota(jnp.int32, (rows_all, 2 * BAND), 0) & (BAND - 1)
            col = lax.broadcasted_iota(jnp.int32, (rows_all, 2 * BAND), 1)
            first_valid = jnp.where(u >= d, 0, BAND)
            valid = (((col < BAND) & (col >= row) & (col >= first_valid))
                     | ((col >= BAND) & (col - BAND <= row)))
            return jnp.where(valid, _dot_nt(qm, kb), -jnp.inf)

        def softmax(s):
            m = jnp.max(s, axis=-1, keepdims=True)
            p = jnp.exp(s - m)
            den = jnp.sum(p, axis=-1, keepdims=True)
            return p.astype(BF16), 1.0 / den, m + jnp.log(den)

        def values(ul, p, d=d, v_ref=v_ref):
            u = span * units + ul
            vb = jnp.concatenate([v_ref[jnp.maximum(u - d, 0)], v_ref[u]], axis=0)
            return _dot(p, vb)

        def store(ul, o_all, inv_den, lse_all, d=d, g=g):
            hmask = _head_masks(BAND)
            o_all = o_all * inv_den
            o_acc = jnp.zeros((BAND, A_WIDTH), F32)
            lse_acc = jnp.zeros((BAND, A_WIDTH), F32)
            for h in range(A_HEADS):
                rs = slice(h * BAND, (h + 1) * BAND)
                o_acc = jnp.where(hmask[h], o_all[rs], o_acc)
                lse_acc = jnp.where(hmask[h], lse_all[rs], lse_acc)
            start = (ul >> _log2(d)) * (BAND * d) + (ul & (d - 1))
            if d == 1:
                rows = pl.ds(pl.multiple_of(start, BAND), BAND)
            else:
                rows = pl.ds(start, BAND, stride=d)
            for sl in range(n_slab):
                osc_ref[g, sl, rows, :] = o_acc[:, sl * LANES:(sl + 1) * LANES]
                lsc_ref[g, sl, rows, :] = lse_acc[:, sl * LANES:(sl + 1) * LANES]

        def unit_pair(i, carry, scores=scores, softmax=softmax, values=values, store=store):
            uls = (2 * i, 2 * i + 1)
            ss = [scores(ul) for ul in uls]
            sm = [softmax(s) for s in ss]
            os_ = [values(ul, p) for ul, (p, _, _) in zip(uls, sm)]
            for ul, o_all, (_, inv_den, lse_all) in zip(uls, os_, sm):
                store(ul, o_all, inv_den, lse_all)
            return carry

        lax.fori_loop(0, units // 2, unit_pair, 0)

    def merge(i, carry):
        rows = pl.ds(pl.multiple_of(i * BAND, BAND), BAND)

        def natural(ref, g):
            return jnp.concatenate([ref[g, sl, rows, :] for sl in range(n_slab)], axis=1)

        outs = [natural(osc_ref, g) for g in range(N_GROUPS)]
        lses = [natural(lsc_ref, g) for g in range(N_GROUPS)]
        m = functools.reduce(jnp.maximum, lses)
        ws = [jnp.exp(l - m) for l in lses]
        a = functools.reduce(jnp.add, [w * o for w, o in zip(ws, outs)])
        o_ref[rows, :] = (a * (1.0 / functools.reduce(jnp.add, ws))).astype(BF16)
        return carry

    lax.fori_loop(0, ATTN_SPAN // BAND, merge, 0)


def _attn_prompt(qkv_units, batch, seq):
    assert seq % ATTN_SPAN == 0
    units = ATTN_SPAN // BAND
    per_seq = seq // BAND
    spans = seq // ATTN_SPAN
    q_spec = pl.BlockSpec((units, BAND, A_WIDTH), lambda b, s: (b * spans + s, 0, 0))
    kv_spec = pl.BlockSpec((per_seq, BAND, A_WIDTH), lambda b, s: (b, 0, 0))
    return pl.pallas_call(
        functools.partial(_attn_prompt_kernel, units=units),
        grid=(batch, spans),
        in_specs=[q_spec] * 3 + [kv_spec] * 6,
        out_specs=pl.BlockSpec((ATTN_SPAN, A_WIDTH), lambda b, s: (b * spans + s, 0)),
        out_shape=jax.ShapeDtypeStruct((batch * seq, A_WIDTH), BF16),
        scratch_shapes=[pltpu.VMEM((N_GROUPS, A_WIDTH // LANES, ATTN_SPAN, LANES), F32)] * 2,
        compiler_params=_params(2),
        name="attn_prompt",
    )(*qkv_units)


def _attn_sample_kernel(q_ref, k_ref, v_ref, c1_ref, c4_ref, c16_ref, o_ref, *, t_new):
    hmask = _head_masks(t_new)
    rows = A_HEADS * t_new
    row_t = lax.broadcasted_iota(jnp.int32, (rows, LANES), 0) & (t_new - 1)
    col = lax.broadcasted_iota(jnp.int32, (rows, LANES), 1)
    pad = jnp.zeros((LANES - t_new, A_WIDTH), F32)

    num = jnp.zeros((t_new, A_WIDTH), F32)
    den_all = jnp.zeros((t_new, A_WIDTH), F32)
    m_all = jnp.full((t_new, A_WIDTH), -jnp.inf, F32)
    for g, (d, c_ref) in enumerate(zip(SWA_DILATIONS, (c1_ref, c4_ref, c16_ref))):
        sl = slice(g * A_WIDTH, (g + 1) * A_WIDTH)
        qm = jnp.concatenate(
            [q_ref[:, sl].astype(F32) * hmask[h].astype(F32) for h in range(A_HEADS)],
            axis=0).astype(BF16)
        k_new = jnp.concatenate([k_ref[:, sl], pad], axis=0).astype(BF16)
        v_new = jnp.concatenate([v_ref[:, sl], pad], axis=0).astype(BF16)
        window = c_ref.shape[-1]
        assert window == BAND * d, (window, d)
        pos_t =lax.broadcasted_iota(jnp.int32, (rows, window), 0) & (t_new - 1)
        pos = lax.broadcasted_iota(jnp.int32, (rows, window), 1)
        ok_old = (pos >= pos_t) & (((pos - pos_t) & (d - 1)) == 0)
        s_old = jnp.where(ok_old, _dot(qm, c_ref[0].astype(BF16)), -jnp.inf)
        ok_new = (col <= row_t) & (((row_t - col) & (d - 1)) == 0)
        s_new = jnp.where(ok_new, _dot_nt(qm, k_new), -jnp.inf)
        m = jnp.maximum(jnp.max(s_old, axis=-1, keepdims=True),
                        jnp.max(s_new, axis=-1, keepdims=True))
        p_old = jnp.exp(s_old - m)
        p_new = jnp.exp(s_new - m)
        den = jnp.sum(p_old, axis=-1, keepdims=True) + jnp.sum(p_new, axis=-1, keepdims=True)
        acc = (_dot_nt(p_old.astype(BF16), c_ref[1].astype(BF16))
               + _dot(p_new.astype(BF16), v_new))
        o_g = jnp.zeros((t_new, A_WIDTH), F32)
        lse_g = jnp.zeros((t_new, A_WIDTH), F32)
        for h in range(A_HEADS):
            rs = slice(h * t_new, (h + 1) * t_new)
            o_g = jnp.where(hmask[h], acc[rs] * (1.0 / den[rs]), o_g)
            lse_g = jnp.where(hmask[h], m[rs] + jnp.log(den[rs]), lse_g)
        m_next = jnp.maximum(m_all, lse_g)
        scale_old = jnp.exp(m_all - m_next)
        w = jnp.exp(lse_g - m_next)
        num = num * scale_old + w * o_g
        den_all = den_all * scale_old + w
        m_all = m_next
    o_ref[...] = (num * (1.0 / den_all)).astype(BF16)


def _attn_sample(qa, ka, va, c128, c512, c2048, batch, t_new):
    w3 = N_GROUPS * A_WIDTH
    tok_spec = pl.BlockSpec((None, t_new, w3), lambda b: (b, 0, 0))
    cache_views = [jnp.transpose(c, (0, 2, 3, 4, 1)).reshape(batch, 2, A_WIDTH, c.shape[1])
                   for c in (c128, c512, c2048)]
    cache_specs = [pl.BlockSpec((None, 2, A_WIDTH, c.shape[-1]), lambda b: (b, 0, 0, 0))
                   for c in cache_views]
    out = pl.pallas_call(
        functools.partial(_attn_sample_kernel, t_new=t_new),
        grid=(batch,),
        in_specs=[tok_spec, tok_spec, tok_spec] + cache_specs,
        out_specs=pl.BlockSpec((None, t_new, A_WIDTH), lambda b: (b, 0, 0)),
        out_shape=jax.ShapeDtypeStruct((batch, t_new, A_WIDTH), BF16),
        compiler_params=_params(1),
        name="attn_sample",
    )(qa.reshape(batch, t_new, w3), ka.reshape(batch, t_new, w3), va.reshape(batch, t_new, w3),
      *cache_views)
    return out.reshape(batch * t_new, A_WIDTH)


def _split2(x):
    hi = x.astype(BF16)
    return hi, (x - hi.astype(F32)).astype(BF16)


def _gla_out(o, rg, gain):
    on = o * lax.rsqrt(jnp.mean(o * o, axis=-1, keepdims=True) + EPS) * gain
    return (on * rg * jax.nn.sigmoid(rg)).astype(BF16)


def _col_bcast(vec_row, width):
    sq = jnp.broadcast_to(vec_row, (LANES, LANES)).T
    return jnp.concatenate([sq] * (width // LANES), axis=1)


GLA_CHUNK = 1024


def _gla_prompt_kernel(q_ref, k_ref, v_ref, la_ref, rg_ref, gn_ref, o_ref, s_ref, *, n_blocks):
    cb, sub = GLA_BLOCK, GLA_SUB
    n_sub = cb // sub

    @pl.when(pl.program_id(1) == 0)
    def _():
        s_ref[...] = jnp.zeros_like(s_ref)

    heads = range(G_HEADS)

    def hk(x, h):
        return x[:, h * G_DK:(h + 1) * G_DK]

    def step(i, carry):
        st = pl.multiple_of(i * cb, cb)
        rows = pl.ds(st, cb)
        row = lax.broadcasted_iota(jnp.int32, (cb, cb), 0)
        col = lax.broadcasted_iota(jnp.int32, (cb, cb), 1)
        tri = (row >= col).astype(F32).astype(BF16)
        rowk = lax.broadcasted_iota(jnp.int32, (cb, G_KW), 0)

        hi, lo = _split2(la_ref[rows, :])
        b = _dot(tri, hi) + _dot(tri, lo)
        q = q_ref[rows, :].astype(F32)
        k = k_ref[rows, :].astype(F32)
        b_last = b[cb - 1:cb, :]
        qe = (q * jnp.exp(b)).astype(BF16)
        sub_end = [b[(j + 1) * sub - 1:(j + 1) * sub, :] for j in range(n_sub)]
        own_end = jnp.concatenate(
            [jnp.broadcast_to(e, (sub, G_KW)) for e in sub_end], axis=0)
        kk = k * jnp.exp(own_end - b)
        q_parts, k_parts = [], []
        for j in range(n_sub):
            lo_row = j * sub
            qj = q[lo_row:, :] * jnp.exp(b[lo_row:, :] - sub_end[j])
            q_parts.append(jnp.concatenate(
                [jnp.zeros((lo_row, G_KW), F32), qj], axis=0).astype(BF16) if lo_row else
                qj.astype(BF16))
            in_sub = (rowk >= lo_row) & (rowk < lo_row + sub)
            k_parts.append(jnp.where(in_sub, kk, 0.0).astype(BF16))
        kd = k * jnp.exp(b_last - b)
        decay = jnp.exp(b_last)

        states = [s_ref[h] for h in heads]
        vals = [v_ref[rows, h * G_DV:(h + 1) * G_DV] for h in heads]
        inter = [_dot(hk(qe, h), states[h].astype(BF16)) for h in heads]
        atts = [_dot_nt(jnp.concatenate([hk(p, h) for p in q_parts], axis=1),
                        jnp.concatenate([hk(p, h) for p in k_parts], axis=1)) for h in heads]
        atts = [jnp.where(col <= row, a, 0.0).astype(BF16) for a in atts]
        outs = [inter[h] + _dot(atts[h], vals[h]) for h in heads]
        for h in heads:
            kd_t = hk(kd, h).T.astype(BF16)
            s_ref[h] = states[h] * _col_bcast(hk(decay, h), G_DV) + _dot(kd_t, vals[h])
        for h in heads:
            vs = slice(h * G_DV, (h + 1) * G_DV)
            o_ref[rows, vs] = _gla_out(outs[h], rg_ref[rows, vs].astype(F32), gn_ref[...])
        return carry

    lax.fori_loop(0, n_blocks, step, 0)


def _gla_prompt(qg, kg, vg, la, rg, gn, batch, seq):
    assert seq % GLA_CHUNK == 0
    chunks = seq // GLA_CHUNK

    def spec(width):
        return pl.BlockSpec((GLA_CHUNK, width), lambda b, c: (b * chunks + c, 0))
    return pl.pallas_call(
        functools.partial(_gla_prompt_kernel, n_blocks=GLA_CHUNK // GLA_BLOCK),
        grid=(batch, chunks),
        in_specs=[spec(G_KW), spec(G_KW), spec(G_VW), spec(G_KW), spec(G_VW),
                  _resident((1, G_DV))],
        out_specs=[spec(G_VW),
                   pl.BlockSpec((None, G_HEADS, G_DK, G_DV), lambda b, c: (b, 0, 0, 0))],
        out_shape=[jax.ShapeDtypeStruct((batch * seq, G_VW), BF16),
                   jax.ShapeDtypeStruct((batch, G_HEADS, G_DK, G_DV), F32)],
        compiler_params=_params(2),
        name="gla_prompt",
    )(qg, kg, vg, la, rg, gn)


def _gla_sample_kernel(q_ref, k_ref, v_ref, la_ref, rg_ref, gn_ref, s0_ref, o_ref, s_ref,
                       *, bb, t_new):
    def one_head(i, h):
        trow = lax.broadcasted_iota(jnp.int32, (t_new, G_DK), 0)
        arow = lax.broadcasted_iota(jnp.int32, (t_new, LANES), 0)
        acol = lax.broadcasted_iota(jnp.int32, (t_new, LANES), 1)
        zk = jnp.zeros((LANES - t_new, G_DK), F32)
        zv = jnp.zeros((LANES - t_new, G_DV), F32)
        ks = slice(h * G_DK, (h + 1) * G_DK)
        vs = slice(h * G_DV, (h + 1) * G_DV)
        b = la_ref[i, :, ks]
        shift = 1
        while shift < t_new:
            b = b + jnp.where(trow >= shift, pltpu.roll(b, shift, 0), 0.0)
            shift *= 2
        q = q_ref[i, :, ks].astype(F32)
        k = k_ref[i, :, ks].astype(F32)
        v = jnp.concatenate([v_ref[i, :, vs].astype(F32), zv], axis=0).astype(BF16)
        b_last = b[t_new - 1:t_new, :]
        s0 = s0_ref[i, h]
        qe = (q * jnp.exp(b)).astype(BF16)
        ke = jnp.concatenate([k * jnp.exp(-b), zk], axis=0).astype(BF16)
        att = jnp.where(acol <= arow, _dot_nt(qe, ke), 0.0)
        out = _dot(att.astype(BF16), v) + _dot(qe, s0.astype(BF16))
        kd = jnp.concatenate([k * jnp.exp(b_last - b), zk], axis=0)
        s_ref[i, h] = s0 * _col_bcast(jnp.exp(b_last), G_DV) + _dot(kd.T.astype(BF16), v)
        o_ref[i, :, vs] = _gla_out(out, rg_ref[i, :, vs].astype(F32), gn_ref[...])

    def seq_pair(i, carry):
        for half in range(2):
            for h in range(G_HEADS):
                one_head(2 * i + half, h)
        return carry

    assert bb % 2 == 0
    lax.fori_loop(0, bb // 2, seq_pair, 0)


def _gla_sample(qg, kg, vg, la, rg, gn, s0, batch, t_new, bb):
    def spec(width):
        return pl.BlockSpec((bb, t_new, width), lambda b: (b, 0, 0))
    st_spec = pl.BlockSpec((bb, G_HEADS, G_DK, G_DV), lambda b: (b, 0, 0, 0))
    o, s = pl.pallas_call(
        functools.partial(_gla_sample_kernel, bb=bb, t_new=t_new),
        grid=(batch // bb,),
        in_specs=[spec(G_KW), spec(G_KW), spec(G_VW), spec(G_KW), spec(G_VW),
                  _resident((1, G_DV)), st_spec],
        out_specs=[spec(G_VW), st_spec],
        out_shape=[jax.ShapeDtypeStruct((batch, t_new, G_VW), BF16),
                   jax.ShapeDtypeStruct((batch, G_HEADS, G_DK, G_DV), F32)],
        compiler_params=_params(1),
        name="gla_sample",
    )(qg.reshape(batch, t_new, G_KW), kg.reshape(batch, t_new, G_KW),
      vg.reshape(batch, t_new, G_VW), la.reshape(batch, t_new, G_KW),
      rg.reshape(batch, t_new, G_VW), gn, s0)
    return o.reshape(batch * t_new, G_VW), s


def _out_kernel(x_ref, a_ref, gb_ref, gate_ref, wa_ref, wb_ref, wo_ref, g_ref, wg_ref, wu_ref,
                wd_ref, o_ref, act_ref):
    a_out = _dot(a_ref[...], wa_ref[...])
    b_out = _dot(gb_ref[...], wb_ref[...])
    mix = (gate_ref[:, 0:D_MODEL].astype(F32) * a_out
           + gate_ref[:, D_MODEL:2 * D_MODEL].astype(F32) * b_out)
    x2 = x_ref[...] + _dot(mix.astype(BF16), wo_ref[...])
    o_ref[...] = _swiglu_residual(x2, g_ref, wg_ref, wu_ref, wd_ref, act_ref)


def _out(x1, attn, gb, gates, wa, wb, wo, g, wg, wu, wd, tm):
    t = x1.shape[0]

    def tok(width):
        return pl.BlockSpec((tm, width), lambda i: (i, 0))
    consts = [wa, wb, wo, g, wg, wu, wd]
    return pl.pallas_call(
        _out_kernel,
        grid=(t // tm,),
        in_specs=[tok(D_MODEL), tok(A_WIDTH), tok(G_VW), tok(2 * D_MODEL)]
        + [_resident(c.shape) for c in consts],
        out_specs=tok(D_MODEL),
        out_shape=jax.ShapeDtypeStruct((t, D_MODEL), F32),
        scratch_shapes=[pltpu.VMEM((tm, D_FF), BF16)],
        compiler_params=_params(1),
        name="out_ffn2",
    )(x1, attn, gb, gates, *consts)


KV_TILE = 512


def _kv_rows_kernel(k_ref, v_ref, *out_refs, n_tiles):
    j = pl.program_id(1)
    wide = N_GROUPS - 1
    for kv, src in enumerate((k_ref, v_ref)):
        out_refs[wide][kv] = src[:, wide * A_WIDTH:(wide + 1) * A_WIDTH].T

    @pl.when(j == n_tiles - 1)
    def _():
        for g in range(wide):
            keep = out_refs[g].shape[-1]
            for kv, src in enumerate((k_ref, v_ref)):
                out_refs[g][kv] = src[KV_TILE - keep:, g * A_WIDTH:(g + 1) * A_WIDTH].T


def _kv_rows(ka, va, batch, seq):
    windows = [BAND * d for d in SWA_DILATIONS]
    assert windows[-1] <= seq and windows[-1] % KV_TILE == 0 and windows[-2] <= KV_TILE
    n_tiles = windows[-1] // KV_TILE
    first = (seq - windows[-1]) // KV_TILE
    per_seq = seq // KV_TILE
    tok = pl.BlockSpec((KV_TILE, N_GROUPS * A_WIDTH), lambda b, j: (b * per_seq + first + j, 0))
    out_specs = [pl.BlockSpec((None, 2, A_WIDTH, w), lambda b, j: (b, 0, 0, 0))
                 for w in windows[:-1]]
    out_specs.append(pl.BlockSpec((None, 2, A_WIDTH, KV_TILE), lambda b, j: (b, 0, 0, j)))
    outs = pl.pallas_call(
        functools.partial(_kv_rows_kernel, n_tiles=n_tiles),
        grid=(batch, n_tiles),
        in_specs=[tok, tok],
        out_specs=out_specs,
        out_shape=[jax.ShapeDtypeStruct((batch, 2, A_WIDTH, w), F32) for w in windows],
        compiler_params=_params(2),
        name="kv_rows",
    )(ka, va)
    return [jnp.transpose(o.reshape(batch, 2, A_HEADS, A_HEAD_DIM, w), (0, 4, 1, 2, 3))
            for o, w in zip(outs, windows)]


def _cache_rows(ka, va, g, batch, seq, keep):
    k = ka.reshape(batch, seq, N_GROUPS, A_HEADS, A_HEAD_DIM)[:, seq - keep:, g]
    v = va.reshape(batch, seq, N_GROUPS, A_HEADS, A_HEAD_DIM)[:, seq - keep:, g]
    return jnp.stack([k, v], axis=2)


def kernel(x_prompt, x_sample, cache_swa128_kv, cache_swa512_kv, cache_swa2048_kv, state_gla,
           norm_ffn1, ffn1_gate, ffn1_up, ffn1_down, norm_mix, w_in, a_q_norm, a_k_norm,
           g_alpha_up, g_alpha_bias, g_out_norm, w_a_out, w_b_out, w_out,
           norm_ffn2, ffn2_gate, ffn2_up, ffn2_down):
    batch, seq, _ = x_prompt.shape
    dec_batch, dec_seq, _ = x_sample.shape

    aw = N_GROUPS * A_WIDTH
    o_g = 3 * aw
    o_lr = o_g + 2 * G_KW + 2 * G_VW
    o_gate = o_lr + G_LOWRANK
    w_attn = w_in[:, :o_g].astype(BF16)
    w_gla = w_in[:, o_g:o_lr].astype(BF16)
    w_lr = jnp.pad(w_in[:, o_lr:o_gate], ((0, 0), (0, LANES - G_LOWRANK))).astype(BF16)
    w_gate = w_in[:, o_gate:].astype(BF16)
    up = jnp.pad(g_alpha_up, ((0, LANES - G_LOWRANK), (0, 0))).astype(BF16)
    bias = g_alpha_bias.reshape(1, G_KW)
    qgain = jnp.tile(a_q_norm, (1, A_HEADS)).reshape(1, aw)
    kgain = jnp.tile(a_k_norm, (1, A_HEADS)).reshape(1, aw)
    lane = jnp.arange(A_WIDTH) // A_HEAD_DIM
    hmean = jnp.where(lane[:, None] == lane[None, :], 1.0 / A_HEAD_DIM, 0.0).astype(BF16)
    gn = g_out_norm.reshape(1, G_DV)
    ffn1 = (norm_ffn1.reshape(1, D_MODEL), ffn1_gate.astype(BF16), ffn1_up.astype(BF16),
            ffn1_down.astype(BF16))
    ffn2 = (norm_ffn2.reshape(1, D_MODEL), ffn2_gate.astype(BF16), ffn2_up.astype(BF16),
            ffn2_down.astype(BF16))
    proj_w = (norm_mix.reshape(1, D_MODEL), w_attn, w_gla, w_lr, w_gate, up, bias,
              qgain, kgain, hmean)
    out_w = (w_a_out.astype(BF16), w_b_out.astype(BF16), w_out.astype(BF16))

    x1p = _ffn(x_prompt.reshape(batch * seq, D_MODEL), *ffn1, tm=512)
    *qkv_units, ka, va, qg, kg, vg, rg, la, gates = _proj(x1p, *proj_w, tm=512, blocked=True)
    a_p = _attn_prompt(qkv_units, batch, seq)
    gbp, state_p = _gla_prompt(qg, kg, vg, la, rg, gn, batch, seq)
    y_prompt = _out(x1p, a_p, gbp, gates, *out_w, *ffn2, tm=256).reshape(batch, seq, D_MODEL)
    rows_p = _kv_rows(ka, va, batch, seq)

    x1s = _ffn(x_sample.reshape(dec_batch * dec_seq, D_MODEL), *ffn1, tm=512)
    qa, ka, va, qg, kg, vg, rg, la, gates = _proj(x1s, *proj_w, tm=512, blocked=False)
    a_s = _attn_sample(qa, ka, va, cache_swa128_kv, cache_swa512_kv, cache_swa2048_kv,
                       dec_batch, dec_seq)
    gbs, state_s = _gla_sample(qg, kg, vg, la, rg, gn, state_gla, dec_batch, dec_seq, bb=8)
    y_sample = _out(x1s, a_s, gbs, gates, *out_w, *ffn2, tm=256).reshape(
        dec_batch, dec_seq, D_MODEL)
    rows_s = [_cache_rows(ka, va, g, dec_batch, dec_seq, dec_seq) for g in range(N_GROUPS)]

    return (y_prompt, y_sample, rows_p[0], rows_p[1], rows_p[2], state_p,
            rows_s[0], rows_s[1], rows_s[2], state_s)
```

```python
import functools

import jax
import jax.numpy as jnp
from jax import lax
from jax.experimental import pallas as pl
from jax.experimental.pallas import tpu as pltpu

F32 = jnp.float32
BF16 = jnp.bfloat16

D_MODEL = 1024
D_FF = 2816
SWA_DILATIONS = (1, 4, 16)
N_GROUPS = 3
A_HEADS = 4
A_HEAD_DIM = 64
A_WIDTH = A_HEADS * A_HEAD_DIM
BAND = 128
G_HEADS = 4
G_DK = 128
G_DV = 256
G_KW = G_HEADS * G_DK
G_VW = G_HEADS * G_DV
G_LOWRANK = 16
G_TAU = 16.0
EPS = 1e-6

LANES = 128
GLA_BLOCK = 128
GLA_SUB = 32
FF_CHUNK = 256
N_CHUNK = 512
VMEM_LIMIT = 58 * 1024 * 1024


def _resident(shape):
    nd = len(shape)
    return pl.BlockSpec(shape, lambda *_: (0,) * nd, pipeline_mode=pl.Buffered(1))


def _params(n_axes):
    return pltpu.CompilerParams(
        dimension_semantics=("arbitrary",) * n_axes, vmem_limit_bytes=VMEM_LIMIT)


def _dot(a, b):
    return jnp.dot(a, b, preferred_element_type=F32)


def _dot_nt(a, b):
    return lax.dot_general(a, b, (((1,), (1,)), ((), ())), preferred_element_type=F32)


def _rmsnorm(x, g):
    return x * lax.rsqrt(jnp.mean(x * x, axis=-1, keepdims=True) + EPS) * g


def _swiglu_residual(x, g_ref, wg_ref, wu_ref, wd_ref, act_ref):
    h = _rmsnorm(x, g_ref[...]).astype(BF16)
    for c in range(D_FF // FF_CHUNK):
        sl = slice(c * FF_CHUNK, (c + 1) * FF_CHUNK)
        gate = _dot(h, wg_ref[:, sl])
        up = _dot(h, wu_ref[:, sl])
        act_ref[:, sl] = (gate * jax.nn.sigmoid(gate) * up).astype(BF16)
    return x + 0.5 * _dot(act_ref[...], wd_ref[...])


def _ffn_kernel(x_ref, g_ref, wg_ref, wu_ref, wd_ref, o_ref, act_ref):
    o_ref[...] = _swiglu_residual(x_ref[...], g_ref, wg_ref, wu_ref, wd_ref, act_ref)


def _ffn(x, g, wg, wu, wd, tm):
    t = x.shape[0]
    return pl.pallas_call(
        _ffn_kernel,
        grid=(t // tm,),
        in_specs=[
            pl.BlockSpec((tm, D_MODEL), lambda i: (i, 0)),
            _resident((1, D_MODEL)),
            _resident((D_MODEL, D_FF)),
            _resident((D_MODEL, D_FF)),
            _resident((D_FF, D_MODEL)),
        ],
        out_specs=pl.BlockSpec((tm, D_MODEL), lambda i: (i, 0)),
        out_shape=jax.ShapeDtypeStruct((t, D_MODEL), F32),
        scratch_shapes=[pltpu.VMEM((tm, D_FF), BF16)],
        compiler_params=_params(1),
        name="ffn1",
    )(x, g, wg, wu, wd)


def _log_sigmoid(z):
    return jnp.minimum(z, 0.0) - jnp.log1p(jnp.exp(-jnp.abs(z)))


def _store_blocked(x, d, out_ref, perm_ref):
    tm = x.shape[0]
    if d == 1:
        for j in range(tm // BAND):
            out_ref[j] = x[j * BAND:(j + 1) * BAND].astype(BF16)
        return
    for s in range(A_WIDTH // LANES):
        perm_ref[s] = x[:, s * LANES:(s + 1) * LANES]
    for r in range(d):
        rows = jnp.concatenate(
            [perm_ref[s, pl.ds(r, tm // d, stride=d), :] for s in range(A_WIDTH // LANES)], axis=1)
        out_ref[r] = rows.astype(BF16)


def _proj_kernel(*refs, blocked):
    (x_ref, g_ref, wa_ref, wg_ref, wlr_ref, wgate_ref, up_ref, bias_ref,
     qgain_ref, kgain_ref, hmean_ref) = refs[:11]
    outs = refs[11:]
    if blocked:
        qb_refs, kb_refs, vb_refs = outs[0:3], outs[3:6], outs[6:9]
        ka_ref, va_ref, qg_ref, kg_ref, vg_ref, rg_ref, la_ref, gate_ref, perm_ref = outs[9:]
    else:
        qa_ref, ka_ref, va_ref, qg_ref, kg_ref, vg_ref, rg_ref, la_ref, gate_ref = outs

    h = _rmsnorm(x_ref[...], g_ref[...]).astype(BF16)
    hmean = hmean_ref[...]

    def head_norm(x, gain):
        ms = _dot((x * x).astype(BF16), hmean)
        return x * lax.rsqrt(ms + EPS) * gain

    aw = N_GROUPS * A_WIDTH
    for g, d in enumerate(SWA_DILATIONS):
        sl = slice(g * A_WIDTH, (g + 1) * A_WIDTH)
        q = _dot(h, wa_ref[:, g * A_WIDTH:(g + 1) * A_WIDTH])
        k = _dot(h, wa_ref[:, aw + g * A_WIDTH:aw + (g + 1) * A_WIDTH])
        v = _dot(h, wa_ref[:, 2 * aw + g * A_WIDTH:2 * aw + (g + 1) * A_WIDTH])
        q = head_norm(q, qgain_ref[:, sl]) * (A_HEAD_DIM ** -0.5)
        k = head_norm(k, kgain_ref[:, sl])
        ka_ref[:, sl] = k
        va_ref[:, sl] = v
        if blocked:
            for i, (x, o_refs) in enumerate(((q, qb_refs), (k, kb_refs), (v, vb_refs))):
                _store_blocked(x, d, o_refs[g], perm_ref.at[(g - 1) * 3 + i] if d > 1 else None)
        else:
            qa_ref[:, sl] = q.astype(BF16)

    qg_ref[...] = (_dot(h, wg_ref[:, 0:G_KW]) * (G_DK ** -0.5)).astype(BF16)
    kg_ref[...] = _dot(h, wg_ref[:, G_KW:2 * G_KW]).astype(BF16)
    for c in range(G_VW // N_CHUNK):
        sl = slice(c * N_CHUNK, (c + 1) * N_CHUNK)
        off = 2 * G_KW + c * N_CHUNK
        vg_ref[:, sl] = _dot(h, wg_ref[:, off:off + N_CHUNK]).astype(BF16)
        rg_ref[:, sl] = _dot(h, wg_ref[:, G_VW + off:G_VW + off + N_CHUNK]).astype(BF16)

    lr = _dot(h, wlr_ref[...]).astype(BF16)
    z = _dot(lr, up_ref[...]) + bias_ref[...]
    la_ref[...] = _log_sigmoid(z) * (1.0 / G_TAU)

    for c in range(2 * D_MODEL // N_CHUNK):
        sl = slice(c * N_CHUNK, (c + 1) * N_CHUNK)
        gate_ref[:, sl] = jax.nn.sigmoid(_dot(h, wgate_ref[:, sl])).astype(BF16)


def _unit_spec(d, tm):
    per_class = tm // d
    if per_class >= BAND:
        return pl.BlockSpec((tm // BAND, BAND, A_WIDTH), lambda i: (i, 0, 0))
    parts = BAND // per_class
    return pl.BlockSpec((d, per_class, A_WIDTH), lambda i: (i // parts, i % parts, 0))


def _proj(x1, g, wa, wg, wlr, wgate, up, bias, qgain, kgain, hmean, tm, blocked):
    t = x1.shape[0]
    aw = N_GROUPS * A_WIDTH
    widths = [(aw, F32), (aw, F32), (G_KW, BF16), (G_KW, BF16), (G_VW, BF16), (G_VW, BF16),
              (G_KW, F32), (2 * D_MODEL, BF16)]
    out_specs = [pl.BlockSpec((tm, w), lambda i: (i, 0)) for w, _ in widths]
    out_shape = [jax.ShapeDtypeStruct((t, w), dt) for w, dt in widths]
    scratch = []
    if blocked:
        assert tm == 4 * BAND and SWA_DILATIONS == (1, 4, 16)
        out_specs = [_unit_spec(d, tm) for _ in range(3) for d in SWA_DILATIONS] + out_specs
        out_shape = [jax.ShapeDtypeStruct((t // BAND, BAND, A_WIDTH), BF16)] * 9 + out_shape
        scratch = [pltpu.VMEM((3 * (N_GROUPS - 1), A_WIDTH // LANES, tm, LANES), F32)]
    else:
        out_specs = [pl.BlockSpec((tm, aw), lambda i: (i, 0))] + out_specs
        out_shape = [jax.ShapeDtypeStruct((t, aw), BF16)] + out_shape
    consts = [g, wa, wg, wlr, wgate, up, bias, qgain, kgain, hmean]
    return pl.pallas_call(
        functools.partial(_proj_kernel, blocked=blocked),
        grid=(t // tm,),
        in_specs=[pl.BlockSpec((tm, D_MODEL), lambda i: (i, 0))]
        + [_resident(c.shape) for c in consts],
        out_specs=out_specs,
        out_shape=out_shape,
        scratch_shapes=scratch,
        compiler_params=_params(1),
        name="in_proj",
    )(x1, *consts)


def _log2(n):
    assert n > 0 and n & (n - 1) == 0, n
    return n.bit_length() - 1


def _head_masks(rows):
    lane = lax.broadcasted_iota(jnp.int32, (rows, A_WIDTH), 1)
    return [(lane >> _log2(A_HEAD_DIM)) == h for h in range(A_HEADS)]


ATTN_SPAN = 2048


def _attn_prompt_kernel(*refs, units):
    q_refs, k_refs, v_refs = refs[0:3], refs[3:6], refs[6:9]
    o_ref, osc_ref, lsc_ref, bias_ref = refs[9:]
    span = pl.program_id(1)
    n_slab = A_WIDTH // LANES

    row = lax.broadcasted_iota(jnp.int32, (A_HEADS * BAND, 2 * BAND), 0) & (BAND - 1)
    col = lax.broadcasted_iota(jnp.int32, (A_HEADS * BAND, 2 * BAND), 1)
    own = (col >= BAND) & (col - BAND <= row)
    bias_ref[0] = jnp.where(own, 0.0, -jnp.inf)
    bias_ref[1] = jnp.where(own | ((col < BAND) & (col >= row)), 0.0, -jnp.inf)

    for g, d in enumerate(SWA_DILATIONS):
        q_ref, k_ref, v_ref = q_refs[g], k_refs[g], v_refs[g]

        def scores(ul, d=d, q_ref=q_ref, k_ref=k_ref):
            u = span * units + ul
            up = jnp.maximum(u - d, 0)
            kb = jnp.concatenate([k_ref[up], k_ref[u]], axis=0)
            q = q_ref[ul]
            qm = jnp.concatenate(
                [q * m.astype(F32).astype(BF16) for m in _head_masks(BAND)], axis=0)
            has_prev = jnp.where(u >= d, 1, 0)
            return _dot_nt(qm, kb) + bias_ref[has_prev]

        def softmax(s):
            m = jnp.max(s, axis=-1, keepdims=True)
            p = jnp.exp(s - m)
            den = jnp.sum(p, axis=-1, keepdims=True)
            return p.astype(BF16), 1.0 / den, m + jnp.log(den)

        def values(ul, p, d=d, v_ref=v_ref):
            u = span * units + ul
            vb = jnp.concatenate([v_ref[jnp.maximum(u - d, 0)], v_ref[u]], axis=0)
            return _dot(p, vb)

        def store(ul, o_all, inv_den, lse_all, d=d, g=g):
            hmask = _head_masks(BAND)
            o_all = o_all * inv_den
            o_acc = jnp.zeros((BAND, A_WIDTH), F32)
            lse_acc = jnp.zeros((BAND, A_WIDTH), F32)
            for h in range(A_HEADS):
                rs = slice(h * BAND, (h + 1) * BAND)
                o_acc = jnp.where(hmask[h], o_all[rs], o_acc)
                lse_acc = jnp.where(hmask[h], lse_all[rs], lse_acc)
            start = (ul >> _log2(d)) * (BAND * d) + (ul & (d - 1))
            if d == 1:
                rows = pl.ds(pl.multiple_of(start, BAND), BAND)
            else:
                rows = pl.ds(start, BAND, stride=d)
            for sl in range(n_slab):
                osc_ref[g, sl, rows, :] = o_acc[:, sl * LANES:(sl + 1) * LANES]
                lsc_ref[g, sl, rows, :] = lse_acc[:, sl * LANES:(sl + 1) * LANES]

        def unit_pair(i, carry, scores=scores, softmax=softmax, values=values, store=store):
            uls = (2 * i, 2 * i + 1)
            ss = [scores(ul) for ul in uls]
            sm = [softmax(s) for s in ss]
            os_ = [values(ul, p) for ul, (p, _, _) in zip(uls, sm)]
            for ul, o_all, (_, inv_den, lse_all) in zip(uls, os_, sm):
                store(ul, o_all, inv_den, lse_all)
            return carry

        lax.fori_loop(0, units // 2, unit_pair, 0)

    def merge(i, carry):
        rows = pl.ds(pl.multiple_of(i * BAND, BAND), BAND)

        def natural(ref, g):
            return jnp.concatenate([ref[g, sl, rows, :] for sl in range(n_slab)], axis=1)

        outs = [natural(osc_ref, g) for g in range(N_GROUPS)]
        lses = [natural(lsc_ref, g) for g in range(N_GROUPS)]
        m = functools.reduce(jnp.maximum, lses)
        ws = [jnp.exp(l - m) for l in lses]
        a = functools.reduce(jnp.add, [w * o for w, o in zip(ws, outs)])
        o_ref[rows, :] = (a * (1.0 / functools.reduce(jnp.add, ws))).astype(BF16)
        return carry

    lax.fori_loop(0, ATTN_SPAN // BAND, merge, 0)


def _attn_prompt(qkv_units, batch, seq):
    assert seq % ATTN_SPAN == 0
    units = ATTN_SPAN // BAND
    per_seq = seq // BAND
    spans = seq // ATTN_SPAN
    q_spec = pl.BlockSpec((units, BAND, A_WIDTH), lambda b, s: (b * spans + s, 0, 0))
    kv_spec = pl.BlockSpec((per_seq, BAND, A_WIDTH), lambda b, s: (b, 0, 0))
    return pl.pallas_call(
        functools.partial(_attn_prompt_kernel, units=units),
        grid=(batch, spans),
        in_specs=[q_spec] * 3 + [kv_spec] * 6,
        out_specs=pl.BlockSpec((ATTN_SPAN, A_WIDTH), lambda b, s: (b * spans + s, 0)),
        out_shape=jax.ShapeDtypeStruct((batch * seq, A_WIDTH), BF16),
        scratch_shapes=[pltpu.VMEM((N_GROUPS, A_WIDTH // LANES, ATTN_SPAN, LANES), F32)] * 2
        + [pltpu.VMEM((2, A_HEADS * BAND, 2 * BAND), F32)],
        compiler_params=_params(2),
        name="attn_prompt",
    )(*qkv_units)


def _attn_sample_kernel(q_ref, k_ref, v_ref, c1_ref, c4_ref, c16_ref, o_ref, *, t_new, bb):
    hmask = _head_masks(t_new)
    rows = A_HEADS * t_new
    row_t = lax.broadcasted_iota(jnp.int32, (rows, LANES), 0) & (t_new - 1)
    col = lax.broadcasted_iota(jnp.int32, (rows, LANES), 1)
    pad = jnp.zeros((LANES - t_new, A_WIDTH), F32)
    caches = (c1_ref, c4_ref, c16_ref)
    jobs = [(i, g) for i in range(bb) for g in range(N_GROUPS)]

    ok_old, ok_new = [], []
    for d, c_ref in zip(SWA_DILATIONS, caches):
        window = c_ref.shape[-1]
        assert window == BAND * d, (window, d)
        pos_t = lax.broadcasted_iota(jnp.int32, (rows, window), 0) & (t_new - 1)
        pos = lax.broadcasted_iota(jnp.int32, (rows, window), 1)
        ok_old.append((pos >= pos_t) & (((pos - pos_t) & (d - 1)) == 0))
        ok_new.append((col <= row_t) & (((row_t - col) & (d - 1)) == 0))

    def new_rows(ref, i, g):
        sl = slice(g * A_WIDTH, (g + 1) * A_WIDTH)
        return jnp.concatenate([ref[i, :, sl], pad], axis=0).astype(BF16)

    def masked_q(i, g):
        q = q_ref[i, :, g * A_WIDTH:(g + 1) * A_WIDTH].astype(F32)
        return jnp.concatenate([q * m.astype(F32) for m in hmask], axis=0).astype(BF16)

    qms = [masked_q(i, g) for i, g in jobs]
    s_old = [jnp.where(ok_old[g], _dot(qm, caches[g][i, 0].astype(BF16)), -jnp.inf)
             for qm, (i, g) in zip(qms, jobs)]
    s_new = [jnp.where(ok_new[g], _dot_nt(qm, new_rows(k_ref, i, g)), -jnp.inf)
             for qm, (i, g) in zip(qms, jobs)]
    ms = [jnp.maximum(jnp.max(so, axis=-1, keepdims=True), jnp.max(sn, axis=-1, keepdims=True))
          for so, sn in zip(s_old, s_new)]
    p_old = [jnp.exp(so - m) for so, m in zip(s_old, ms)]
    p_new = [jnp.exp(sn - m) for sn, m in zip(s_new, ms)]
    dens = [jnp.sum(po, axis=-1, keepdims=True) + jnp.sum(pn, axis=-1, keepdims=True)
            for po, pn in zip(p_old, p_new)]
    accs = [_dot_nt(po.astype(BF16), caches[g][i, 1].astype(BF16))
            + _dot(pn.astype(BF16), new_rows(v_ref, i, g))
            for po, pn, (i, g) in zip(p_old, p_new, jobs)]

    for i in range(bb):
        num = jnp.zeros((t_new, A_WIDTH), F32)
        den_all = jnp.zeros((t_new, A_WIDTH), F32)
        m_all = jnp.full((t_new, A_WIDTH), -jnp.inf, F32)
        for g in range(N_GROUPS):
            j = jobs.index((i, g))
            o_g = jnp.zeros((t_new, A_WIDTH), F32)
            lse_g = jnp.zeros((t_new, A_WIDTH), F32)
            for h in range(A_HEADS):
                rs = slice(h * t_new, (h + 1) * t_new)
                o_g = jnp.where(hmask[h], accs[j][rs] * (1.0 / dens[j][rs]), o_g)
                lse_g = jnp.where(hmask[h], ms[j][rs] + jnp.log(dens[j][rs]), lse_g)
            m_next = jnp.maximum(m_all, lse_g)
            scale_old = jnp.exp(m_all - m_next)
            w = jnp.exp(lse_g - m_next)
            num = num * scale_old + w * o_g
            den_all = den_all * scale_old + w
            m_all = m_next
        o_ref[i] = (num * (1.0 / den_all)).astype(BF16)


def _attn_sample(qa, ka, va, c128, c512, c2048, batch, t_new, bb):
    w3 = N_GROUPS * A_WIDTH
    tok_spec = pl.BlockSpec((bb, t_new, w3), lambda b: (b, 0, 0))
    cache_views = [jnp.transpose(c, (0, 2, 3, 4, 1)).reshape(batch, 2, A_WIDTH, c.shape[1])
                   for c in (c128, c512, c2048)]
    cache_specs = [pl.BlockSpec((bb, 2, A_WIDTH, c.shape[-1]), lambda b: (b, 0, 0, 0))
                   for c in cache_views]
    out = pl.pallas_call(
        functools.partial(_attn_sample_kernel, t_new=t_new, bb=bb),
        grid=(batch // bb,),
        in_specs=[tok_spec, tok_spec, tok_spec] + cache_specs,
        out_specs=pl.BlockSpec((bb, t_new, A_WIDTH), lambda b: (b, 0, 0)),
        out_shape=jax.ShapeDtypeStruct((batch, t_new, A_WIDTH), BF16),
        compiler_params=_params(1),
        name="attn_sample",
    )(qa.reshape(batch, t_new, w3), ka.reshape(batch, t_new, w3), va.reshape(batch, t_new, w3),
      *cache_views)
    return out.reshape(batch * t_new, A_WIDTH)


def _split2(x):
    hi = x.astype(BF16)
    return hi, (x - hi.astype(F32)).astype(BF16)


def _gla_out(o, rg, gain):
    on = o * lax.rsqrt(jnp.mean(o * o, axis=-1, keepdims=True) + EPS) * gain
    return (on * rg * jax.nn.sigmoid(rg)).astype(BF16)


def _col_bcast(vec_row, width):
    sq = jnp.broadcast_to(vec_row, (LANES, LANES)).T
    return jnp.concatenate([sq] * (width // LANES), axis=1)


GLA_CHUNK = 1024


def _gla_prompt_kernel(q_ref, k_ref, v_ref, la_ref, rg_ref, gn_ref, o_ref, s_ref, *, n_blocks):
    cb, sub = GLA_BLOCK, GLA_SUB
    n_sub = cb // sub

    @pl.when(pl.program_id(1) == 0)
    def _():
        s_ref[...] = jnp.zeros_like(s_ref)

    heads = range(G_HEADS)

    def hk(x, h):
        return x[:, h * G_DK:(h + 1) * G_DK]

    def step(i, carry):
        st = pl.multiple_of(i * cb, cb)
        rows = pl.ds(st, cb)
        row = lax.broadcasted_iota(jnp.int32, (cb, cb), 0)
        col = lax.broadcasted_iota(jnp.int32, (cb, cb), 1)
        tri = (row >= col).astype(F32).astype(BF16)
        rowk = lax.broadcasted_iota(jnp.int32, (cb, G_KW), 0)

        hi, lo = _split2(la_ref[rows, :])
        b = _dot(tri, hi) + _dot(tri, lo)
        q = q_ref[rows, :].astype(F32)
        k = k_ref[rows, :].astype(F32)
        b_last = b[cb - 1:cb, :]
        qe = (q * jnp.exp(b)).astype(BF16)
        sub_end = [b[(j + 1) * sub - 1:(j + 1) * sub, :] for j in range(n_sub)]
        own_end = jnp.concatenate(
            [jnp.broadcast_to(e, (sub, G_KW)) for e in sub_end], axis=0)
        kk = k * jnp.exp(own_end - b)
        q_parts, k_parts = [], []
        for j in range(n_sub):
            lo_row = j * sub
            qj = q[lo_row:, :] * jnp.exp(b[lo_row:, :] - sub_end[j])
            q_parts.append(jnp.concatenate(
                [jnp.zeros((lo_row, G_KW), F32), qj], axis=0).astype(BF16) if lo_row else
                qj.astype(BF16))
            in_sub = (rowk >= lo_row) & (rowk < lo_row + sub)
            k_parts.append(jnp.where(in_sub, kk, 0.0).astype(BF16))
        kd = k * jnp.exp(b_last - b)
        decay = jnp.exp(b_last)

        states = [s_ref[h] for h in heads]
        vals = [v_ref[rows, h * G_DV:(h + 1) * G_DV] for h in heads]
        inter = [_dot(hk(qe, h), states[h].astype(BF16)) for h in heads]
        atts = [_dot_nt(jnp.concatenate([hk(p, h) for p in q_parts], axis=1),
                        jnp.concatenate([hk(p, h) for p in k_parts], axis=1)) for h in heads]
        atts = [jnp.where(col <= row, a, 0.0).astype(BF16) for a in atts]
        outs = [inter[h] + _dot(atts[h], vals[h]) for h in heads]
        for h in heads:
            kd_t = hk(kd, h).T.astype(BF16)
            s_ref[h] = states[h] * _col_bcast(hk(decay, h), G_DV) + _dot(kd_t, vals[h])
        for h in heads:
            vs = slice(h * G_DV, (h + 1) * G_DV)
            o_ref[rows, vs] = _gla_out(outs[h], rg_ref[rows, vs].astype(F32), gn_ref[...])
        return carry

    lax.fori_loop(0, n_blocks, step, 0)


def _gla_prompt(qg, kg, vg, la, rg, gn, batch, seq):
    assert seq % GLA_CHUNK == 0
    chunks = seq // GLA_CHUNK

    def spec(width):
        return pl.BlockSpec((GLA_CHUNK, width), lambda b, c: (b * chunks + c, 0))
    return pl.pallas_call(
        functools.partial(_gla_prompt_kernel, n_blocks=GLA_CHUNK // GLA_BLOCK),
        grid=(batch, chunks),
        in_specs=[spec(G_KW), spec(G_KW), spec(G_VW), spec(G_KW), spec(G_VW),
                  _resident((1, G_DV))],
        out_specs=[spec(G_VW),
                   pl.BlockSpec((None, G_HEADS, G_DK, G_DV), lambda b, c: (b, 0, 0, 0))],
        out_shape=[jax.ShapeDtypeStruct((batch * seq, G_VW), BF16),
                   jax.ShapeDtypeStruct((batch, G_HEADS, G_DK, G_DV), F32)],
        compiler_params=_params(2),
        name="gla_prompt",
    )(qg, kg, vg, la, rg, gn)


def _gla_sample_kernel(q_ref, k_ref, v_ref, la_ref, rg_ref, gn_ref, s0_ref, o_ref, s_ref,
                       *, bb, t_new):
    per_body = 2

    def hk(x, h):
        return x[:, h * G_DK:(h + 1) * G_DK]

    def hv(x, h):
        return x[:, h * G_DV:(h + 1) * G_DV]

    def body(i0, carry):
        trow = lax.broadcasted_iota(jnp.int32, (t_new, G_KW), 0)
        arow = lax.broadcasted_iota(jnp.int32, (t_new, LANES), 0)
        acol = lax.broadcasted_iota(jnp.int32, (t_new, LANES), 1)
        zk = jnp.zeros((LANES - t_new, G_KW), F32)
        zv = jnp.zeros((LANES - t_new, G_VW), F32)
        seqs = [per_body * i0 + j for j in range(per_body)]
        jobs = [(n, h) for n in range(per_body) for h in range(G_HEADS)]

        bs = []
        for i in seqs:
            b = la_ref[i]
            shift = 1
            while shift < t_new:
                b = b + jnp.where(trow >= shift, pltpu.roll(b, shift, 0), 0.0)
                shift *= 2
            bs.append(b)
        qs = [q_ref[i].astype(F32) for i in seqs]
        ks = [k_ref[i].astype(F32) for i in seqs]
        vs = [jnp.concatenate([v_ref[i].astype(F32), zv], axis=0).astype(BF16) for i in seqs]
        b_last = [b[t_new - 1:t_new, :] for b in bs]
        qe = [(q * jnp.exp(b)).astype(BF16) for q, b in zip(qs, bs)]
        ke = [jnp.concatenate([k * jnp.exp(-b), zk], axis=0).astype(BF16) for k, b in zip(ks, bs)]
        kd = [jnp.concatenate([k * jnp.exp(bl - b), zk], axis=0)
              for k, b, bl in zip(ks, bs, b_last)]
        decay = [jnp.exp(bl) for bl in b_last]

        s0 = [s0_ref[seqs[n], h] for n, h in jobs]
        att = [jnp.where(acol <= arow, _dot_nt(hk(qe[n], h), hk(ke[n], h)), 0.0).astype(BF16)
               for n, h in jobs]
        out = [_dot(a, hv(vs[n], h)) + _dot(hk(qe[n], h), s.astype(BF16))
               for a, s, (n, h) in zip(att, s0, jobs)]
        for s, (n, h) in zip(s0, jobs):
            s_ref[seqs[n], h] = (s * _col_bcast(hk(decay[n], h), G_DV)
                                 + _dot(hk(kd[n], h).T.astype(BF16), hv(vs[n], h)))
        for o, (n, h) in zip(out, jobs):
            sl = slice(h * G_DV, (h + 1) * G_DV)
            o_ref[seqs[n], :, sl] = _gla_out(o, rg_ref[seqs[n], :, sl].astype(F32), gn_ref[...])
        return carry

    assert bb % per_body == 0
    lax.fori_loop(0, bb // per_body, body, 0)


def _gla_sample(qg, kg, vg, la, rg, gn, s0, batch, t_new, bb):
    def spec(width):
        return pl.BlockSpec((bb, t_new, width), lambda b: (b, 0, 0))
    st_spec = pl.BlockSpec((bb, G_HEADS, G_DK, G_DV), lambda b: (b, 0, 0, 0))
    o, s = pl.pallas_call(
        functools.partial(_gla_sample_kernel, bb=bb, t_new=t_new),
        grid=(batch // bb,),
        in_specs=[spec(G_KW), spec(G_KW), spec(G_VW), spec(G_KW), spec(G_VW),
                  _resident((1, G_DV)), st_spec],
        out_specs=[spec(G_VW), st_spec],
        out_shape=[jax.ShapeDtypeStruct((batch, t_new, G_VW), BF16),
                   jax.ShapeDtypeStruct((batch, G_HEADS, G_DK, G_DV), F32)],
        compiler_params=_params(1),
        name="gla_sample",
    )(qg.reshape(batch, t_new, G_KW), kg.reshape(batch, t_new, G_KW),
      vg.reshape(batch, t_new, G_VW), la.reshape(batch, t_new, G_KW),
      rg.reshape(batch, t_new, G_VW), gn, s0)
    return o.reshape(batch * t_new, G_VW), s


def _out_kernel(x_ref, a_ref, gb_ref, gate_ref, wa_ref, wb_ref, wo_ref, g_ref, wg_ref, wu_ref,
                wd_ref, o_ref, act_ref):
    a_out = _dot(a_ref[...], wa_ref[...])
    b_out = _dot(gb_ref[...], wb_ref[...])
    mix = (gate_ref[:, 0:D_MODEL].astype(F32) * a_out
           + gate_ref[:, D_MODEL:2 * D_MODEL].astype(F32) * b_out)
    x2 = x_ref[...] + _dot(mix.astype(BF16), wo_ref[...])
    o_ref[...] = _swiglu_residual(x2, g_ref, wg_ref, wu_ref, wd_ref, act_ref)


def _out(x1, attn, gb, gates, wa, wb, wo, g, wg, wu, wd, tm):
    t = x1.shape[0]

    def tok(width):
        return pl.BlockSpec((tm, width), lambda i: (i, 0))
    consts = [wa, wb, wo, g, wg, wu, wd]
    return pl.pallas_call(
        _out_kernel,
        grid=(t // tm,),
        in_specs=[tok(D_MODEL), tok(A_WIDTH), tok(G_VW), tok(2 * D_MODEL)]
        + [_resident(c.shape) for c in consts],
        out_specs=tok(D_MODEL),
        out_shape=jax.ShapeDtypeStruct((t, D_MODEL), F32),
        scratch_shapes=[pltpu.VMEM((tm, D_FF), BF16)],
        compiler_params=_params(1),
        name="out_ffn2",
    )(x1, attn, gb, gates, *consts)


KV_TILE = 512


def _kv_rows_kernel(k_ref, v_ref, *out_refs, n_tiles):
    j = pl.program_id(1)
    wide = N_GROUPS - 1
    for kv, src in enumerate((k_ref, v_ref)):
        out_refs[wide][kv] = src[:, wide * A_WIDTH:(wide + 1) * A_WIDTH].T

    @pl.when(j == n_tiles - 1)
    def _():
        for g in range(wide):
            keep = out_refs[g].shape[-1]
            for kv, src in enumerate((k_ref, v_ref)):
                out_refs[g][kv] = src[KV_TILE - keep:, g * A_WIDTH:(g + 1) * A_WIDTH].T


def _kv_rows(ka, va, batch, seq):
    windows = [BAND * d for d in SWA_DILATIONS]
    assert windows[-1] <= seq and windows[-1] % KV_TILE == 0 and windows[-2] <= KV_TILE
    n_tiles = windows[-1] // KV_TILE
    first = (seq - windows[-1]) // KV_TILE
    per_seq = seq // KV_TILE
    tok = pl.BlockSpec((KV_TILE, N_GROUPS * A_WIDTH), lambda b, j: (b * per_seq + first + j, 0))
    out_specs = [pl.BlockSpec((None, 2, A_WIDTH, w), lambda b, j: (b, 0, 0, 0))
                 for w in windows[:-1]]
    out_specs.append(pl.BlockSpec((None, 2, A_WIDTH, KV_TILE), lambda b, j: (b, 0, 0, j)))
    outs = pl.pallas_call(
        functools.partial(_kv_rows_kernel, n_tiles=n_tiles),
        grid=(batch, n_tiles),
        in_specs=[tok, tok],
        out_specs=out_specs,
        out_shape=[jax.ShapeDtypeStruct((batch, 2, A_WIDTH, w), F32) for w in windows],
        compiler_params=_params(2),
        name="kv_rows",
    )(ka, va)
    return [jnp.transpose(o.reshape(batch, 2, A_HEADS, A_HEAD_DIM, w), (0, 4, 1, 2, 3))
            for o, w in zip(outs, windows)]


def _cache_rows(ka, va, g, batch, seq, keep):
    k = ka.reshape(batch, seq, N_GROUPS, A_HEADS, A_HEAD_DIM)[:, seq - keep:, g]
    v = va.reshape(batch, seq, N_GROUPS, A_HEADS, A_HEAD_DIM)[:, seq - keep:, g]
    return jnp.stack([k, v], axis=2)


def kernel(x_prompt, x_sample, cache_swa128_kv, cache_swa512_kv, cache_swa2048_kv, state_gla,
           norm_ffn1, ffn1_gate, ffn1_up, ffn1_down, norm_mix, w_in, a_q_norm, a_k_norm,
           g_alpha_up, g_alpha_bias, g_out_norm, w_a_out, w_b_out, w_out,
           norm_ffn2, ffn2_gate, ffn2_up, ffn2_down):
    batch, seq, _ = x_prompt.shape
    dec_batch, dec_seq, _ = x_sample.shape

    aw = N_GROUPS * A_WIDTH
    o_g = 3 * aw
    o_lr = o_g + 2 * G_KW + 2 * G_VW
    o_gate = o_lr + G_LOWRANK
    w_attn = w_in[:, :o_g].astype(BF16)
    w_gla = w_in[:, o_g:o_lr].astype(BF16)
    w_lr = jnp.pad(w_in[:, o_lr:o_gate], ((0, 0), (0, LANES - G_LOWRANK))).astype(BF16)
    w_gate = w_in[:, o_gate:].astype(BF16)
    up = jnp.pad(g_alpha_up, ((0, LANES - G_LOWRANK), (0, 0))).astype(BF16)
    bias = g_alpha_bias.reshape(1, G_KW)
    qgain = jnp.tile(a_q_norm, (1, A_HEADS)).reshape(1, aw)
    kgain = jnp.tile(a_k_norm, (1, A_HEADS)).reshape(1, aw)
    lane = jnp.arange(A_WIDTH) // A_HEAD_DIM
    hmean = jnp.where(lane[:, None] == lane[None, :], 1.0 / A_HEAD_DIM, 0.0).astype(BF16)
    gn = g_out_norm.reshape(1, G_DV)
    ffn1 = (norm_ffn1.reshape(1, D_MODEL), ffn1_gate.astype(BF16), ffn1_up.astype(BF16),
            ffn1_down.astype(BF16))
    ffn2 = (norm_ffn2.reshape(1, D_MODEL), ffn2_gate.astype(BF16), ffn2_up.astype(BF16),
            ffn2_down.astype(BF16))
    proj_w = (norm_mix.reshape(1, D_MODEL), w_attn, w_gla, w_lr, w_gate, up, bias,
              qgain, kgain, hmean)
    out_w = (w_a_out.astype(BF16), w_b_out.astype(BF16), w_out.astype(BF16))

    x1p = _ffn(x_prompt.reshape(batch * seq, D_MODEL), *ffn1, tm=512)
    *qkv_units, ka, va, qg, kg, vg, rg, la, gates = _proj(x1p, *proj_w, tm=512, blocked=True)
    a_p = _attn_prompt(qkv_units, batch, seq)
    gbp, state_p = _gla_prompt(qg, kg, vg, la, rg, gn, batch, seq)
    y_prompt = _out(x1p, a_p, gbp, gates, *out_w, *ffn2, tm=512).reshape(batch, seq, D_MODEL)
    rows_p = _kv_rows(ka, va, batch, seq)

    x1s = _ffn(x_sample.reshape(dec_batch * dec_seq, D_MODEL), *ffn1, tm=512)
    qa, ka, va, qg, kg, vg, rg, la, gates = _proj(x1s, *proj_w, tm=512, blocked=False)
    a_s = _attn_sample(qa, ka, va, cache_swa128_kv, cache_swa512_kv, cache_swa2048_kv,
                       dec_batch, dec_seq, bb=2)
    gbs, state_s = _gla_sample(qg, kg, vg, la, rg, gn, state_gla, dec_batch, dec_seq, bb=8)
    y_sample = _out(x1s, a_s, gbs, gates, *out_w, *ffn2, tm=512).reshape(
        dec_batch, dec_seq, D_MODEL)
    rows_s = [_cache_rows(ka, va, g, dec_batch, dec_seq, dec_seq) for g in range(N_GROUPS)]

    return (y_prompt, y_sample, rows_p[0], rows_p[1], rows_p[2], state_p,
            rows_s[0], rows_s[1], rows_s[2], state_s)
```

```python
import functools

import jax
import jax.numpy as jnp
from jax import lax
from jax.experimental import pallas as pl
from jax.experimental.pallas import tpu as pltpu

F32 = jnp.float32
BF16 = jnp.bfloat16

D_MODEL = 1024
D_FF = 2816
SWA_DILATIONS = (1, 4, 16)
N_GROUPS = 3
A_HEADS = 4
A_HEAD_DIM = 64
A_WIDTH = A_HEADS * A_HEAD_DIM
BAND = 128
G_HEADS = 4
G_DK = 128
G_DV = 256
G_KW = G_HEADS * G_DK
G_VW = G_HEADS * G_DV
G_LOWRANK = 16
G_TAU = 16.0
EPS = 1e-6

LANES = 128
GLA_BLOCK = 128
GLA_SUB = 32
FF_CHUNK = 256
N_CHUNK = 512
VMEM_LIMIT = 58 * 1024 * 1024


def _resident(shape):
    nd = len(shape)
    return pl.BlockSpec(shape, lambda *_: (0,) * nd, pipeline_mode=pl.Buffered(1))


def _params(n_axes):
    return pltpu.CompilerParams(
        dimension_semantics=("arbitrary",) * n_axes, vmem_limit_bytes=VMEM_LIMIT)


def _dot(a, b):
    return jnp.dot(a, b, preferred_element_type=F32)


def _dot_nt(a, b):
    return lax.dot_general(a, b, (((1,), (1,)), ((), ())), preferred_element_type=F32)


def _rmsnorm(x, g):
    return x * lax.rsqrt(jnp.mean(x * x, axis=-1, keepdims=True) + EPS) * g


def _swiglu_residual(x, g_ref, wg_ref, wu_ref, wd_ref, act_ref):
    h = _rmsnorm(x, g_ref[...]).astype(BF16)
    for c in range(D_FF // FF_CHUNK):
        sl = slice(c * FF_CHUNK, (c + 1) * FF_CHUNK)
        gate = _dot(h, wg_ref[:, sl])
        up = _dot(h, wu_ref[:, sl])
        act_ref[:, sl] = (gate * jax.nn.sigmoid(gate) * up).astype(BF16)
    return x + 0.5 * _dot(act_ref[...], wd_ref[...])


def _ffn_kernel(x_ref, g_ref, wg_ref, wu_ref, wd_ref, o_ref, act_ref):
    o_ref[...] = _swiglu_residual(x_ref[...], g_ref, wg_ref, wu_ref, wd_ref, act_ref)


def _ffn(x, g, wg, wu, wd, tm):
    t = x.shape[0]
    return pl.pallas_call(
        _ffn_kernel,
        grid=(t // tm,),
        in_specs=[
            pl.BlockSpec((tm, D_MODEL), lambda i: (i, 0)),
            _resident((1, D_MODEL)),
            _resident((D_MODEL, D_FF)),
            _resident((D_MODEL, D_FF)),
            _resident((D_FF, D_MODEL)),
        ],
        out_specs=pl.BlockSpec((tm, D_MODEL), lambda i: (i, 0)),
        out_shape=jax.ShapeDtypeStruct((t, D_MODEL), F32),
        scratch_shapes=[pltpu.VMEM((tm, D_FF), BF16)],
        compiler_params=_params(1),
        name="ffn1",
    )(x, g, wg, wu, wd)


def _log_sigmoid(z):
    return jnp.minimum(z, 0.0) - jnp.log1p(jnp.exp(-jnp.abs(z)))


def _store_blocked(x, d, out_ref, perm_ref):
    tm = x.shape[0]
    if d == 1:
        for j in range(tm // BAND):
            out_ref[j] = x[j * BAND:(j + 1) * BAND].astype(BF16)
        return
    for s in range(A_WIDTH // LANES):
        perm_ref[s] = x[:, s * LANES:(s + 1) * LANES]
    for r in range(d):
        rows = jnp.concatenate(
            [perm_ref[s, pl.ds(r, tm // d, stride=d), :] for s in range(A_WIDTH // LANES)], axis=1)
        out_ref[r] = rows.astype(BF16)


def _proj_kernel(*refs, blocked, tiles_per_seq):
    (x_ref, g_ref, wa_ref, wg_ref, wlr_ref, wgate_ref, up_ref, bias_ref,
     qgain_ref, kgain_ref, hmean_ref) = refs[:11]
    outs = refs[11:]
    if blocked:
        qb_refs, kb_refs, vb_refs, row_refs = outs[0:3], outs[3:6], outs[6:9], outs[9:12]
        qg_ref, kg_ref, vg_ref, rg_ref, la_ref, gate_ref, perm_ref = outs[12:]
        tm = x_ref.shape[0]
    else:
        qa_ref, ka_ref, va_ref, qg_ref, kg_ref, vg_ref, rg_ref, la_ref, gate_ref = outs

    h = _rmsnorm(x_ref[...], g_ref[...]).astype(BF16)
    hmean = hmean_ref[...]

    def head_norm(x, gain):
        ms = _dot((x * x).astype(BF16), hmean)
        return x * lax.rsqrt(ms + EPS) * gain

    aw = N_GROUPS * A_WIDTH
    for g, d in enumerate(SWA_DILATIONS):
        sl = slice(g * A_WIDTH, (g + 1) * A_WIDTH)
        q = _dot(h, wa_ref[:, g * A_WIDTH:(g + 1) * A_WIDTH])
        k = _dot(h, wa_ref[:, aw + g * A_WIDTH:aw + (g + 1) * A_WIDTH])
        v = _dot(h, wa_ref[:, 2 * aw + g * A_WIDTH:2 * aw + (g + 1) * A_WIDTH])
        q = head_norm(q, qgain_ref[:, sl]) * (A_HEAD_DIM ** -0.5)
        k = head_norm(k, kgain_ref[:, sl])
        if blocked:
            for i, (x, o_refs) in enumerate(((q, qb_refs), (k, kb_refs), (v, vb_refs))):
                _store_blocked(x, d, o_refs[g], perm_ref.at[(g - 1) * 3 + i] if d > 1 else None)
            keep = min(BAND * d, tm)
            row_refs[g][0] = k[tm - keep:, :].T
            row_refs[g][1] = v[tm - keep:, :].T
        else:
            ka_ref[:, sl] = k
            va_ref[:, sl] = v
            qa_ref[:, sl] = q.astype(BF16)

    qg_ref[...] = (_dot(h, wg_ref[:, 0:G_KW]) * (G_DK ** -0.5)).astype(BF16)
    kg_ref[...] = _dot(h, wg_ref[:, G_KW:2 * G_KW]).astype(BF16)
    for c in range(G_VW // N_CHUNK):
        sl = slice(c * N_CHUNK, (c + 1) * N_CHUNK)
        off = 2 * G_KW + c * N_CHUNK
        vg_ref[:, sl] = _dot(h, wg_ref[:, off:off + N_CHUNK]).astype(BF16)
        rg_ref[:, sl] = _dot(h, wg_ref[:, G_VW + off:G_VW + off + N_CHUNK]).astype(BF16)

    lr = _dot(h, wlr_ref[...]).astype(BF16)
    z = _dot(lr, up_ref[...]) + bias_ref[...]
    la_ref[...] = _log_sigmoid(z) * (1.0 / G_TAU)

    for c in range(2 * D_MODEL // N_CHUNK):
        sl = slice(c * N_CHUNK, (c + 1) * N_CHUNK)
        gate_ref[:, sl] = jax.nn.sigmoid(_dot(h, wgate_ref[:, sl])).astype(BF16)


def _unit_spec(d, tm):
    per_class = tm // d
    if per_class >= BAND:
        return pl.BlockSpec((tm // BAND, BAND, A_WIDTH), lambda i: (i, 0, 0))
    parts = BAND // per_class
    return pl.BlockSpec((d, per_class, A_WIDTH), lambda i: (i // parts, i % parts, 0))


def _row_spec(d, tm, tiles_per_seq):
    window = BAND * d
    if window <= tm:
        return pl.BlockSpec((None, 2, A_WIDTH, window), lambda i: (i // tiles_per_seq, 0, 0, 0))
    first_tile = tiles_per_seq - window // tm
    return pl.BlockSpec(
        (None, 2, A_WIDTH, tm),
        lambda i: (i // tiles_per_seq, 0, 0, jnp.maximum(i % tiles_per_seq - first_tile, 0)))


def _proj(x1, g, wa, wg, wlr, wgate, up, bias, qgain, kgain, hmean, tm, seq=None):
    t = x1.shape[0]
    aw = N_GROUPS * A_WIDTH
    blocked = seq is not None
    widths = [(G_KW, BF16), (G_KW, BF16), (G_VW, BF16), (G_VW, BF16), (G_KW, F32),
              (2 * D_MODEL, BF16)]
    scratch = []
    tiles_per_seq = None
    if blocked:
        assert tm == 4 * BAND and SWA_DILATIONS == (1, 4, 16)
        assert seq % tm == 0 and BAND * SWA_DILATIONS[-1] <= seq
        tiles_per_seq = seq // tm
        out_specs = ([_unit_spec(d, tm) for _ in range(3) for d in SWA_DILATIONS]
                     + [_row_spec(d, tm, tiles_per_seq) for d in SWA_DILATIONS])
        out_shape = ([jax.ShapeDtypeStruct((t // BAND, BAND, A_WIDTH), BF16)] * 9
                     + [jax.ShapeDtypeStruct((t // seq, 2, A_WIDTH, BAND * d), F32)
                        for d in SWA_DILATIONS])
        scratch = [pltpu.VMEM((3 * (N_GROUPS - 1), A_WIDTH // LANES, tm, LANES), F32)]
    else:
        widths = [(aw, BF16), (aw, F32), (aw, F32)] + widths
        out_specs, out_shape = [], []
    out_specs = out_specs + [pl.BlockSpec((tm, w), lambda i: (i, 0)) for w, _ in widths]
    out_shape = out_shape + [jax.ShapeDtypeStruct((t, w), dt) for w, dt in widths]
    consts = [g, wa, wg, wlr, wgate, up, bias, qgain, kgain, hmean]
    return pl.pallas_call(
        functools.partial(_proj_kernel, blocked=blocked, tiles_per_seq=tiles_per_seq),
        grid=(t // tm,),
        in_specs=[pl.BlockSpec((tm, D_MODEL), lambda i: (i, 0))]
        + [_resident(c.shape) for c in consts],
        out_specs=out_specs,
        out_shape=out_shape,
        scratch_shapes=scratch,
        compiler_params=_params(1),
        name="in_proj",
    )(x1, *consts)


def _log2(n):
    assert n > 0 and n & (n - 1) == 0, n
    return n.bit_length() - 1


def _head_masks(rows):
    lane = lax.broadcasted_iota(jnp.int32, (rows, A_WIDTH), 1)
    return [(lane >> _log2(A_HEAD_DIM)) == h for h in range(A_HEADS)]


ATTN_SPAN = 2048


def _attn_prompt_kernel(*refs, units):
    q_refs, k_refs, v_refs = refs[0:3], refs[3:6], refs[6:9]
    o_ref, osc_ref, lsc_ref, bias_ref = refs[9:]
    span = pl.program_id(1)
    n_slab = A_WIDTH // LANES

    row = lax.broadcasted_iota(jnp.int32, (A_HEADS * BAND, 2 * BAND), 0) & (BAND - 1)
    col = lax.broadcasted_iota(jnp.int32, (A_HEADS * BAND, 2 * BAND), 1)
    own = (col >= BAND) & (col - BAND <= row)
    bias_ref[0] = jnp.where(own, 0.0, -jnp.inf)
    bias_ref[1] = jnp.where(own | ((col < BAND) & (col >= row)), 0.0, -jnp.inf)

    for g, d in enumerate(SWA_DILATIONS):
        q_ref, k_ref, v_ref = q_refs[g], k_refs[g], v_refs[g]

        def scores(ul, d=d, q_ref=q_ref, k_ref=k_ref):
            u = span * units + ul
            up = jnp.maximum(u - d, 0)
            kb = jnp.concatenate([k_ref[up], k_ref[u]], axis=0)
            q = q_ref[ul]
            qm = jnp.concatenate(
                [q * m.astype(F32).astype(BF16) for m in _head_masks(BAND)], axis=0)
            has_prev = jnp.where(u >= d, 1, 0)
            return _dot_nt(qm, kb) + bias_ref[has_prev]

        def softmax(s):
            m = jnp.max(s, axis=-1, keepdims=True)
            p = jnp.exp(s - m)
            den = jnp.sum(p, axis=-1, keepdims=True)
            return p.astype(BF16), 1.0 / den, m + jnp.log(den)

        def values(ul, p, d=d, v_ref=v_ref):
            u = span * units + ul
            vb = jnp.concatenate([v_ref[jnp.maximum(u - d, 0)], v_ref[u]], axis=0)
            return _dot(p, vb)

        def store(ul, o_all, inv_den, lse_all, d=d, g=g):
            hmask = _head_masks(BAND)
            o_all = o_all * inv_den
            o_acc = jnp.zeros((BAND, A_WIDTH), F32)
            lse_acc = jnp.zeros((BAND, A_WIDTH), F32)
            for h in range(A_HEADS):
                rs = slice(h * BAND, (h + 1) * BAND)
                o_acc = jnp.where(hmask[h], o_all[rs], o_acc)
                lse_acc = jnp.where(hmask[h], lse_all[rs], lse_acc)
            start = (ul >> _log2(d)) * (BAND * d) + (ul & (d - 1))
            if d == 1:
                rows = pl.ds(pl.multiple_of(start, BAND), BAND)
            else:
                rows = pl.ds(start, BAND, stride=d)
            for sl in range(n_slab):
                osc_ref[g, sl, rows, :] = o_acc[:, sl * LANES:(sl + 1) * LANES]
                lsc_ref[g, sl, rows, :] = lse_acc[:, sl * LANES:(sl + 1) * LANES]

        def unit_pair(i, carry, scores=scores, softmax=softmax, values=values, store=store):
            uls = (2 * i, 2 * i + 1)
            ss = [scores(ul) for ul in uls]
            sm = [softmax(s) for s in ss]
            os_ = [values(ul, p) for ul, (p, _, _) in zip(uls, sm)]
            for ul, o_all, (_, inv_den, lse_all) in zip(uls, os_, sm):
                store(ul, o_all, inv_den, lse_all)
            return carry

        lax.fori_loop(0, units // 2, unit_pair, 0)

    def merge(i, carry):
        rows = pl.ds(pl.multiple_of(i * BAND, BAND), BAND)

        def natural(ref, g):
            return jnp.concatenate([ref[g, sl, rows, :] for sl in range(n_slab)], axis=1)

        outs = [natural(osc_ref, g) for g in range(N_GROUPS)]
        lses = [natural(lsc_ref, g) for g in range(N_GROUPS)]
        m = functools.reduce(jnp.maximum, lses)
        ws = [jnp.exp(l - m) for l in lses]
        a = functools.reduce(jnp.add, [w * o for w, o in zip(ws, outs)])
        o_ref[rows, :] = (a * (1.0 / functools.reduce(jnp.add, ws))).astype(BF16)
        return carry

    lax.fori_loop(0, ATTN_SPAN // BAND, merge, 0)


def _attn_prompt(qkv_units, batch, seq):
    assert seq % ATTN_SPAN == 0
    units = ATTN_SPAN // BAND
    per_seq = seq // BAND
    spans = seq // ATTN_SPAN
    q_spec = pl.BlockSpec((units, BAND, A_WIDTH), lambda b, s: (b * spans + s, 0, 0))
    kv_spec = pl.BlockSpec((per_seq, BAND, A_WIDTH), lambda b, s: (b, 0, 0))
    return pl.pallas_call(
        functools.partial(_attn_prompt_kernel, units=units),
        grid=(batch, spans),
        in_specs=[q_spec] * 3 + [kv_spec] * 6,
        out_specs=pl.BlockSpec((ATTN_SPAN, A_WIDTH), lambda b, s: (b * spans + s, 0)),
        out_shape=jax.ShapeDtypeStruct((batch * seq, A_WIDTH), BF16),
        scratch_shapes=[pltpu.VMEM((N_GROUPS, A_WIDTH // LANES, ATTN_SPAN, LANES), F32)] * 2
        + [pltpu.VMEM((2, A_HEADS * BAND, 2 * BAND), F32)],
        compiler_params=_params(2),
        name="attn_prompt",
    )(*qkv_units)


def _attn_sample_kernel(q_ref, k_ref, v_ref, c1_ref, c4_ref, c16_ref, o_ref, *, t_new, bb):
    hmask = _head_masks(t_new)
    rows = A_HEADS * t_new
    row_t = lax.broadcasted_iota(jnp.int32, (rows, LANES), 0) & (t_new - 1)
    col = lax.broadcasted_iota(jnp.int32, (rows, LANES), 1)
    pad = jnp.zeros((LANES - t_new, A_WIDTH), F32)
    caches = (c1_ref, c4_ref, c16_ref)
    jobs = [(i, g) for i in range(bb) for g in range(N_GROUPS)]

    ok_old, ok_new = [], []
    for d, c_ref in zip(SWA_DILATIONS, caches):
        window = c_ref.shape[-1]
        assert window == BAND * d, (window, d)
        pos_t = lax.broadcasted_iota(jnp.int32, (rows, window), 0) & (t_new - 1)
        pos = lax.broadcasted_iota(jnp.int32, (rows, window), 1)
        ok_old.append((pos >= pos_t) & (((pos - pos_t) & (d - 1)) == 0))
        ok_new.append((col <= row_t) & (((row_t - col) & (d - 1)) == 0))

    def new_rows(ref, i, g):
        sl = slice(g * A_WIDTH, (g + 1) * A_WIDTH)
        return jnp.concatenate([ref[i, :, sl], pad], axis=0).astype(BF16)

    def masked_q(i, g):
        q = q_ref[i, :, g * A_WIDTH:(g + 1) * A_WIDTH].astype(F32)
        return jnp.concatenate([q * m.astype(F32) for m in hmask], axis=0).astype(BF16)

    qms = [masked_q(i, g) for i, g in jobs]
    s_old = [jnp.where(ok_old[g], _dot(qm, caches[g][i, 0].astype(BF16)), -jnp.inf)
             for qm, (i, g) in zip(qms, jobs)]
    s_new = [jnp.where(ok_new[g], _dot_nt(qm, new_rows(k_ref, i, g)), -jnp.inf)
             for qm, (i, g) in zip(qms, jobs)]
    ms = [jnp.maximum(jnp.max(so, axis=-1, keepdims=True), jnp.max(sn, axis=-1, keepdims=True))
          for so, sn in zip(s_old, s_new)]
    p_old = [jnp.exp(so - m) for so, m in zip(s_old, ms)]
    p_new = [jnp.exp(sn - m) for sn, m in zip(s_new, ms)]
    dens = [jnp.sum(po, axis=-1, keepdims=True) + jnp.sum(pn, axis=-1, keepdims=True)
            for po, pn in zip(p_old, p_new)]
    accs = [_dot_nt(po.astype(BF16), caches[g][i, 1].astype(BF16))
            + _dot(pn.astype(BF16), new_rows(v_ref, i, g))
            for po, pn, (i, g) in zip(p_old, p_new, jobs)]

    for i in range(bb):
        num = jnp.zeros((t_new, A_WIDTH), F32)
        den_all = jnp.zeros((t_new, A_WIDTH), F32)
        m_all = jnp.full((t_new, A_WIDTH), -jnp.inf, F32)
        for g in range(N_GROUPS):
            j = jobs.index((i, g))
            o_g = jnp.zeros((t_new, A_WIDTH), F32)
            lse_g = jnp.zeros((t_new, A_WIDTH), F32)
            for h in range(A_HEADS):
                rs = slice(h * t_new, (h + 1) * t_new)
                o_g = jnp.where(hmask[h], accs[j][rs] * (1.0 / dens[j][rs]), o_g)
                lse_g = jnp.where(hmask[h], ms[j][rs] + jnp.log(dens[j][rs]), lse_g)
            m_next = jnp.maximum(m_all, lse_g)
            scale_old = jnp.exp(m_all - m_next)
            w = jnp.exp(lse_g - m_next)
            num = num * scale_old + w * o_g
            den_all = den_all * scale_old + w
            m_all = m_next
        o_ref[i] = (num * (1.0 / den_all)).astype(BF16)


def _attn_sample(qa, ka, va, c128, c512, c2048, batch, t_new, bb):
    w3 = N_GROUPS * A_WIDTH
    tok_spec = pl.BlockSpec((bb, t_new, w3), lambda b: (b, 0, 0))
    cache_views = [jnp.transpose(c, (0, 2, 3, 4, 1)).reshape(batch, 2, A_WIDTH, c.shape[1])
                   for c in (c128, c512, c2048)]
    cache_specs = [pl.BlockSpec((bb, 2, A_WIDTH, c.shape[-1]), lambda b: (b, 0, 0, 0))
                   for c in cache_views]
    out = pl.pallas_call(
        functools.partial(_attn_sample_kernel, t_new=t_new, bb=bb),
        grid=(batch // bb,),
        in_specs=[tok_spec, tok_spec, tok_spec] + cache_specs,
        out_specs=pl.BlockSpec((bb, t_new, A_WIDTH), lambda b: (b, 0, 0)),
        out_shape=jax.ShapeDtypeStruct((batch, t_new, A_WIDTH), BF16),
        compiler_params=_params(1),
        name="attn_sample",
    )(qa.reshape(batch, t_new, w3), ka.reshape(batch, t_new, w3), va.reshape(batch, t_new, w3),
      *cache_views)
    return out.reshape(batch * t_new, A_WIDTH)


def _split2(x):
    hi = x.astype(BF16)
    return hi, (x - hi.astype(F32)).astype(BF16)


def _gla_out(o, rg, gain):
    on = o * lax.rsqrt(jnp.mean(o * o, axis=-1, keepdims=True) + EPS) * gain
    return (on * rg * jax.nn.sigmoid(rg)).astype(BF16)


def _col_bcast(vec_row, width):
    sq = jnp.broadcast_to(vec_row, (LANES, LANES)).T
    return jnp.concatenate([sq] * (width // LANES), axis=1)


GLA_CHUNK = 1024
GLA_SEQS = 2


def _gla_prompt_kernel(q_ref, k_ref, v_ref, la_ref, rg_ref, gn_ref, o_ref, s_ref, *, n_blocks):
    cb, sub = GLA_BLOCK, GLA_SUB
    n_sub = cb // sub
    n_seq = q_ref.shape[0]

    @pl.when(pl.program_id(1) == 0)
    def _():
        s_ref[...] = jnp.zeros_like(s_ref)

    jobs = [(n, h) for n in range(n_seq) for h in range(G_HEADS)]

    def hk(x, h):
        return x[:, h * G_DK:(h + 1) * G_DK]

    def decays(n, rows, tri, rowk):
        hi, lo = _split2(la_ref[n, rows, :])
        b = _dot(tri, hi) + _dot(tri, lo)
        q = q_ref[n, rows, :].astype(F32)
        k = k_ref[n, rows, :].astype(F32)
        b_last = b[cb - 1:cb, :]
        qe = (q * jnp.exp(b)).astype(BF16)
        sub_end = [b[(j + 1) * sub - 1:(j + 1) * sub, :] for j in range(n_sub)]
        own_end = jnp.concatenate(
            [jnp.broadcast_to(e, (sub, G_KW)) for e in sub_end], axis=0)
        kk = k * jnp.exp(own_end - b)
        q_parts, k_parts = [], []
        for j in range(n_sub):
            lo_row = j * sub
            qj = q[lo_row:, :] * jnp.exp(b[lo_row:, :] - sub_end[j])
            q_parts.append(jnp.concatenate(
                [jnp.zeros((lo_row, G_KW), F32), qj], axis=0).astype(BF16) if lo_row else
                qj.astype(BF16))
            in_sub = (rowk >= lo_row) & (rowk < lo_row + sub)
            k_parts.append(jnp.where(in_sub, kk, 0.0).astype(BF16))
        return qe, q_parts, k_parts, k * jnp.exp(b_last - b), jnp.exp(b_last)

    def step(i, carry):
        rows = pl.ds(pl.multiple_of(i * cb, cb), cb)
        row = lax.broadcasted_iota(jnp.int32, (cb, cb), 0)
        col = lax.broadcasted_iota(jnp.int32, (cb, cb), 1)
        tri = (row >= col).astype(F32).astype(BF16)
        rowk = lax.broadcasted_iota(jnp.int32, (cb, G_KW), 0)

        per_seq = [decays(n, rows, tri, rowk) for n in range(n_seq)]
        states = [s_ref[n, h] for n, h in jobs]
        vals = [v_ref[n, rows, h * G_DV:(h + 1) * G_DV] for n, h in jobs]
        inter = [_dot(hk(per_seq[n][0], h), s.astype(BF16)) for s, (n, h) in zip(states, jobs)]
        atts = [_dot_nt(jnp.concatenate([hk(p, h) for p in per_seq[n][1]], axis=1),
                        jnp.concatenate([hk(p, h) for p in per_seq[n][2]], axis=1))
                for n, h in jobs]
        atts = [jnp.where(col <= row, a, 0.0).astype(BF16) for a in atts]
        outs = [x + _dot(a, v) for x, a, v in zip(inter, atts, vals)]
        for s, v, (n, h) in zip(states, vals, jobs):
            kd_t = hk(per_seq[n][3], h).T.astype(BF16)
            s_ref[n, h] = s * _col_bcast(hk(per_seq[n][4], h), G_DV) + _dot(kd_t, v)
        for o, (n, h) in zip(outs, jobs):
            vs = slice(h * G_DV, (h + 1) * G_DV)
            o_ref[n, rows, vs] = _gla_out(o, rg_ref[n, rows, vs].astype(F32), gn_ref[...])
        return carry

    lax.fori_loop(0, n_blocks, step, 0)


def _gla_prompt(qg, kg, vg, la, rg, gn, batch, seq):
    assert seq % GLA_CHUNK == 0 and batch % GLA_SEQS == 0
    chunks = seq // GLA_CHUNK

    def spec(width):
        return pl.BlockSpec((GLA_SEQS, GLA_CHUNK, width), lambda b, c: (b, c, 0))

    def per_seq(x):
        return x.reshape(batch, seq, x.shape[-1])
    o, s = pl.pallas_call(
        functools.partial(_gla_prompt_kernel, n_blocks=GLA_CHUNK // GLA_BLOCK),
        grid=(batch // GLA_SEQS, chunks),
        in_specs=[spec(G_KW), spec(G_KW), spec(G_VW), spec(G_KW), spec(G_VW),
                  _resident((1, G_DV))],
        out_specs=[spec(G_VW),
                   pl.BlockSpec((GLA_SEQS, G_HEADS, G_DK, G_DV), lambda b, c: (b, 0, 0, 0))],
        out_shape=[jax.ShapeDtypeStruct((batch, seq, G_VW), BF16),
                   jax.ShapeDtypeStruct((batch, G_HEADS, G_DK, G_DV), F32)],
        compiler_params=_params(2),
        name="gla_prompt",
    )(per_seq(qg), per_seq(kg), per_seq(vg), per_seq(la), per_seq(rg), gn)
    return o.reshape(batch * seq, G_VW), s


def _gla_sample_kernel(q_ref, k_ref, v_ref, la_ref, rg_ref, gn_ref, s0_ref, o_ref, s_ref,
                       *, bb, t_new):
    per_body = 2

    def hk(x, h):
        return x[:, h * G_DK:(h + 1) * G_DK]

    def hv(x, h):
        return x[:, h * G_DV:(h + 1) * G_DV]

    def body(i0, carry):
        trow = lax.broadcasted_iota(jnp.int32, (t_new, G_KW), 0)
        arow = lax.broadcasted_iota(jnp.int32, (t_new, LANES), 0)
        acol = lax.broadcasted_iota(jnp.int32, (t_new, LANES), 1)
        zk = jnp.zeros((LANES - t_new, G_KW), F32)
        zv = jnp.zeros((LANES - t_new, G_VW), F32)
        seqs = [per_body * i0 + j for j in range(per_body)]
        jobs = [(n, h) for n in range(per_body) for h in range(G_HEADS)]

        bs = []
        for i in seqs:
            b = la_ref[i]
            shift = 1
            while shift < t_new:
                b = b + jnp.where(trow >= shift, pltpu.roll(b, shift, 0), 0.0)
                shift *= 2
            bs.append(b)
        qs = [q_ref[i].astype(F32) for i in seqs]
        ks = [k_ref[i].astype(F32) for i in seqs]
        vs = [jnp.concatenate([v_ref[i].astype(F32), zv], axis=0).astype(BF16) for i in seqs]
        b_last = [b[t_new - 1:t_new, :] for b in bs]
        qe = [(q * jnp.exp(b)).astype(BF16) for q, b in zip(qs, bs)]
        ke = [jnp.concatenate([k * jnp.exp(-b), zk], axis=0).astype(BF16) for k, b in zip(ks, bs)]
        kd = [jnp.concatenate([k * jnp.exp(bl - b), zk], axis=0)
              for k, b, bl in zip(ks, bs, b_last)]
        decay = [jnp.exp(bl) for bl in b_last]

        s0 = [s0_ref[seqs[n], h] for n, h in jobs]
        att = [jnp.where(acol <= arow, _dot_nt(hk(qe[n], h), hk(ke[n], h)), 0.0).astype(BF16)
               for n, h in jobs]
        out = [_dot(a, hv(vs[n], h)) + _dot(hk(qe[n], h), s.astype(BF16))
               for a, s, (n, h) in zip(att, s0, jobs)]
        for s, (n, h) in zip(s0, jobs):
            s_ref[seqs[n], h] = (s * _col_bcast(hk(decay[n], h), G_DV)
                                 + _dot(hk(kd[n], h).T.astype(BF16), hv(vs[n], h)))
        for o, (n, h) in zip(out, jobs):
            sl = slice(h * G_DV, (h + 1) * G_DV)
            o_ref[seqs[n], :, sl] = _gla_out(o, rg_ref[seqs[n], :, sl].astype(F32), gn_ref[...])
        return carry

    assert bb % per_body == 0
    lax.fori_loop(0, bb // per_body, body, 0)


def _gla_sample(qg, kg, vg, la, rg, gn, s0, batch, t_new, bb):
    def spec(width):
        return pl.BlockSpec((bb, t_new, width), lambda b: (b, 0, 0))
    st_spec = pl.BlockSpec((bb, G_HEADS, G_DK, G_DV), lambda b: (b, 0, 0, 0))
    o, s = pl.pallas_call(
        functools.partial(_gla_sample_kernel, bb=bb, t_new=t_new),
        grid=(batch // bb,),
        in_specs=[spec(G_KW), spec(G_KW), spec(G_VW), spec(G_KW), spec(G_VW),
                  _resident((1, G_DV)), st_spec],
        out_specs=[spec(G_VW), st_spec],
        out_shape=[jax.ShapeDtypeStruct((batch, t_new, G_VW), BF16),
                   jax.ShapeDtypeStruct((batch, G_HEADS, G_DK, G_DV), F32)],
        compiler_params=_params(1),
        name="gla_sample",
    )(qg.reshape(batch, t_new, G_KW), kg.reshape(batch, t_new, G_KW),
      vg.reshape(batch, t_new, G_VW), la.reshape(batch, t_new, G_KW),
      rg.reshape(batch, t_new, G_VW), gn, s0)
    return o.reshape(batch * t_new, G_VW), s


def _out_kernel(x_ref, a_ref, gb_ref, gate_ref, wa_ref, wb_ref, wo_ref, g_ref, wg_ref, wu_ref,
                wd_ref, o_ref, act_ref):
    a_out = _dot(a_ref[...], wa_ref[...])
    b_out = _dot(gb_ref[...], wb_ref[...])
    mix = (gate_ref[:, 0:D_MODEL].astype(F32) * a_out
           + gate_ref[:, D_MODEL:2 * D_MODEL].astype(F32) * b_out)
    x2 = x_ref[...] + _dot(mix.astype(BF16), wo_ref[...])
    o_ref[...] = _swiglu_residual(x2, g_ref, wg_ref, wu_ref, wd_ref, act_ref)


def _out(x1, attn, gb, gates, wa, wb, wo, g, wg, wu, wd, tm):
    t = x1.shape[0]

    def tok(width):
        return pl.BlockSpec((tm, width), lambda i: (i, 0))
    consts = [wa, wb, wo, g, wg, wu, wd]
    return pl.pallas_call(
        _out_kernel,
        grid=(t // tm,),
        in_specs=[tok(D_MODEL), tok(A_WIDTH), tok(G_VW), tok(2 * D_MODEL)]
        + [_resident(c.shape) for c in consts],
        out_specs=tok(D_MODEL),
        out_shape=jax.ShapeDtypeStruct((t, D_MODEL), F32),
        scratch_shapes=[pltpu.VMEM((tm, D_FF), BF16)],
        compiler_params=_params(1),
        name="out_ffn2",
    )(x1, attn, gb, gates, *consts)


def _sample_rows_kernel(k_ref, v_ref, *refs, batch, t_new):
    o_refs, slab_ref = refs[:N_GROUPS], refs[N_GROUPS]
    n_slab = N_GROUPS * A_WIDTH // LANES
    per_group = A_WIDTH // LANES
    for kv, src in enumerate((k_ref, v_ref)):
        for s in range(n_slab):
            slab_ref[kv, s] = src[:, s * LANES:(s + 1) * LANES]
    for kv in range(2):
        for s in range(n_slab):
            g, part = divmod(s, per_group)
            for t in range(t_new):
                rows = slab_ref[kv, s, pl.ds(t, batch, stride=t_new), :]
                o_refs[g][t, kv, part * LANES:(part + 1) * LANES, :] = rows.T


def _sample_rows(ka, va, batch, t_new):
    assert batch == LANES
    n_slab = N_GROUPS * A_WIDTH // LANES
    outs = pl.pallas_call(
        functools.partial(_sample_rows_kernel, batch=batch, t_new=t_new),
        out_shape=[jax.ShapeDtypeStruct((t_new, 2, A_WIDTH, batch), F32)] * N_GROUPS,
        scratch_shapes=[pltpu.VMEM((2, n_slab, batch * t_new, LANES), F32)],
        compiler_params=pltpu.CompilerParams(vmem_limit_bytes=VMEM_LIMIT),
        name="sample_rows",
    )(ka, va)
    return [jnp.transpose(o.reshape(t_new, 2, A_HEADS, A_HEAD_DIM, batch), (4, 0, 1, 2, 3))
            for o in outs]


def _position_major(rows, batch):
    window = rows.shape[-1]
    return jnp.transpose(rows.reshape(batch, 2, A_HEADS, A_HEAD_DIM, window), (0, 4, 1, 2, 3))


def kernel(x_prompt, x_sample, cache_swa128_kv, cache_swa512_kv, cache_swa2048_kv, state_gla,
           norm_ffn1, ffn1_gate, ffn1_up, ffn1_down, norm_mix, w_in, a_q_norm, a_k_norm,
           g_alpha_up, g_alpha_bias, g_out_norm, w_a_out, w_b_out, w_out,
           norm_ffn2, ffn2_gate, ffn2_up, ffn2_down):
    batch, seq, _ = x_prompt.shape
    dec_batch, dec_seq, _ = x_sample.shape

    aw = N_GROUPS * A_WIDTH
    o_g = 3 * aw
    o_lr = o_g + 2 * G_KW + 2 * G_VW
    o_gate = o_lr + G_LOWRANK
    w_attn = w_in[:, :o_g].astype(BF16)
    w_gla = w_in[:, o_g:o_lr].astype(BF16)
    w_lr = jnp.pad(w_in[:, o_lr:o_gate], ((0, 0), (0, LANES - G_LOWRANK))).astype(BF16)
    w_gate = w_in[:, o_gate:].astype(BF16)
    up = jnp.pad(g_alpha_up, ((0, LANES - G_LOWRANK), (0, 0))).astype(BF16)
    bias = g_alpha_bias.reshape(1, G_KW)
    qgain = jnp.tile(a_q_norm, (1, A_HEADS)).reshape(1, aw)
    kgain = jnp.tile(a_k_norm, (1, A_HEADS)).reshape(1, aw)
    lane = jnp.arange(A_WIDTH) // A_HEAD_DIM
    hmean = jnp.where(lane[:, None] == lane[None, :], 1.0 / A_HEAD_DIM, 0.0).astype(BF16)
    gn = g_out_norm.reshape(1, G_DV)
    ffn1 = (norm_ffn1.reshape(1, D_MODEL), ffn1_gate.astype(BF16), ffn1_up.astype(BF16),
            ffn1_down.astype(BF16))
    ffn2 = (norm_ffn2.reshape(1, D_MODEL), ffn2_gate.astype(BF16), ffn2_up.astype(BF16),
            ffn2_down.astype(BF16))
    proj_w = (norm_mix.reshape(1, D_MODEL), w_attn, w_gla, w_lr, w_gate, up, bias,
              qgain, kgain, hmean)
    out_w = (w_a_out.astype(BF16), w_b_out.astype(BF16), w_out.astype(BF16))

    x1p = _ffn(x_prompt.reshape(batch * seq, D_MODEL), *ffn1, tm=512)
    proj_p = _proj(x1p, *proj_w, tm=512, seq=seq)
    qkv_units, rows_p, (qg, kg, vg, rg, la, gates) = proj_p[:9], proj_p[9:12], proj_p[12:]
    a_p = _attn_prompt(qkv_units, batch, seq)
    gbp, state_p = _gla_prompt(qg, kg, vg, la, rg, gn, batch, seq)
    y_prompt = _out(x1p, a_p, gbp, gates, *out_w, *ffn2, tm=512).reshape(batch, seq, D_MODEL)
    rows_p = [_position_major(r, batch) for r in rows_p]

    x1s = _ffn(x_sample.reshape(dec_batch * dec_seq, D_MODEL), *ffn1, tm=512)
    qa, ka, va, qg, kg, vg, rg, la, gates = _proj(x1s, *proj_w, tm=512)
    a_s = _attn_sample(qa, ka, va, cache_swa128_kv, cache_swa512_kv, cache_swa2048_kv,
                       dec_batch, dec_seq, bb=2)
    gbs, state_s = _gla_sample(qg, kg, vg, la, rg, gn, state_gla, dec_batch, dec_seq, bb=8)
    y_sample = _out(x1s, a_s, gbs, gates, *out_w, *ffn2, tm=512).reshape(
        dec_batch, dec_seq, D_MODEL)
    rows_s = _sample_rows(ka, va, dec_batch, dec_seq)

    return (y_prompt, y_sample, rows_p[0], rows_p[1], rows_p[2], state_p,
            rows_s[0], rows_s[1], rows_s[2], state_s)
```

```python
import functools

import jax
import jax.numpy as jnp
from jax import lax
from jax.experimental import pallas as pl
from jax.experimental.pallas import tpu as pltpu

F32 = jnp.float32
BF16 = jnp.bfloat16

D_MODEL = 1024
D_FF = 2816
SWA_DILATIONS = (1, 4, 16)
N_GROUPS = 3
A_HEADS = 4
A_HEAD_DIM = 64
A_WIDTH = A_HEADS * A_HEAD_DIM
BAND = 128
G_HEADS = 4
G_DK = 128
G_DV = 256
G_KW = G_HEADS * G_DK
G_VW = G_HEADS * G_DV
G_LOWRANK = 16
G_TAU = 16.0
EPS = 1e-6

LANES = 128
GLA_BLOCK = 128
GLA_SUB = 32
FF_CHUNK = 256
N_CHUNK = 512
VMEM_LIMIT = 58 * 1024 * 1024


def _resident(shape):
    nd = len(shape)
    return pl.BlockSpec(shape, lambda *_: (0,) * nd, pipeline_mode=pl.Buffered(1))


def _params(n_axes):
    return pltpu.CompilerParams(
        dimension_semantics=("arbitrary",) * n_axes, vmem_limit_bytes=VMEM_LIMIT)


def _dot(a, b):
    return jnp.dot(a, b, preferred_element_type=F32)


def _dot_nt(a, b):
    return lax.dot_general(a, b, (((1,), (1,)), ((), ())), preferred_element_type=F32)


def _rmsnorm(x, g):
    return x * lax.rsqrt(jnp.mean(x * x, axis=-1, keepdims=True) + EPS) * g


def _swiglu_residual(x, g_ref, wg_ref, wu_ref, wd_ref, act_ref, between=None):
    h = _rmsnorm(x, g_ref[...]).astype(BF16)
    for c in range(D_FF // FF_CHUNK):
        sl = slice(c * FF_CHUNK, (c + 1) * FF_CHUNK)
        gate = _dot(h, wg_ref[:, sl])
        up = _dot(h, wu_ref[:, sl])
        act_ref[:, sl] = (gate * jax.nn.sigmoid(gate) * up).astype(BF16)
        if between is not None:
            between(c)
    return x + 0.5 * _dot(act_ref[...], wd_ref[...])


def _ffn_kernel(x_ref, g_ref, wg_ref, wu_ref, wd_ref, o_ref, act_ref):
    o_ref[...] = _swiglu_residual(x_ref[...], g_ref, wg_ref, wu_ref, wd_ref, act_ref)


def _ffn(x, g, wg, wu, wd, tm):
    t = x.shape[0]
    return pl.pallas_call(
        _ffn_kernel,
        grid=(t // tm,),
        in_specs=[
            pl.BlockSpec((tm, D_MODEL), lambda i: (i, 0)),
            _resident((1, D_MODEL)),
            _resident((D_MODEL, D_FF)),
            _resident((D_MODEL, D_FF)),
            _resident((D_FF, D_MODEL)),
        ],
        out_specs=pl.BlockSpec((tm, D_MODEL), lambda i: (i, 0)),
        out_shape=jax.ShapeDtypeStruct((t, D_MODEL), F32),
        scratch_shapes=[pltpu.VMEM((tm, D_FF), BF16)],
        compiler_params=_params(1),
        name="ffn1",
    )(x, g, wg, wu, wd)


def _log_sigmoid(z):
    return jnp.minimum(z, 0.0) - jnp.log1p(jnp.exp(-jnp.abs(z)))


def _store_blocked(x, d, out_ref, perm_ref):
    tm = x.shape[0]
    if d == 1:
        for j in range(tm // BAND):
            out_ref[j] = x[j * BAND:(j + 1) * BAND].astype(BF16)
        return
    for s in range(A_WIDTH // LANES):
        perm_ref[s] = x[:, s * LANES:(s + 1) * LANES]
    for r in range(d):
        rows = jnp.concatenate(
            [perm_ref[s, pl.ds(r, tm // d, stride=d), :] for s in range(A_WIDTH // LANES)], axis=1)
        out_ref[r] = rows.astype(BF16)


def _proj_kernel(*refs, blocked, tiles_per_seq):
    (x_ref, g_ref, wa_ref, wg_ref, wlr_ref, wgate_ref, up_ref, bias_ref,
     qgain_ref, kgain_ref, hmean_ref) = refs[:11]
    outs = refs[11:]
    if blocked:
        qb_refs, kb_refs, vb_refs, row_refs = outs[0:3], outs[3:6], outs[6:9], outs[9:12]
        qg_ref, kg_ref, vg_ref, rg_ref, la_ref, gate_ref, perm_ref = outs[12:]
        tm = x_ref.shape[0]
    else:
        qa_ref, ka_ref, va_ref, qg_ref, kg_ref, vg_ref, rg_ref, la_ref, gate_ref = outs

    h = _rmsnorm(x_ref[...], g_ref[...]).astype(BF16)
    hmean = hmean_ref[...]

    def head_norm(x, gain):
        ms = _dot((x * x).astype(BF16), hmean)
        return x * lax.rsqrt(ms + EPS) * gain

    aw = N_GROUPS * A_WIDTH
    for g, d in enumerate(SWA_DILATIONS):
        sl = slice(g * A_WIDTH, (g + 1) * A_WIDTH)
        q = _dot(h, wa_ref[:, g * A_WIDTH:(g + 1) * A_WIDTH])
        k = _dot(h, wa_ref[:, aw + g * A_WIDTH:aw + (g + 1) * A_WIDTH])
        v = _dot(h, wa_ref[:, 2 * aw + g * A_WIDTH:2 * aw + (g + 1) * A_WIDTH])
        q = head_norm(q, qgain_ref[:, sl]) * (A_HEAD_DIM ** -0.5)
        k = head_norm(k, kgain_ref[:, sl])
        if blocked:
            for i, (x, o_refs) in enumerate(((q, qb_refs), (k, kb_refs), (v, vb_refs))):
                _store_blocked(x, d, o_refs[g], perm_ref.at[(g - 1) * 3 + i] if d > 1 else None)
            keep = min(BAND * d, tm)
            row_refs[g][0] = k[tm - keep:, :].T
            row_refs[g][1] = v[tm - keep:, :].T
        else:
            ka_ref[:, sl] = k
            va_ref[:, sl] = v
            qa_ref[:, sl] = q.astype(BF16)

    qg_ref[...] = (_dot(h, wg_ref[:, 0:G_KW]) * (G_DK ** -0.5)).astype(BF16)
    kg_ref[...] = _dot(h, wg_ref[:, G_KW:2 * G_KW]).astype(BF16)
    for c in range(G_VW // N_CHUNK):
        sl = slice(c * N_CHUNK, (c + 1) * N_CHUNK)
        off = 2 * G_KW + c * N_CHUNK
        vg_ref[:, sl] = _dot(h, wg_ref[:, off:off + N_CHUNK]).astype(BF16)
        rg_ref[:, sl] = _dot(h, wg_ref[:, G_VW + off:G_VW + off + N_CHUNK]).astype(BF16)

    lr = _dot(h, wlr_ref[...]).astype(BF16)
    z = _dot(lr, up_ref[...]) + bias_ref[...]
    la_ref[...] = _log_sigmoid(z) * (1.0 / G_TAU)

    for c in range(2 * D_MODEL // N_CHUNK):
        sl = slice(c * N_CHUNK, (c + 1) * N_CHUNK)
        gate_ref[:, sl] = jax.nn.sigmoid(_dot(h, wgate_ref[:, sl])).astype(BF16)


def _unit_spec(d, tm):
    per_class = tm // d
    if per_class >= BAND:
        return pl.BlockSpec((tm // BAND, BAND, A_WIDTH), lambda i: (i, 0, 0))
    parts = BAND // per_class
    return pl.BlockSpec((d, per_class, A_WIDTH), lambda i: (i // parts, i % parts, 0))


def _row_spec(d, tm, tiles_per_seq):
    window = BAND * d
    if window <= tm:
        return pl.BlockSpec((None, 2, A_WIDTH, window), lambda i: (i // tiles_per_seq, 0, 0, 0))
    first_tile = tiles_per_seq - window // tm
    return pl.BlockSpec(
        (None, 2, A_WIDTH, tm),
        lambda i: (i // tiles_per_seq, 0, 0, jnp.maximum(i % tiles_per_seq - first_tile, 0)))


def _proj(x1, g, wa, wg, wlr, wgate, up, bias, qgain, kgain, hmean, tm, seq=None):
    t = x1.shape[0]
    aw = N_GROUPS * A_WIDTH
    blocked = seq is not None
    widths = [(G_KW, BF16), (G_KW, BF16), (G_VW, BF16), (G_VW, BF16), (G_KW, F32),
              (2 * D_MODEL, BF16)]
    scratch = []
    tiles_per_seq = None
    if blocked:
        assert tm == 4 * BAND and SWA_DILATIONS == (1, 4, 16)
        assert seq % tm == 0 and BAND * SWA_DILATIONS[-1] <= seq
        tiles_per_seq = seq // tm
        out_specs = ([_unit_spec(d, tm) for _ in range(3) for d in SWA_DILATIONS]
                     + [_row_spec(d, tm, tiles_per_seq) for d in SWA_DILATIONS])
        out_shape = ([jax.ShapeDtypeStruct((t // BAND, BAND, A_WIDTH), BF16)] * 9
                     + [jax.ShapeDtypeStruct((t // seq, 2, A_WIDTH, BAND * d), F32)
                        for d in SWA_DILATIONS])
        scratch = [pltpu.VMEM((3 * (N_GROUPS - 1), A_WIDTH // LANES, tm, LANES), F32)]
    else:
        widths = [(aw, BF16), (aw, F32), (aw, F32)] + widths
        out_specs, out_shape = [], []
    out_specs = out_specs + [pl.BlockSpec((tm, w), lambda i: (i, 0)) for w, _ in widths]
    out_shape = out_shape + [jax.ShapeDtypeStruct((t, w), dt) for w, dt in widths]
    consts = [g, wa, wg, wlr, wgate, up, bias, qgain, kgain, hmean]
    return pl.pallas_call(
        functools.partial(_proj_kernel, blocked=blocked, tiles_per_seq=tiles_per_seq),
        grid=(t // tm,),
        in_specs=[pl.BlockSpec((tm, D_MODEL), lambda i: (i, 0))]
        + [_resident(c.shape) for c in consts],
        out_specs=out_specs,
        out_shape=out_shape,
        scratch_shapes=scratch,
        compiler_params=_params(1),
        name="in_proj",
    )(x1, *consts)


def _log2(n):
    assert n > 0 and n & (n - 1) == 0, n
    return n.bit_length() - 1


def _head_masks(rows):
    lane = lax.broadcasted_iota(jnp.int32, (rows, A_WIDTH), 1)
    return [(lane >> _log2(A_HEAD_DIM)) == h for h in range(A_HEADS)]


ATTN_SPAN = 2048


def _attn_prompt_kernel(*refs, units):
    q_refs, k_refs, v_refs = refs[0:3], refs[3:6], refs[6:9]
    o_ref, osc_ref, lsc_ref, bias_ref = refs[9:]
    span = pl.program_id(1)
    n_slab = A_WIDTH // LANES

    row = lax.broadcasted_iota(jnp.int32, (A_HEADS * BAND, 2 * BAND), 0) & (BAND - 1)
    col = lax.broadcasted_iota(jnp.int32, (A_HEADS * BAND, 2 * BAND), 1)
    own = (col >= BAND) & (col - BAND <= row)
    bias_ref[0] = jnp.where(own, 0.0, -jnp.inf)
    bias_ref[1] = jnp.where(own | ((col < BAND) & (col >= row)), 0.0, -jnp.inf)

    for g, d in enumerate(SWA_DILATIONS):
        q_ref, k_ref, v_ref = q_refs[g], k_refs[g], v_refs[g]

        def scores(ul, d=d, q_ref=q_ref, k_ref=k_ref):
            u = span * units + ul
            up = jnp.maximum(u - d, 0)
            kb = jnp.concatenate([k_ref[up], k_ref[u]], axis=0)
            q = q_ref[ul]
            qm = jnp.concatenate(
                [q * m.astype(F32).astype(BF16) for m in _head_masks(BAND)], axis=0)
            has_prev = jnp.where(u >= d, 1, 0)
            return _dot_nt(qm, kb) + bias_ref[has_prev]

        def softmax(s):
            m = jnp.max(s, axis=-1, keepdims=True)
            p = jnp.exp(s - m)
            den = jnp.sum(p, axis=-1, keepdims=True)
            return p.astype(BF16), 1.0 / den, m + jnp.log(den)

        def values(ul, p, d=d, v_ref=v_ref):
            u = span * units + ul
            vb = jnp.concatenate([v_ref[jnp.maximum(u - d, 0)], v_ref[u]], axis=0)
            return _dot(p, vb)

        def store(ul, o_all, inv_den, lse_all, d=d, g=g):
            hmask = _head_masks(BAND)
            o_all = o_all * inv_den
            o_acc = jnp.zeros((BAND, A_WIDTH), F32)
            lse_acc = jnp.zeros((BAND, A_WIDTH), F32)
            for h in range(A_HEADS):
                rs = slice(h * BAND, (h + 1) * BAND)
                o_acc = jnp.where(hmask[h], o_all[rs], o_acc)
                lse_acc = jnp.where(hmask[h], lse_all[rs], lse_acc)
            start = (ul >> _log2(d)) * (BAND * d) + (ul & (d - 1))
            if d == 1:
                rows = pl.ds(pl.multiple_of(start, BAND), BAND)
            else:
                rows = pl.ds(start, BAND, stride=d)
            for sl in range(n_slab):
                osc_ref[g, sl, rows, :] = o_acc[:, sl * LANES:(sl + 1) * LANES]
                lsc_ref[g, sl, rows, :] = lse_acc[:, sl * LANES:(sl + 1) * LANES]

        def unit_pair(i, carry, scores=scores, softmax=softmax, values=values, store=store):
            uls = (2 * i, 2 * i + 1)
            ss = [scores(ul) for ul in uls]
            sm = [softmax(s) for s in ss]
            os_ = [values(ul, p) for ul, (p, _, _) in zip(uls, sm)]
            for ul, o_all, (_, inv_den, lse_all) in zip(uls, os_, sm):
                store(ul, o_all, inv_den, lse_all)
            return carry

        lax.fori_loop(0, units // 2, unit_pair, 0)

    def merge(i, carry):
        rows = pl.ds(pl.multiple_of(i * BAND, BAND), BAND)

        def natural(ref, g):
            return jnp.concatenate([ref[g, sl, rows, :] for sl in range(n_slab)], axis=1)

        outs = [natural(osc_ref, g) for g in range(N_GROUPS)]
        lses = [natural(lsc_ref, g) for g in range(N_GROUPS)]
        m = functools.reduce(jnp.maximum, lses)
        ws = [jnp.exp(l - m) for l in lses]
        a = functools.reduce(jnp.add, [w * o for w, o in zip(ws, outs)])
        o_ref[rows, :] = (a * (1.0 / functools.reduce(jnp.add, ws))).astype(BF16)
        return carry

    lax.fori_loop(0, ATTN_SPAN // BAND, merge, 0)


def _attn_prompt(qkv_units, batch, seq):
    assert seq % ATTN_SPAN == 0
    units = ATTN_SPAN // BAND
    per_seq = seq // BAND
    spans = seq // ATTN_SPAN
    q_spec = pl.BlockSpec((units, BAND, A_WIDTH), lambda b, s: (b * spans + s, 0, 0))
    kv_spec = pl.BlockSpec((per_seq, BAND, A_WIDTH), lambda b, s: (b, 0, 0))
    return pl.pallas_call(
        functools.partial(_attn_prompt_kernel, units=units),
        grid=(batch, spans),
        in_specs=[q_spec] * 3 + [kv_spec] * 6,
        out_specs=pl.BlockSpec((ATTN_SPAN, A_WIDTH), lambda b, s: (b * spans + s, 0)),
        out_shape=jax.ShapeDtypeStruct((batch * seq, A_WIDTH), BF16),
        scratch_shapes=[pltpu.VMEM((N_GROUPS, A_WIDTH // LANES, ATTN_SPAN, LANES), F32)] * 2
        + [pltpu.VMEM((2, A_HEADS * BAND, 2 * BAND), F32)],
        compiler_params=_params(2),
        name="attn_prompt",
    )(*qkv_units)


def _attn_sample_kernel(*refs, t_new, bb):
    for _ in _attn_sample_stages(*refs, t_new=t_new, bb=bb):
        pass


def _attn_sample_stages(q_ref, k_ref, v_ref, c1_ref, c4_ref, c16_ref, o_ref, *, t_new, bb):
    hmask = _head_masks(t_new)
    rows = A_HEADS * t_new
    row_t = lax.broadcasted_iota(jnp.int32, (rows, LANES), 0) & (t_new - 1)
    col = lax.broadcasted_iota(jnp.int32, (rows, LANES), 1)
    pad = jnp.zeros((LANES - t_new, A_WIDTH), F32)
    caches = (c1_ref, c4_ref, c16_ref)
    jobs = [(i, g) for i in range(bb) for g in range(N_GROUPS)]

    ok_old, ok_new = [], []
    for d, c_ref in zip(SWA_DILATIONS, caches):
        window = c_ref.shape[-1]
        assert window == BAND * d, (window, d)
        pos_t = lax.broadcasted_iota(jnp.int32, (rows, window), 0) & (t_new - 1)
        pos = lax.broadcasted_iota(jnp.int32, (rows, window), 1)
        ok_old.append((pos >= pos_t) & (((pos - pos_t) & (d - 1)) == 0))
        ok_new.append((col <= row_t) & (((row_t - col) & (d - 1)) == 0))

    def new_rows(ref, i, g):
        sl = slice(g * A_WIDTH, (g + 1) * A_WIDTH)
        return jnp.concatenate([ref[i, :, sl], pad], axis=0).astype(BF16)

    def masked_q(i, g):
        q = q_ref[i, :, g * A_WIDTH:(g + 1) * A_WIDTH].astype(F32)
        return jnp.concatenate([q * m.astype(F32) for m in hmask], axis=0).astype(BF16)

    qms = [masked_q(i, g) for i, g in jobs]
    s_old = [jnp.where(ok_old[g], _dot(qm, caches[g][i, 0].astype(BF16)), -jnp.inf)
             for qm, (i, g) in zip(qms, jobs)]
    s_new = [jnp.where(ok_new[g], _dot_nt(qm, new_rows(k_ref, i, g)), -jnp.inf)
             for qm, (i, g) in zip(qms, jobs)]
    yield
    ms = [jnp.maximum(jnp.max(so, axis=-1, keepdims=True), jnp.max(sn, axis=-1, keepdims=True))
          for so, sn in zip(s_old, s_new)]
    p_old = [jnp.exp(so - m) for so, m in zip(s_old, ms)]
    p_new = [jnp.exp(sn - m) for sn, m in zip(s_new, ms)]
    dens = [jnp.sum(po, axis=-1, keepdims=True) + jnp.sum(pn, axis=-1, keepdims=True)
            for po, pn in zip(p_old, p_new)]
    yield
    accs = [_dot_nt(po.astype(BF16), caches[g][i, 1].astype(BF16))
            + _dot(pn.astype(BF16), new_rows(v_ref, i, g))
            for po, pn, (i, g) in zip(p_old, p_new, jobs)]
    yield

    for i in range(bb):
        num = jnp.zeros((t_new, A_WIDTH), F32)
        den_all = jnp.zeros((t_new, A_WIDTH), F32)
        m_all = jnp.full((t_new, A_WIDTH), -jnp.inf, F32)
        for g in range(N_GROUPS):
            j = jobs.index((i, g))
            o_g = jnp.zeros((t_new, A_WIDTH), F32)
            lse_g = jnp.zeros((t_new, A_WIDTH), F32)
            for h in range(A_HEADS):
                rs = slice(h * t_new, (h + 1) * t_new)
                o_g = jnp.where(hmask[h], accs[j][rs] * (1.0 / dens[j][rs]), o_g)
                lse_g = jnp.where(hmask[h], ms[j][rs] + jnp.log(dens[j][rs]), lse_g)
            m_next = jnp.maximum(m_all, lse_g)
            scale_old = jnp.exp(m_all - m_next)
            w = jnp.exp(lse_g - m_next)
            num = num * scale_old + w * o_g
            den_all = den_all * scale_old + w
            m_all = m_next
        o_ref[i] = (num * (1.0 / den_all)).astype(BF16)


def _attn_sample(qa, ka, va, c128, c512, c2048, batch, t_new, bb):
    w3 = N_GROUPS * A_WIDTH
    tok_spec = pl.BlockSpec((bb, t_new, w3), lambda b: (b, 0, 0))
    cache_views = [jnp.transpose(c, (0, 2, 3, 4, 1)).reshape(batch, 2, A_WIDTH, c.shape[1])
                   for c in (c128, c512, c2048)]
    cache_specs = [pl.BlockSpec((bb, 2, A_WIDTH, c.shape[-1]), lambda b: (b, 0, 0, 0))
                   for c in cache_views]
    out = pl.pallas_call(
        functools.partial(_attn_sample_kernel, t_new=t_new, bb=bb),
        grid=(batch // bb,),
        in_specs=[tok_spec, tok_spec, tok_spec] + cache_specs,
        out_specs=pl.BlockSpec((bb, t_new, A_WIDTH), lambda b: (b, 0, 0)),
        out_shape=jax.ShapeDtypeStruct((batch, t_new, A_WIDTH), BF16),
        compiler_params=_params(1),
        name="attn_sample",
    )(qa.reshape(batch, t_new, w3), ka.reshape(batch, t_new, w3), va.reshape(batch, t_new, w3),
      *cache_views)
    return out.reshape(batch * t_new, A_WIDTH)


def _split2(x):
    hi = x.astype(BF16)
    return hi, (x - hi.astype(F32)).astype(BF16)


def _gla_out(o, rg, gain):
    on = o * lax.rsqrt(jnp.mean(o * o, axis=-1, keepdims=True) + EPS) * gain
    return (on * rg * jax.nn.sigmoid(rg)).astype(BF16)


def _col_bcast(vec_row, width):
    sq = jnp.broadcast_to(vec_row, (LANES, LANES)).T
    return jnp.concatenate([sq] * (width // LANES), axis=1)


GLA_CHUNK = 1024
GLA_SEQS = 2


def _gla_prompt_kernel(q_ref, k_ref, v_ref, la_ref, rg_ref, gn_ref, o_ref, s_ref, *, n_blocks):
    cb, sub = GLA_BLOCK, GLA_SUB
    n_sub = cb // sub
    n_seq = q_ref.shape[0]

    @pl.when(pl.program_id(1) == 0)
    def _():
        s_ref[...] = jnp.zeros_like(s_ref)

    jobs = [(n, h) for n in range(n_seq) for h in range(G_HEADS)]

    def hk(x, h):
        return x[:, h * G_DK:(h + 1) * G_DK]

    def decays(n, rows, tri, rowk):
        hi, lo = _split2(la_ref[n, rows, :])
        b = _dot(tri, hi) + _dot(tri, lo)
        q = q_ref[n, rows, :].astype(F32)
        k = k_ref[n, rows, :].astype(F32)
        b_last = b[cb - 1:cb, :]
        qe = (q * jnp.exp(b)).astype(BF16)
        sub_end = [b[(j + 1) * sub - 1:(j + 1) * sub, :] for j in range(n_sub)]
        own_end = jnp.concatenate(
            [jnp.broadcast_to(e, (sub, G_KW)) for e in sub_end], axis=0)
        kk = k * jnp.exp(own_end - b)
        q_parts, k_parts = [], []
        for j in range(n_sub):
            lo_row = j * sub
            qj = q[lo_row:, :] * jnp.exp(b[lo_row:, :] - sub_end[j])
            q_parts.append(jnp.concatenate(
                [jnp.zeros((lo_row, G_KW), F32), qj], axis=0).astype(BF16) if lo_row else
                qj.astype(BF16))
            in_sub = (rowk >= lo_row) & (rowk < lo_row + sub)
            k_parts.append(jnp.where(in_sub, kk, 0.0).astype(BF16))
        return qe, q_parts, k_parts, k * jnp.exp(b_last - b), jnp.exp(b_last)

    def step(i, carry):
        rows = pl.ds(pl.multiple_of(i * cb, cb), cb)
        row = lax.broadcasted_iota(jnp.int32, (cb, cb), 0)
        col = lax.broadcasted_iota(jnp.int32, (cb, cb), 1)
        tri = (row >= col).astype(F32).astype(BF16)
        rowk = lax.broadcasted_iota(jnp.int32, (cb, G_KW), 0)

        per_seq = [decays(n, rows, tri, rowk) for n in range(n_seq)]
        states = [s_ref[n, h] for n, h in jobs]
        vals = [v_ref[n, rows, h * G_DV:(h + 1) * G_DV] for n, h in jobs]
        inter = [_dot(hk(per_seq[n][0], h), s.astype(BF16)) for s, (n, h) in zip(states, jobs)]
        atts = [_dot_nt(jnp.concatenate([hk(p, h) for p in per_seq[n][1]], axis=1),
                        jnp.concatenate([hk(p, h) for p in per_seq[n][2]], axis=1))
                for n, h in jobs]
        atts = [jnp.where(col <= row, a, 0.0).astype(BF16) for a in atts]
        outs = [x + _dot(a, v) for x, a, v in zip(inter, atts, vals)]
        for s, v, (n, h) in zip(states, vals, jobs):
            kd_t = hk(per_seq[n][3], h).T.astype(BF16)
            s_ref[n, h] = s * _col_bcast(hk(per_seq[n][4], h), G_DV) + _dot(kd_t, v)
        for o, (n, h) in zip(outs, jobs):
            vs = slice(h * G_DV, (h + 1) * G_DV)
            o_ref[n, rows, vs] = _gla_out(o, rg_ref[n, rows, vs].astype(F32), gn_ref[...])
        return carry

    lax.fori_loop(0, n_blocks, step, 0)


def _gla_prompt(qg, kg, vg, la, rg, gn, batch, seq):
    assert seq % GLA_CHUNK == 0 and batch % GLA_SEQS == 0
    chunks = seq // GLA_CHUNK

    def spec(width):
        return pl.BlockSpec((GLA_SEQS, GLA_CHUNK, width), lambda b, c: (b, c, 0))

    def per_seq(x):
        return x.reshape(batch, seq, x.shape[-1])
    o, s = pl.pallas_call(
        functools.partial(_gla_prompt_kernel, n_blocks=GLA_CHUNK // GLA_BLOCK),
        grid=(batch // GLA_SEQS, chunks),
        in_specs=[spec(G_KW), spec(G_KW), spec(G_VW), spec(G_KW), spec(G_VW),
                  _resident((1, G_DV))],
        out_specs=[spec(G_VW),
                   pl.BlockSpec((GLA_SEQS, G_HEADS, G_DK, G_DV), lambda b, c: (b, 0, 0, 0))],
        out_shape=[jax.ShapeDtypeStruct((batch, seq, G_VW), BF16),
                   jax.ShapeDtypeStruct((batch, G_HEADS, G_DK, G_DV), F32)],
        compiler_params=_params(2),
        name="gla_prompt",
    )(per_seq(qg), per_seq(kg), per_seq(vg), per_seq(la), per_seq(rg), gn)
    return o.reshape(batch * seq, G_VW), s


def _gla_sample_kernel(q_ref, k_ref, v_ref, la_ref, rg_ref, gn_ref, s0_ref, o_ref, s_ref,
                       *, bb, t_new):
    per_body = 2

    def hk(x, h):
        return x[:, h * G_DK:(h + 1) * G_DK]

    def hv(x, h):
        return x[:, h * G_DV:(h + 1) * G_DV]

    def body(i0, carry):
        trow = lax.broadcasted_iota(jnp.int32, (t_new, G_KW), 0)
        arow = lax.broadcasted_iota(jnp.int32, (t_new, LANES), 0)
        acol = lax.broadcasted_iota(jnp.int32, (t_new, LANES), 1)
        zk = jnp.zeros((LANES - t_new, G_KW), F32)
        zv = jnp.zeros((LANES - t_new, G_VW), F32)
        seqs = [per_body * i0 + j for j in range(per_body)]
        jobs = [(n, h) for n in range(per_body) for h in range(G_HEADS)]

        bs = []
        for i in seqs:
            b = la_ref[i]
            shift = 1
            while shift < t_new:
                b = b + jnp.where(trow >= shift, pltpu.roll(b, shift, 0), 0.0)
                shift *= 2
            bs.append(b)
        qs = [q_ref[i].astype(F32) for i in seqs]
        ks = [k_ref[i].astype(F32) for i in seqs]
        vs = [jnp.concatenate([v_ref[i].astype(F32), zv], axis=0).astype(BF16) for i in seqs]
        b_last = [b[t_new - 1:t_new, :] for b in bs]
        qe = [(q * jnp.exp(b)).astype(BF16) for q, b in zip(qs, bs)]
        ke = [jnp.concatenate([k * jnp.exp(-b), zk], axis=0).astype(BF16) for k, b in zip(ks, bs)]
        kd = [jnp.concatenate([k * jnp.exp(bl - b), zk], axis=0)
              for k, b, bl in zip(ks, bs, b_last)]
        decay = [jnp.exp(bl) for bl in b_last]

        s0 = [s0_ref[seqs[n], h] for n, h in jobs]
        att = [jnp.where(acol <= arow, _dot_nt(hk(qe[n], h), hk(ke[n], h)), 0.0).astype(BF16)
               for n, h in jobs]
        out = [_dot(a, hv(vs[n], h)) + _dot(hk(qe[n], h), s.astype(BF16))
               for a, s, (n, h) in zip(att, s0, jobs)]
        for s, (n, h) in zip(s0, jobs):
            s_ref[seqs[n], h] = (s * _col_bcast(hk(decay[n], h), G_DV)
                                 + _dot(hk(kd[n], h).T.astype(BF16), hv(vs[n], h)))
        for o, (n, h) in zip(out, jobs):
            sl = slice(h * G_DV, (h + 1) * G_DV)
            o_ref[seqs[n], :, sl] = _gla_out(o, rg_ref[seqs[n], :, sl].astype(F32), gn_ref[...])
        return carry

    assert bb % per_body == 0
    lax.fori_loop(0, bb // per_body, body, 0)


def _gla_sample(qg, kg, vg, la, rg, gn, s0, batch, t_new, bb):
    def spec(width):
        return pl.BlockSpec((bb, t_new, width), lambda b: (b, 0, 0))
    st_spec = pl.BlockSpec((bb, G_HEADS, G_DK, G_DV), lambda b: (b, 0, 0, 0))
    o, s = pl.pallas_call(
        functools.partial(_gla_sample_kernel, bb=bb, t_new=t_new),
        grid=(batch // bb,),
        in_specs=[spec(G_KW), spec(G_KW), spec(G_VW), spec(G_KW), spec(G_VW),
                  _resident((1, G_DV)), st_spec],
        out_specs=[spec(G_VW), st_spec],
        out_shape=[jax.ShapeDtypeStruct((batch, t_new, G_VW), BF16),
                   jax.ShapeDtypeStruct((batch, G_HEADS, G_DK, G_DV), F32)],
        compiler_params=_params(1),
        name="gla_sample",
    )(qg.reshape(batch, t_new, G_KW), kg.reshape(batch, t_new, G_KW),
      vg.reshape(batch, t_new, G_VW), la.reshape(batch, t_new, G_KW),
      rg.reshape(batch, t_new, G_VW), gn, s0)
    return o.reshape(batch * t_new, G_VW), s


def _out_kernel(x_ref, a_ref, gb_ref, gate_ref, wa_ref, wb_ref, wo_ref, g_ref, wg_ref, wu_ref,
                wd_ref, o_ref, act_ref, between=None):
    a_out = _dot(a_ref[...], wa_ref[...])
    b_out = _dot(gb_ref[...], wb_ref[...])
    mix = (gate_ref[:, 0:D_MODEL].astype(F32) * a_out
           + gate_ref[:, D_MODEL:2 * D_MODEL].astype(F32) * b_out)
    x2 = x_ref[...] + _dot(mix.astype(BF16), wo_ref[...])
    o_ref[...] = _swiglu_residual(x2, g_ref, wg_ref, wu_ref, wd_ref, act_ref, between)


def _out(x1, attn, gb, gates, wa, wb, wo, g, wg, wu, wd, tm):
    t = x1.shape[0]

    def tok(width):
        return pl.BlockSpec((tm, width), lambda i: (i, 0))
    consts = [wa, wb, wo, g, wg, wu, wd]
    return pl.pallas_call(
        _out_kernel,
        grid=(t // tm,),
        in_specs=[tok(D_MODEL), tok(A_WIDTH), tok(G_VW), tok(2 * D_MODEL)]
        + [_resident(c.shape) for c in consts],
        out_specs=tok(D_MODEL),
        out_shape=jax.ShapeDtypeStruct((t, D_MODEL), F32),
        scratch_shapes=[pltpu.VMEM((tm, D_FF), BF16)],
        compiler_params=_params(1),
        name="out_ffn2",
    )(x1, attn, gb, gates, *consts)


def _out_with_sample_attn_kernel(*refs, t_new, bb):
    out_in, attn_in = refs[:11], refs[11:17]
    y_ref, a_ref, act_ref = refs[17:]
    stages = _attn_sample_stages(*attn_in, a_ref, t_new=t_new, bb=bb)
    every = 3

    def between(c):
        if c % every == every - 1:
            next(stages, None)

    _out_kernel(*out_in, y_ref, act_ref, between=between)
    for _ in stages:
        pass


def _out_with_sample_attn(x1, attn, gb, gates, wa, wb, wo, g, wg, wu, wd,
                          qa, ka, va, c128, c512, c2048, batch, t_new, tm):
    t = x1.shape[0]
    steps = t // tm
    assert batch % steps == 0
    bb = batch // steps
    w3 = N_GROUPS * A_WIDTH

    def tok(width):
        return pl.BlockSpec((tm, width), lambda i: (i, 0))
    consts = [wa, wb, wo, g, wg, wu, wd]
    new_spec = pl.BlockSpec((bb, t_new, w3), lambda i: (i, 0, 0))
    cache_views = [jnp.transpose(c, (0, 2, 3, 4, 1)).reshape(batch, 2, A_WIDTH, c.shape[1])
                   for c in (c128, c512, c2048)]
    cache_specs = [pl.BlockSpec((bb, 2, A_WIDTH, c.shape[-1]), lambda i: (i, 0, 0, 0))
                   for c in cache_views]
    y, a_s = pl.pallas_call(
        functools.partial(_out_with_sample_attn_kernel, t_new=t_new, bb=bb),
        grid=(steps,),
        in_specs=[tok(D_MODEL), tok(A_WIDTH), tok(G_VW), tok(2 * D_MODEL)]
        + [_resident(c.shape) for c in consts] + [new_spec] * 3 + cache_specs,
        out_specs=[tok(D_MODEL), pl.BlockSpec((bb, t_new, A_WIDTH), lambda i: (i, 0, 0))],
        out_shape=[jax.ShapeDtypeStruct((t, D_MODEL), F32),
                   jax.ShapeDtypeStruct((batch, t_new, A_WIDTH), BF16)],
        scratch_shapes=[pltpu.VMEM((tm, D_FF), BF16)],
        compiler_params=_params(1),
        name="out_ffn2_attn_sample",
    )(x1, attn, gb, gates, *consts,
      qa.reshape(batch, t_new, w3), ka.reshape(batch, t_new, w3), va.reshape(batch, t_new, w3),
      *cache_views)
    return y, a_s.reshape(batch * t_new, A_WIDTH)


def _sample_rows_kernel(k_ref, v_ref, *refs, batch, t_new):
    o_refs, slab_ref = refs[:N_GROUPS], refs[N_GROUPS]
    n_slab = N_GROUPS * A_WIDTH // LANES
    per_group = A_WIDTH // LANES
    for kv, src in enumerate((k_ref, v_ref)):
        for s in range(n_slab):
            slab_ref[kv, s] = src[:, s * LANES:(s + 1) * LANES]
    for kv in range(2):
        for s in range(n_slab):
            g, part = divmod(s, per_group)
            for t in range(t_new):
                rows = slab_ref[kv, s, pl.ds(t, batch, stride=t_new), :]
                o_refs[g][t, kv, part * LANES:(part + 1) * LANES, :] = rows.T


def _sample_rows(ka, va, batch, t_new):
    assert batch == LANES
    n_slab = N_GROUPS * A_WIDTH // LANES
    outs = pl.pallas_call(
        functools.partial(_sample_rows_kernel, batch=batch, t_new=t_new),
        out_shape=[jax.ShapeDtypeStruct((t_new, 2, A_WIDTH, batch), F32)] * N_GROUPS,
        scratch_shapes=[pltpu.VMEM((2, n_slab, batch * t_new, LANES), F32)],
        compiler_params=pltpu.CompilerParams(vmem_limit_bytes=VMEM_LIMIT),
        name="sample_rows",
    )(ka, va)
    return [jnp.transpose(o.reshape(t_new, 2, A_HEADS, A_HEAD_DIM, batch), (4, 0, 1, 2, 3))
            for o in outs]


def _position_major(rows, batch):
    window = rows.shape[-1]
    return jnp.transpose(rows.reshape(batch, 2, A_HEADS, A_HEAD_DIM, window), (0, 4, 1, 2, 3))


def kernel(x_prompt, x_sample, cache_swa128_kv, cache_swa512_kv, cache_swa2048_kv, state_gla,
           norm_ffn1, ffn1_gate, ffn1_up, ffn1_down, norm_mix, w_in, a_q_norm, a_k_norm,
           g_alpha_up, g_alpha_bias, g_out_norm, w_a_out, w_b_out, w_out,
           norm_ffn2, ffn2_gate, ffn2_up, ffn2_down):
    batch, seq, _ = x_prompt.shape
    dec_batch, dec_seq, _ = x_sample.shape

    aw = N_GROUPS * A_WIDTH
    o_g = 3 * aw
    o_lr = o_g + 2 * G_KW + 2 * G_VW
    o_gate = o_lr + G_LOWRANK
    w_attn = w_in[:, :o_g].astype(BF16)
    w_gla = w_in[:, o_g:o_lr].astype(BF16)
    w_lr = jnp.pad(w_in[:, o_lr:o_gate], ((0, 0), (0, LANES - G_LOWRANK))).astype(BF16)
    w_gate = w_in[:, o_gate:].astype(BF16)
    up = jnp.pad(g_alpha_up, ((0, LANES - G_LOWRANK), (0, 0))).astype(BF16)
    bias = g_alpha_bias.reshape(1, G_KW)
    qgain = jnp.tile(a_q_norm, (1, A_HEADS)).reshape(1, aw)
    kgain = jnp.tile(a_k_norm, (1, A_HEADS)).reshape(1, aw)
    lane = jnp.arange(A_WIDTH) // A_HEAD_DIM
    hmean = jnp.where(lane[:, None] == lane[None, :], 1.0 / A_HEAD_DIM, 0.0).astype(BF16)
    gn = g_out_norm.reshape(1, G_DV)
    ffn1 = (norm_ffn1.reshape(1, D_MODEL), ffn1_gate.astype(BF16), ffn1_up.astype(BF16),
            ffn1_down.astype(BF16))
    ffn2 = (norm_ffn2.reshape(1, D_MODEL), ffn2_gate.astype(BF16), ffn2_up.astype(BF16),
            ffn2_down.astype(BF16))
    proj_w = (norm_mix.reshape(1, D_MODEL), w_attn, w_gla, w_lr, w_gate, up, bias,
              qgain, kgain, hmean)
    out_w = (w_a_out.astype(BF16), w_b_out.astype(BF16), w_out.astype(BF16))

    x1s = _ffn(x_sample.reshape(dec_batch * dec_seq, D_MODEL), *ffn1, tm=512)
    qa, ka, va, qg_s, kg_s, vg_s, rg_s, la_s, gates_s = _proj(x1s, *proj_w, tm=512)
    rows_s = _sample_rows(ka, va, dec_batch, dec_seq)

    x1p = _ffn(x_prompt.reshape(batch * seq, D_MODEL), *ffn1, tm=512)
    proj_p = _proj(x1p, *proj_w, tm=512, seq=seq)
    qkv_units, rows_p, (qg, kg, vg, rg, la, gates) = proj_p[:9], proj_p[9:12], proj_p[12:]
    a_p = _attn_prompt(qkv_units, batch, seq)
    gbp, state_p = _gla_prompt(qg, kg, vg, la, rg, gn, batch, seq)
    y_prompt, a_s = _out_with_sample_attn(
        x1p, a_p, gbp, gates, *out_w, *ffn2, qa, ka, va,
        cache_swa128_kv, cache_swa512_kv, cache_swa2048_kv, dec_batch, dec_seq, tm=256)
    y_prompt = y_prompt.reshape(batch, seq, D_MODEL)
    rows_p = [_position_major(r, batch) for r in rows_p]

    gbs, state_s = _gla_sample(qg_s, kg_s, vg_s, la_s, rg_s, gn, state_gla, dec_batch, dec_seq,
                               bb=8)
    y_sample = _out(x1s, a_s, gbs, gates_s, *out_w, *ffn2, tm=512).reshape(
        dec_batch, dec_seq, D_MODEL)

    return (y_prompt, y_sample, rows_p[0], rows_p[1], rows_p[2], state_p,
            rows_s[0], rows_s[1], rows_s[2], state_s)
```

```python
import functools

import jax
import jax.numpy as jnp
from jax import lax
from jax.experimental import pallas as pl
from jax.experimental.pallas import tpu as pltpu

F32 = jnp.float32
BF16 = jnp.bfloat16

D_MODEL = 1024
D_FF = 2816
SWA_DILATIONS = (1, 4, 16)
N_GROUPS = 3
A_HEADS = 4
A_HEAD_DIM = 64
A_WIDTH = A_HEADS * A_HEAD_DIM
BAND = 128
G_HEADS = 4
G_DK = 128
G_DV = 256
G_KW = G_HEADS * G_DK
G_VW = G_HEADS * G_DV
G_LOWRANK = 16
G_TAU = 16.0
EPS = 1e-6

LANES = 128
GLA_BLOCK = 128
GLA_SUB = 32
FF_CHUNK = 256
N_CHUNK = 512
VMEM_LIMIT = 58 * 1024 * 1024


def _resident(shape):
    nd = len(shape)
    return pl.BlockSpec(shape, lambda *_: (0,) * nd, pipeline_mode=pl.Buffered(1))


def _params(n_axes):
    return pltpu.CompilerParams(
        dimension_semantics=("arbitrary",) * n_axes, vmem_limit_bytes=VMEM_LIMIT)


def _dot(a, b):
    return jnp.dot(a, b, preferred_element_type=F32)


def _dot_nt(a, b):
    return lax.dot_general(a, b, (((1,), (1,)), ((), ())), preferred_element_type=F32)


def _rmsnorm(x, g):
    return x * lax.rsqrt(jnp.mean(x * x, axis=-1, keepdims=True) + EPS) * g


def _swiglu_residual(x, g_ref, wg_ref, wu_ref, wd_ref, act_ref, between=None):
    h = _rmsnorm(x, g_ref[...]).astype(BF16)
    for c in range(D_FF // FF_CHUNK):
        sl = slice(c * FF_CHUNK, (c + 1) * FF_CHUNK)
        gate = _dot(h, wg_ref[:, sl].astype(BF16))
        up = _dot(h, wu_ref[:, sl].astype(BF16))
        act_ref[:, sl] = (gate * jax.nn.sigmoid(gate) * up).astype(BF16)
        if between is not None:
            between(c)
    return x + 0.5 * _dot(act_ref[...], wd_ref[...].astype(BF16))


def _ffn_kernel(xa_ref, xb_ref, g_ref, wg_ref, wu_ref, wd_ref, o_ref, act_ref, *, steps_a):
    x = jnp.where(pl.program_id(0) < steps_a, xa_ref[...], xb_ref[...])
    o_ref[...] = _swiglu_residual(x, g_ref, wg_ref, wu_ref, wd_ref, act_ref)


def _ffn(xa, xb, g, wg, wu, wd, tm):
    steps_a, steps_b = xa.shape[0] // tm, xb.shape[0] // tm
    assert xa.shape[0] % tm == 0 and xb.shape[0] % tm == 0
    return pl.pallas_call(
        functools.partial(_ffn_kernel, steps_a=steps_a),
        grid=(steps_a + steps_b,),
        in_specs=[
            pl.BlockSpec((tm, D_MODEL), lambda i: (jnp.minimum(i, steps_a - 1), 0)),
            pl.BlockSpec((tm, D_MODEL), lambda i: (jnp.maximum(i - steps_a, 0), 0)),
            _resident((1, D_MODEL)),
            _resident((D_MODEL, D_FF)),
            _resident((D_MODEL, D_FF)),
            _resident((D_FF, D_MODEL)),
        ],
        out_specs=pl.BlockSpec((tm, D_MODEL), lambda i: (i, 0)),
        out_shape=jax.ShapeDtypeStruct((xa.shape[0] + xb.shape[0], D_MODEL), F32),
        scratch_shapes=[pltpu.VMEM((tm, D_FF), BF16)],
        compiler_params=_params(1),
        name="ffn1",
    )(xa, xb, g, wg, wu, wd)


def _log_sigmoid(z):
    return jnp.minimum(z, 0.0) - jnp.log1p(jnp.exp(-jnp.abs(z)))


def _store_blocked(x, d, out_ref, perm_ref):
    tm = x.shape[0]
    if d == 1:
        for j in range(tm // BAND):
            out_ref[j] = x[j * BAND:(j + 1) * BAND].astype(BF16)
        return
    for s in range(A_WIDTH // LANES):
        perm_ref[s] = x[:, s * LANES:(s + 1) * LANES]
    for r in range(d):
        rows = jnp.concatenate(
            [perm_ref[s, pl.ds(r, tm // d, stride=d), :] for s in range(A_WIDTH // LANES)], axis=1)
        out_ref[r] = rows.astype(BF16)


def _proj_kernel(*refs, blocked, tiles_per_seq):
    (x_ref, g_ref, wa_ref, wg_ref, wlr_ref, wgate_ref, up_ref, bias_ref,
     qgain_ref, kgain_ref, hmean_ref) = refs[:11]
    outs = refs[11:]
    if blocked:
        qb_refs, kb_refs, vb_refs, row_refs = outs[0:3], outs[3:6], outs[6:9], outs[9:12]
        qg_ref, kg_ref, vg_ref, rg_ref, la_ref, gate_ref, perm_ref = outs[12:]
        tm = x_ref.shape[0]
    else:
        qa_ref, ka_ref, va_ref, qg_ref, kg_ref, vg_ref, rg_ref, la_ref, gate_ref = outs

    h = _rmsnorm(x_ref[...], g_ref[...]).astype(BF16)
    hmean = hmean_ref[...]

    aw = N_GROUPS * A_WIDTH
    groups = range(N_GROUPS)

    def attn_cols(part, g):
        off = part * aw + g * A_WIDTH
        return wa_ref[:, off:off + A_WIDTH]

    q_raw = [_dot(h, attn_cols(0, g)) for g in groups]
    k_raw = [_dot(h, attn_cols(1, g)) for g in groups]
    vals = [_dot(h, attn_cols(2, g)) for g in groups]

    qg_ref[...] = (_dot(h, wg_ref[:, 0:G_KW]) * (G_DK ** -0.5)).astype(BF16)
    kg_ref[...] = _dot(h, wg_ref[:, G_KW:2 * G_KW]).astype(BF16)

    q_ms = [_dot((x * x).astype(BF16), hmean) for x in q_raw]
    k_ms = [_dot((x * x).astype(BF16), hmean) for x in k_raw]

    def attn_epilogue(g):
        d = SWA_DILATIONS[g]
        sl = slice(g * A_WIDTH, (g + 1) * A_WIDTH)
        q = q_raw[g] * lax.rsqrt(q_ms[g] + EPS) * qgain_ref[:, sl] * (A_HEAD_DIM ** -0.5)
        k = k_raw[g] * lax.rsqrt(k_ms[g] + EPS) * kgain_ref[:, sl]
        v = vals[g]
        if blocked:
            for i, (x, o_refs) in enumerate(((q, qb_refs), (k, kb_refs), (v, vb_refs))):
                _store_blocked(x, d, o_refs[g], perm_ref.at[(g - 1) * 3 + i] if d > 1 else None)
            keep = min(BAND * d, tm)
            row_refs[g][0] = k[tm - keep:, :].T
            row_refs[g][1] = v[tm - keep:, :].T
        else:
            ka_ref[:, sl] = k
            va_ref[:, sl] = v
            qa_ref[:, sl] = q.astype(BF16)

    n_gate = 2 * D_MODEL // N_CHUNK
    for c in range(n_gate):
        sl = slice(c * N_CHUNK, (c + 1) * N_CHUNK)
        gate_ref[:, sl] = jax.nn.sigmoid(_dot(h, wgate_ref[:, sl])).astype(BF16)
        if c < N_GROUPS:
            attn_epilogue(c)
    assert n_gate >= N_GROUPS

    for c in range(G_VW // N_CHUNK):
        sl = slice(c * N_CHUNK, (c + 1) * N_CHUNK)
        off = 2 * G_KW + c * N_CHUNK
        vg_ref[:, sl] = _dot(h, wg_ref[:, off:off + N_CHUNK]).astype(BF16)
        rg_ref[:, sl] = _dot(h, wg_ref[:, G_VW + off:G_VW + off + N_CHUNK]).astype(BF16)

    lr = _dot(h, wlr_ref[...]).astype(BF16)
    z = _dot(lr, up_ref[...]) + bias_ref[...]
    la_ref[...] = _log_sigmoid(z) * (1.0 / G_TAU)


def _unit_spec(d, tm):
    per_class = tm // d
    if per_class >= BAND:
        return pl.BlockSpec((tm // BAND, BAND, A_WIDTH), lambda i: (i, 0, 0))
    parts = BAND // per_class
    return pl.BlockSpec((d, per_class, A_WIDTH), lambda i: (i // parts, i % parts, 0))


def _row_spec(d, tm, tiles_per_seq):
    window = BAND * d
    if window <= tm:
        return pl.BlockSpec((None, 2, A_WIDTH, window), lambda i: (i // tiles_per_seq, 0, 0, 0))
    first_tile = tiles_per_seq - window // tm
    return pl.BlockSpec(
        (None, 2, A_WIDTH, tm),
        lambda i: (i // tiles_per_seq, 0, 0, jnp.maximum(i % tiles_per_seq - first_tile, 0)))


def _proj(x1, g, wa, wg, wlr, wgate, up, bias, qgain, kgain, hmean, tm, rows, seq=None):
    t = rows.stop - rows.start
    assert rows.start % tm == 0 and t % tm == 0
    block0 = rows.start // tm
    aw = N_GROUPS * A_WIDTH
    blocked = seq is not None
    widths = [(G_KW, BF16), (G_KW, BF16), (G_VW, BF16), (G_VW, BF16), (G_KW, F32),
              (2 * D_MODEL, BF16)]
    scratch = []
    tiles_per_seq = None
    if blocked:
        assert tm == 4 * BAND and SWA_DILATIONS == (1, 4, 16)
        assert seq % tm == 0 and BAND * SWA_DILATIONS[-1] <= seq
        tiles_per_seq = seq // tm
        out_specs = ([_unit_spec(d, tm) for _ in range(3) for d in SWA_DILATIONS]
                     + [_row_spec(d, tm, tiles_per_seq) for d in SWA_DILATIONS])
        out_shape = ([jax.ShapeDtypeStruct((t // BAND, BAND, A_WIDTH), BF16)] * 9
                     + [jax.ShapeDtypeStruct((t // seq, 2, A_WIDTH, BAND * d), F32)
                        for d in SWA_DILATIONS])
        scratch = [pltpu.VMEM((3 * (N_GROUPS - 1), A_WIDTH // LANES, tm, LANES), F32)]
    else:
        widths = [(aw, BF16), (aw, F32), (aw, F32)] + widths
        out_specs, out_shape = [], []
    out_specs = out_specs + [pl.BlockSpec((tm, w), lambda i: (i, 0)) for w, _ in widths]
    out_shape = out_shape + [jax.ShapeDtypeStruct((t, w), dt) for w, dt in widths]
    consts = [g, wa, wg, wlr, wgate, up, bias, qgain, kgain, hmean]
    return pl.pallas_call(
        functools.partial(_proj_kernel, blocked=blocked, tiles_per_seq=tiles_per_seq),
        grid=(t // tm,),
        in_specs=[pl.BlockSpec((tm, D_MODEL), lambda i: (i + block0, 0))]
        + [_resident(c.shape) for c in consts],
        out_specs=out_specs,
        out_shape=out_shape,
        scratch_shapes=scratch,
        compiler_params=_params(1),
        name="in_proj",
    )(x1, *consts)


def _log2(n):
    assert n > 0 and n & (n - 1) == 0, n
    return n.bit_length() - 1


def _head_masks(rows):
    lane = lax.broadcasted_iota(jnp.int32, (rows, A_WIDTH), 1)
    return [(lane >> _log2(A_HEAD_DIM)) == h for h in range(A_HEADS)]


ATTN_SPAN = 2048
ATTN_UNROLL = 2


def _attn_prompt_kernel(*refs, units):
    q_refs, k_refs, v_refs = refs[0:3], refs[3:6], refs[6:9]
    o_ref, osc_ref, lsc_ref, bias_ref = refs[9:]
    span = pl.program_id(1)
    n_slab = A_WIDTH // LANES

    row = lax.broadcasted_iota(jnp.int32, (A_HEADS * BAND, 2 * BAND), 0) & (BAND - 1)
    col = lax.broadcasted_iota(jnp.int32, (A_HEADS * BAND, 2 * BAND), 1)
    own = (col >= BAND) & (col - BAND <= row)
    bias_ref[0] = jnp.where(own, 0.0, -jnp.inf)
    bias_ref[1] = jnp.where(own | ((col < BAND) & (col >= row)), 0.0, -jnp.inf)

    for g, d in enumerate(SWA_DILATIONS):
        q_ref, k_ref, v_ref = q_refs[g], k_refs[g], v_refs[g]

        def scores(ul, d=d, q_ref=q_ref, k_ref=k_ref):
            u = span * units + ul
            up = jnp.maximum(u - d, 0)
            kb = jnp.concatenate([k_ref[up], k_ref[u]], axis=0)
            q = q_ref[ul]
            qm = jnp.concatenate(
                [q * m.astype(F32).astype(BF16) for m in _head_masks(BAND)], axis=0)
            has_prev = jnp.where(u >= d, 1, 0)
            return _dot_nt(qm, kb) + bias_ref[has_prev]

        def softmax(s):
            m = jnp.max(s, axis=-1, keepdims=True)
            p = jnp.exp(s - m)
            den = jnp.sum(p, axis=-1, keepdims=True)
            return p.astype(BF16), 1.0 / den, m + jnp.log(den)

        def values(ul, p, d=d, v_ref=v_ref):
            u = span * units + ul
            vb = jnp.concatenate([v_ref[jnp.maximum(u - d, 0)], v_ref[u]], axis=0)
            return _dot(p, vb)

        def store(ul, o_all, inv_den, lse_all, d=d, g=g):
            hmask = _head_masks(BAND)
            o_all = o_all * inv_den
            o_acc = jnp.zeros((BAND, A_WIDTH), F32)
            lse_acc = jnp.zeros((BAND, A_WIDTH), F32)
            for h in range(A_HEADS):
                rs = slice(h * BAND, (h + 1) * BAND)
                o_acc = jnp.where(hmask[h], o_all[rs], o_acc)
                lse_acc = jnp.where(hmask[h], lse_all[rs], lse_acc)
            start = (ul >> _log2(d)) * (BAND * d) + (ul & (d - 1))
            if d == 1:
                rows = pl.ds(pl.multiple_of(start, BAND), BAND)
            else:
                rows = pl.ds(start, BAND, stride=d)
            for sl in range(n_slab):
                osc_ref[g, sl, rows, :] = o_acc[:, sl * LANES:(sl + 1) * LANES]
                lsc_ref[g, sl, rows, :] = lse_acc[:, sl * LANES:(sl + 1) * LANES]

        def unit_group(i, carry, scores=scores, softmax=softmax, values=values, store=store):
            uls = [ATTN_UNROLL * i + j for j in range(ATTN_UNROLL)]
            ss = [scores(ul) for ul in uls]
            sm = [softmax(s) for s in ss]
            os_ = [values(ul, p) for ul, (p, _, _) in zip(uls, sm)]
            for ul, o_all, (_, inv_den, lse_all) in zip(uls, os_, sm):
                store(ul, o_all, inv_den, lse_all)
            return carry

        assert units % ATTN_UNROLL == 0
        lax.fori_loop(0, units // ATTN_UNROLL, unit_group, 0)

    def merge(i, carry):
        rows = pl.ds(pl.multiple_of(i * BAND, BAND), BAND)

        def natural(ref, g):
            return jnp.concatenate([ref[g, sl, rows, :] for sl in range(n_slab)], axis=1)

        outs = [natural(osc_ref, g) for g in range(N_GROUPS)]
        lses = [natural(lsc_ref, g) for g in range(N_GROUPS)]
        m = functools.reduce(jnp.maximum, lses)
        ws = [jnp.exp(l - m) for l in lses]
        a = functools.reduce(jnp.add, [w * o for w, o in zip(ws, outs)])
        o_ref[rows, :] = (a * (1.0 / functools.reduce(jnp.add, ws))).astype(BF16)
        return carry

    lax.fori_loop(0, ATTN_SPAN // BAND, merge, 0)


def _attn_prompt(qkv_units, batch, seq):
    assert seq % ATTN_SPAN == 0
    units = ATTN_SPAN // BAND
    per_seq = seq // BAND
    spans = seq // ATTN_SPAN
    q_spec = pl.BlockSpec((units, BAND, A_WIDTH), lambda b, s: (b * spans + s, 0, 0))
    kv_spec = pl.BlockSpec((per_seq, BAND, A_WIDTH), lambda b, s: (b, 0, 0))
    return pl.pallas_call(
        functools.partial(_attn_prompt_kernel, units=units),
        grid=(batch, spans),
        in_specs=[q_spec] * 3 + [kv_spec] * 6,
        out_specs=pl.BlockSpec((ATTN_SPAN, A_WIDTH), lambda b, s: (b * spans + s, 0)),
        out_shape=jax.ShapeDtypeStruct((batch * seq, A_WIDTH), BF16),
        scratch_shapes=[pltpu.VMEM((N_GROUPS, A_WIDTH // LANES, ATTN_SPAN, LANES), F32)] * 2
        + [pltpu.VMEM((2, A_HEADS * BAND, 2 * BAND), F32)],
        compiler_params=_params(2),
        name="attn_prompt",
    )(*qkv_units)


def _attn_sample_kernel(*refs, t_new, bb):
    for _ in _attn_sample_stages(*refs, t_new=t_new, bb=bb):
        pass


def _attn_sample_stages(q_ref, k_ref, v_ref, c1_ref, c4_ref, c16_ref, o_ref, *, t_new, bb):
    hmask = _head_masks(t_new)
    rows = A_HEADS * t_new
    row_t = lax.broadcasted_iota(jnp.int32, (rows, LANES), 0) & (t_new - 1)
    col = lax.broadcasted_iota(jnp.int32, (rows, LANES), 1)
    pad = jnp.zeros((LANES - t_new, A_WIDTH), F32)
    caches = (c1_ref, c4_ref, c16_ref)
    jobs = [(i, g) for i in range(bb) for g in range(N_GROUPS)]

    ok_old, ok_new = [], []
    for d, c_ref in zip(SWA_DILATIONS, caches):
        window = c_ref.shape[-1]
        assert window == BAND * d, (window, d)
        pos_t = lax.broadcasted_iota(jnp.int32, (rows, window), 0) & (t_new - 1)
        pos = lax.broadcasted_iota(jnp.int32, (rows, window), 1)
        ok_old.append((pos >= pos_t) & (((pos - pos_t) & (d - 1)) == 0))
        ok_new.append((col <= row_t) & (((row_t - col) & (d - 1)) == 0))

    def new_rows(ref, i, g):
        sl = slice(g * A_WIDTH, (g + 1) * A_WIDTH)
        return jnp.concatenate([ref[i, :, sl], pad], axis=0).astype(BF16)

    def masked_q(i, g):
        q = q_ref[i, :, g * A_WIDTH:(g + 1) * A_WIDTH].astype(F32)
        return jnp.concatenate([q * m.astype(F32) for m in hmask], axis=0).astype(BF16)

    qms = [masked_q(i, g) for i, g in jobs]
    s_old = [jnp.where(ok_old[g], _dot(qm, caches[g][i, 0].astype(BF16)), -jnp.inf)
             for qm, (i, g) in zip(qms, jobs)]
    s_new = [jnp.where(ok_new[g], _dot_nt(qm, new_rows(k_ref, i, g)), -jnp.inf)
             for qm, (i, g) in zip(qms, jobs)]
    yield
    ms = [jnp.maximum(jnp.max(so, axis=-1, keepdims=True), jnp.max(sn, axis=-1, keepdims=True))
          for so, sn in zip(s_old, s_new)]
    p_old = [jnp.exp(so - m) for so, m in zip(s_old, ms)]
    p_new = [jnp.exp(sn - m) for sn, m in zip(s_new, ms)]
    dens = [jnp.sum(po, axis=-1, keepdims=True) + jnp.sum(pn, axis=-1, keepdims=True)
            for po, pn in zip(p_old, p_new)]
    yield
    accs = [_dot_nt(po.astype(BF16), caches[g][i, 1].astype(BF16))
            + _dot(pn.astype(BF16), new_rows(v_ref, i, g))
            for po, pn, (i, g) in zip(p_old, p_new, jobs)]
    yield

    for i in range(bb):
        num = jnp.zeros((t_new, A_WIDTH), F32)
        den_all = jnp.zeros((t_new, A_WIDTH), F32)
        m_all = jnp.full((t_new, A_WIDTH), -jnp.inf, F32)
        for g in range(N_GROUPS):
            j = jobs.index((i, g))
            o_g = jnp.zeros((t_new, A_WIDTH), F32)
            lse_g = jnp.zeros((t_new, A_WIDTH), F32)
            for h in range(A_HEADS):
                rs = slice(h * t_new, (h + 1) * t_new)
                o_g = jnp.where(hmask[h], accs[j][rs] * (1.0 / dens[j][rs]), o_g)
                lse_g = jnp.where(hmask[h], ms[j][rs] + jnp.log(dens[j][rs]), lse_g)
            m_next = jnp.maximum(m_all, lse_g)
            scale_old = jnp.exp(m_all - m_next)
            w = jnp.exp(lse_g - m_next)
            num = num * scale_old + w * o_g
            den_all = den_all * scale_old + w
            m_all = m_next
        o_ref[i] = (num * (1.0 / den_all)).astype(BF16)


def _attn_sample(qa, ka, va, c128, c512, c2048, batch, t_new, bb):
    w3 = N_GROUPS * A_WIDTH
    tok_spec = pl.BlockSpec((bb, t_new, w3), lambda b: (b, 0, 0))
    cache_views = [jnp.transpose(c, (0, 2, 3, 4, 1)).reshape(batch, 2, A_WIDTH, c.shape[1])
                   for c in (c128, c512, c2048)]
    cache_specs = [pl.BlockSpec((bb, 2, A_WIDTH, c.shape[-1]), lambda b: (b, 0, 0, 0))
                   for c in cache_views]
    out = pl.pallas_call(
        functools.partial(_attn_sample_kernel, t_new=t_new, bb=bb),
        grid=(batch // bb,),
        in_specs=[tok_spec, tok_spec, tok_spec] + cache_specs,
        out_specs=pl.BlockSpec((bb, t_new, A_WIDTH), lambda b: (b, 0, 0)),
        out_shape=jax.ShapeDtypeStruct((batch, t_new, A_WIDTH), BF16),
        compiler_params=_params(1),
        name="attn_sample",
    )(qa.reshape(batch, t_new, w3), ka.reshape(batch, t_new, w3), va.reshape(batch, t_new, w3),
      *cache_views)
    return out.reshape(batch * t_new, A_WIDTH)


def _split2(x):
    hi = x.astype(BF16)
    return hi, (x - hi.astype(F32)).astype(BF16)


def _gla_out(o, rg, gain):
    on = o * lax.rsqrt(jnp.mean(o * o, axis=-1, keepdims=True) + EPS) * gain
    return (on * rg * jax.nn.sigmoid(rg)).astype(BF16)


def _col_bcast(vec_row, width):
    sq = jnp.broadcast_to(vec_row, (LANES, LANES)).T
    return jnp.concatenate([sq] * (width // LANES), axis=1)


GLA_CHUNK = 1024
GLA_SEQS = 2


def _gla_prompt_kernel(q_ref, k_ref, v_ref, la_ref, rg_ref, gn_ref, o_ref, s_ref, *, n_blocks):
    cb, sub = GLA_BLOCK, GLA_SUB
    n_sub = cb // sub
    n_seq = q_ref.shape[0]

    @pl.when(pl.program_id(1) == 0)
    def _():
        s_ref[...] = jnp.zeros_like(s_ref)

    jobs = [(n, h) for n in range(n_seq) for h in range(G_HEADS)]

    def hk(x, h):
        return x[:, h * G_DK:(h + 1) * G_DK]

    def decays(n, rows, tri, rowk):
        hi, lo = _split2(la_ref[n, rows, :])
        b = _dot(tri, hi) + _dot(tri, lo)
        q = q_ref[n, rows, :].astype(F32)
        k = k_ref[n, rows, :].astype(F32)
        b_last = b[cb - 1:cb, :]
        qe = (q * jnp.exp(b)).astype(BF16)
        sub_end = [b[(j + 1) * sub - 1:(j + 1) * sub, :] for j in range(n_sub)]
        own_end = jnp.concatenate(
            [jnp.broadcast_to(e, (sub, G_KW)) for e in sub_end], axis=0)
        kk = k * jnp.exp(own_end - b)
        q_parts, k_parts = [], []
        for j in range(n_sub):
            lo_row = j * sub
            qj = q[lo_row:, :] * jnp.exp(b[lo_row:, :] - sub_end[j])
            q_parts.append(jnp.concatenate(
                [jnp.zeros((lo_row, G_KW), F32), qj], axis=0).astype(BF16) if lo_row else
                qj.astype(BF16))
            in_sub = (rowk >= lo_row) & (rowk < lo_row + sub)
            k_parts.append(jnp.where(in_sub, kk, 0.0).astype(BF16))
        return qe, q_parts, k_parts, k * jnp.exp(b_last - b), jnp.exp(b_last)

    def step(i, carry):
        rows = pl.ds(pl.multiple_of(i * cb, cb), cb)
        row = lax.broadcasted_iota(jnp.int32, (cb, cb), 0)
        col = lax.broadcasted_iota(jnp.int32, (cb, cb), 1)
        tri = (row >= col).astype(F32).astype(BF16)
        rowk = lax.broadcasted_iota(jnp.int32, (cb, G_KW), 0)

        per_seq = [decays(n, rows, tri, rowk) for n in range(n_seq)]
        states = [s_ref[n, h] for n, h in jobs]
        vals = [v_ref[n, rows, h * G_DV:(h + 1) * G_DV] for n, h in jobs]
        inter = [_dot(hk(per_seq[n][0], h), s.astype(BF16)) for s, (n, h) in zip(states, jobs)]
        atts = [_dot_nt(jnp.concatenate([hk(p, h) for p in per_seq[n][1]], axis=1),
                        jnp.concatenate([hk(p, h) for p in per_seq[n][2]], axis=1))
                for n, h in jobs]
        atts = [jnp.where(col <= row, a, 0.0).astype(BF16) for a in atts]
        outs = [x + _dot(a, v) for x, a, v in zip(inter, atts, vals)]
        for s, v, (n, h) in zip(states, vals, jobs):
            kd_t = hk(per_seq[n][3], h).T.astype(BF16)
            s_ref[n, h] = s * _col_bcast(hk(per_seq[n][4], h), G_DV) + _dot(kd_t, v)
        for o, (n, h) in zip(outs, jobs):
            vs = slice(h * G_DV, (h + 1) * G_DV)
            o_ref[n, rows, vs] = _gla_out(o, rg_ref[n, rows, vs].astype(F32), gn_ref[...])
        return carry

    lax.fori_loop(0, n_blocks, step, 0)


def _gla_prompt(qg, kg, vg, la, rg, gn, batch, seq):
    assert seq % GLA_CHUNK == 0 and batch % GLA_SEQS == 0
    chunks = seq // GLA_CHUNK

    def spec(width):
        return pl.BlockSpec((GLA_SEQS, GLA_CHUNK, width), lambda b, c: (b, c, 0))

    def per_seq(x):
        return x.reshape(batch, seq, x.shape[-1])
    o, s = pl.pallas_call(
        functools.partial(_gla_prompt_kernel, n_blocks=GLA_CHUNK // GLA_BLOCK),
        grid=(batch // GLA_SEQS, chunks),
        in_specs=[spec(G_KW), spec(G_KW), spec(G_VW), spec(G_KW), spec(G_VW),
                  _resident((1, G_DV))],
        out_specs=[spec(G_VW),
                   pl.BlockSpec((GLA_SEQS, G_HEADS, G_DK, G_DV), lambda b, c: (b, 0, 0, 0))],
        out_shape=[jax.ShapeDtypeStruct((batch, seq, G_VW), BF16),
                   jax.ShapeDtypeStruct((batch, G_HEADS, G_DK, G_DV), F32)],
        compiler_params=_params(2),
        name="gla_prompt",
    )(per_seq(qg), per_seq(kg), per_seq(vg), per_seq(la), per_seq(rg), gn)
    return o.reshape(batch * seq, G_VW), s


def _gla_sample_kernel(q_ref, k_ref, v_ref, la_ref, rg_ref, gn_ref, s0_ref, o_ref, s_ref,
                       *, bb, t_new):
    per_body = 2

    def hk(x, h):
        return x[:, h * G_DK:(h + 1) * G_DK]

    def hv(x, h):
        return x[:, h * G_DV:(h + 1) * G_DV]

    def body(i0, carry):
        trow = lax.broadcasted_iota(jnp.int32, (t_new, G_KW), 0)
        arow = lax.broadcasted_iota(jnp.int32, (t_new, LANES), 0)
        acol = lax.broadcasted_iota(jnp.int32, (t_new, LANES), 1)
        zk = jnp.zeros((LANES - t_new, G_KW), F32)
        zv = jnp.zeros((LANES - t_new, G_VW), F32)
        seqs = [per_body * i0 + j for j in range(per_body)]
        jobs = [(n, h) for n in range(per_body) for h in range(G_HEADS)]

        bs = []
        for i in seqs:
            b = la_ref[i]
            shift = 1
            while shift < t_new:
                b = b + jnp.where(trow >= shift, pltpu.roll(b, shift, 0), 0.0)
                shift *= 2
            bs.append(b)
        qs = [q_ref[i].astype(F32) for i in seqs]
        ks = [k_ref[i].astype(F32) for i in seqs]
        vs = [jnp.concatenate([v_ref[i].astype(F32), zv], axis=0).astype(BF16) for i in seqs]
        b_last = [b[t_new - 1:t_new, :] for b in bs]
        qe = [(q * jnp.exp(b)).astype(BF16) for q, b in zip(qs, bs)]
        ke = [jnp.concatenate([k * jnp.exp(-b), zk], axis=0).astype(BF16) for k, b in zip(ks, bs)]
        kd = [jnp.concatenate([k * jnp.exp(bl - b), zk], axis=0)
              for k, b, bl in zip(ks, bs, b_last)]
        decay = [jnp.exp(bl) for bl in b_last]

        s0 = [s0_ref[seqs[n], h] for n, h in jobs]
        att = [jnp.where(acol <= arow, _dot_nt(hk(qe[n], h), hk(ke[n], h)), 0.0).astype(BF16)
               for n, h in jobs]
        out = [_dot(a, hv(vs[n], h)) + _dot(hk(qe[n], h), s.astype(BF16))
               for a, s, (n, h) in zip(att, s0, jobs)]
        for s, (n, h) in zip(s0, jobs):
            s_ref[seqs[n], h] = (s * _col_bcast(hk(decay[n], h), G_DV)
                                 + _dot(hk(kd[n], h).T.astype(BF16), hv(vs[n], h)))
        for o, (n, h) in zip(out, jobs):
            sl = slice(h * G_DV, (h + 1) * G_DV)
            o_ref[seqs[n], :, sl] = _gla_out(o, rg_ref[seqs[n], :, sl].astype(F32), gn_ref[...])
        return carry

    assert bb % per_body == 0
    lax.fori_loop(0, bb // per_body, body, 0)


def _gla_sample(qg, kg, vg, la, rg, gn, s0, batch, t_new, bb):
    def spec(width):
        return pl.BlockSpec((bb, t_new, width), lambda b: (b, 0, 0))
    st_spec = pl.BlockSpec((bb, G_HEADS, G_DK, G_DV), lambda b: (b, 0, 0, 0))
    o, s = pl.pallas_call(
        functools.partial(_gla_sample_kernel, bb=bb, t_new=t_new),
        grid=(batch // bb,),
        in_specs=[spec(G_KW), spec(G_KW), spec(G_VW), spec(G_KW), spec(G_VW),
                  _resident((1, G_DV)), st_spec],
        out_specs=[spec(G_VW), st_spec],
        out_shape=[jax.ShapeDtypeStruct((batch, t_new, G_VW), BF16),
                   jax.ShapeDtypeStruct((batch, G_HEADS, G_DK, G_DV), F32)],
        compiler_params=_params(1),
        name="gla_sample",
    )(qg.reshape(batch, t_new, G_KW), kg.reshape(batch, t_new, G_KW),
      vg.reshape(batch, t_new, G_VW), la.reshape(batch, t_new, G_KW),
      rg.reshape(batch, t_new, G_VW), gn, s0)
    return o.reshape(batch * t_new, G_VW), s


def _out_kernel(x_ref, a_ref, gb_ref, gate_ref, wa_ref, wb_ref, wo_ref, g_ref, wg_ref, wu_ref,
                wd_ref, o_ref, act_ref, between=None):
    a_out = _dot(a_ref[...], wa_ref[...])
    b_out = _dot(gb_ref[...], wb_ref[...])
    mix = (gate_ref[:, 0:D_MODEL].astype(F32) * a_out
           + gate_ref[:, D_MODEL:2 * D_MODEL].astype(F32) * b_out)
    x2 = x_ref[...] + _dot(mix.astype(BF16), wo_ref[...])
    o_ref[...] = _swiglu_residual(x2, g_ref, wg_ref, wu_ref, wd_ref, act_ref, between)


def _out(x1, attn, gb, gates, wa, wb, wo, g, wg, wu, wd, tm, row0=0):
    t = attn.shape[0]
    assert row0 % tm == 0 and t % tm == 0
    block0 = row0 // tm

    def tok(width):
        return pl.BlockSpec((tm, width), lambda i: (i, 0))
    consts = [wa, wb, wo, g, wg, wu, wd]
    return pl.pallas_call(
        _out_kernel,
        grid=(t // tm,),
        in_specs=[pl.BlockSpec((tm, D_MODEL), lambda i: (i + block0, 0)),
                  tok(A_WIDTH), tok(G_VW), tok(2 * D_MODEL)]
        + [_resident(c.shape) for c in consts],
        out_specs=tok(D_MODEL),
        out_shape=jax.ShapeDtypeStruct((t, D_MODEL), F32),
        scratch_shapes=[pltpu.VMEM((tm, D_FF), BF16)],
        compiler_params=_params(1),
        name="out_ffn2",
    )(x1, attn, gb, gates, *consts)


def _out_with_sample_attn_kernel(*refs, t_new, bb):
    out_in, attn_in = refs[:11], refs[11:17]
    y_ref, a_ref, act_ref = refs[17:]
    stages = _attn_sample_stages(*attn_in, a_ref, t_new=t_new, bb=bb)
    every = 3

    def between(c):
        if c % every == every - 1:
            next(stages, None)

    _out_kernel(*out_in, y_ref, act_ref, between=between)
    for _ in stages:
        pass


def _out_with_sample_attn(x1, attn, gb, gates, wa, wb, wo, g, wg, wu, wd,
                          qa, ka, va, c128, c512, c2048, batch, t_new, tm):
    t = attn.shape[0]
    steps = t // tm
    assert batch % steps == 0
    bb = batch // steps
    w3 = N_GROUPS * A_WIDTH

    def tok(width):
        return pl.BlockSpec((tm, width), lambda i: (i, 0))
    consts = [wa, wb, wo, g, wg, wu, wd]
    new_spec = pl.BlockSpec((bb, t_new, w3), lambda i: (i, 0, 0))
    cache_views = [jnp.transpose(c, (0, 2, 3, 4, 1)).reshape(batch, 2, A_WIDTH, c.shape[1])
                   for c in (c128, c512, c2048)]
    cache_specs = [pl.BlockSpec((bb, 2, A_WIDTH, c.shape[-1]), lambda i: (i, 0, 0, 0))
                   for c in cache_views]
    y, a_s = pl.pallas_call(
        functools.partial(_out_with_sample_attn_kernel, t_new=t_new, bb=bb),
        grid=(steps,),
        in_specs=[tok(D_MODEL), tok(A_WIDTH), tok(G_VW), tok(2 * D_MODEL)]
        + [_resident(c.shape) for c in consts] + [new_spec] * 3 + cache_specs,
        out_specs=[tok(D_MODEL), pl.BlockSpec((bb, t_new, A_WIDTH), lambda i: (i, 0, 0))],
        out_shape=[jax.ShapeDtypeStruct((t, D_MODEL), F32),
                   jax.ShapeDtypeStruct((batch, t_new, A_WIDTH), BF16)],
        scratch_shapes=[pltpu.VMEM((tm, D_FF), BF16)],
        compiler_params=_params(1),
        name="out_ffn2_attn_sample",
    )(x1, attn, gb, gates, *consts,
      qa.reshape(batch, t_new, w3), ka.reshape(batch, t_new, w3), va.reshape(batch, t_new, w3),
      *cache_views)
    return y, a_s.reshape(batch * t_new, A_WIDTH)


def _sample_rows_kernel(k_ref, v_ref, *refs, batch, t_new):
    o_refs, slab_ref = refs[:N_GROUPS], refs[N_GROUPS]
    n_slab = N_GROUPS * A_WIDTH // LANES
    per_group = A_WIDTH // LANES
    for kv, src in enumerate((k_ref, v_ref)):
        for s in range(n_slab):
            slab_ref[kv, s] = src[:, s * LANES:(s + 1) * LANES]
    for kv in range(2):
        for s in range(n_slab):
            g, part = divmod(s, per_group)
            for t in range(t_new):
                rows = slab_ref[kv, s, pl.ds(t, batch, stride=t_new), :]
                o_refs[g][t, kv, part * LANES:(part + 1) * LANES, :] = rows.T


def _sample_rows(ka, va, batch, t_new):
    assert batch == LANES
    n_slab = N_GROUPS * A_WIDTH // LANES
    outs = pl.pallas_call(
        functools.partial(_sample_rows_kernel, batch=batch, t_new=t_new),
        out_shape=[jax.ShapeDtypeStruct((t_new, 2, A_WIDTH, batch), F32)] * N_GROUPS,
        scratch_shapes=[pltpu.VMEM((2, n_slab, batch * t_new, LANES), F32)],
        compiler_params=pltpu.CompilerParams(vmem_limit_bytes=VMEM_LIMIT),
        name="sample_rows",
    )(ka, va)
    return [jnp.transpose(o.reshape(t_new, 2, A_HEADS, A_HEAD_DIM, batch), (4, 0, 1, 2, 3))
            for o in outs]


def _position_major(rows, batch):
    window = rows.shape[-1]
    return jnp.transpose(rows.reshape(batch, 2, A_HEADS, A_HEAD_DIM, window), (0, 4, 1, 2, 3))


def kernel(x_prompt, x_sample, cache_swa128_kv, cache_swa512_kv, cache_swa2048_kv, state_gla,
           norm_ffn1, ffn1_gate, ffn1_up, ffn1_down, norm_mix, w_in, a_q_norm, a_k_norm,
           g_alpha_up, g_alpha_bias, g_out_norm, w_a_out, w_b_out, w_out,
           norm_ffn2, ffn2_gate, ffn2_up, ffn2_down):
    batch, seq, _ = x_prompt.shape
    dec_batch, dec_seq, _ = x_sample.shape

    aw = N_GROUPS * A_WIDTH
    o_g = 3 * aw
    o_lr = o_g + 2 * G_KW + 2 * G_VW
    o_gate = o_lr + G_LOWRANK
    w_attn = w_in[:, :o_g].astype(BF16)
    w_gla = w_in[:, o_g:o_lr].astype(BF16)
    w_lr = jnp.pad(w_in[:, o_lr:o_gate], ((0, 0), (0, LANES - G_LOWRANK))).astype(BF16)
    w_gate = w_in[:, o_gate:].astype(BF16)
    up = jnp.pad(g_alpha_up, ((0, LANES - G_LOWRANK), (0, 0))).astype(BF16)
    bias = g_alpha_bias.reshape(1, G_KW)
    qgain = jnp.tile(a_q_norm, (1, A_HEADS)).reshape(1, aw)
    kgain = jnp.tile(a_k_norm, (1, A_HEADS)).reshape(1, aw)
    lane = jnp.arange(A_WIDTH) // A_HEAD_DIM
    hmean = jnp.where(lane[:, None] == lane[None, :], 1.0 / A_HEAD_DIM, 0.0).astype(BF16)
    gn = g_out_norm.reshape(1, G_DV)
    ffn1 = (norm_ffn1.reshape(1, D_MODEL), ffn1_gate, ffn1_up, ffn1_down)
    ffn2 = (norm_ffn2.reshape(1, D_MODEL), ffn2_gate.astype(BF16), ffn2_up.astype(BF16),
            ffn2_down.astype(BF16))
    proj_w = (norm_mix.reshape(1, D_MODEL), w_attn, w_gla, w_lr, w_gate, up, bias,
              qgain, kgain, hmean)
    out_w = (w_a_out.astype(BF16), w_b_out.astype(BF16), w_out.astype(BF16))

    t_p, t_s = batch * seq, dec_batch * dec_seq
    x1 = _ffn(x_prompt.reshape(t_p, D_MODEL), x_sample.reshape(t_s, D_MODEL), *ffn1, tm=512)

    qa, ka, va, qg_s, kg_s, vg_s, rg_s, la_s, gates_s = _proj(
        x1, *proj_w, tm=512, rows=range(t_p, t_p + t_s))
    rows_s = _sample_rows(ka, va, dec_batch, dec_seq)

    proj_p = _proj(x1, *proj_w, tm=512, rows=range(0, t_p), seq=seq)
    qkv_units, rows_p, (qg, kg, vg, rg, la, gates) = proj_p[:9], proj_p[9:12], proj_p[12:]
    a_p = _attn_prompt(qkv_units, batch, seq)
    gbp, state_p = _gla_prompt(qg, kg, vg, la, rg, gn, batch, seq)
    y_prompt, a_s = _out_with_sample_attn(
        x1, a_p, gbp, gates, *out_w, *ffn2, qa, ka, va,
        cache_swa128_kv, cache_swa512_kv, cache_swa2048_kv, dec_batch, dec_seq, tm=256)
    y_prompt = y_prompt.reshape(batch, seq, D_MODEL)
    rows_p = [_position_major(r, batch) for r in rows_p]

    gbs, state_s = _gla_sample(qg_s, kg_s, vg_s, la_s, rg_s, gn, state_gla, dec_batch, dec_seq,
                               bb=8)
    y_sample = _out(x1, a_s, gbs, gates_s, *out_w, *ffn2, tm=512, row0=t_p).reshape(
        dec_batch, dec_seq, D_MODEL)

    return (y_prompt, y_sample, rows_p[0], rows_p[1], rows_p[2], state_p,
            rows_s[0], rows_s[1], rows_s[2], state_s)
```

```python
import functools

import jax
import jax.numpy as jnp
from jax import lax
from jax.experimental import pallas as pl
from jax.experimental.pallas import tpu as pltpu

F32 = jnp.float32
BF16 = jnp.bfloat16

D_MODEL = 1024
D_FF = 2816
SWA_DILATIONS = (1, 4, 16)
N_GROUPS = 3
A_HEADS = 4
A_HEAD_DIM = 64
A_WIDTH = A_HEADS * A_HEAD_DIM
BAND = 128
G_HEADS = 4
G_DK = 128
G_DV = 256
G_KW = G_HEADS * G_DK
G_VW = G_HEADS * G_DV
G_LOWRANK = 16
G_TAU = 16.0
EPS = 1e-6
W_IN_GLA = 3 * N_GROUPS * A_WIDTH
W_IN_LR = W_IN_GLA + 2 * G_KW + 2 * G_VW
W_IN_GATE = W_IN_LR + G_LOWRANK
D_IN = W_IN_GATE + 2 * D_MODEL

LANES = 128
GLA_BLOCK = 128
GLA_SUB = 32
FF_CHUNK = 256
N_CHUNK = 512
VMEM_LIMIT = 58 * 1024 * 1024


def _resident(shape):
    nd = len(shape)
    return pl.BlockSpec(shape, lambda *_: (0,) * nd, pipeline_mode=pl.Buffered(1))


def _params(n_axes):
    return pltpu.CompilerParams(
        dimension_semantics=("arbitrary",) * n_axes, vmem_limit_bytes=VMEM_LIMIT)


def _dot(a, b):
    return jnp.dot(a, b, preferred_element_type=F32)


def _dot_nt(a, b):
    return lax.dot_general(a, b, (((1,), (1,)), ((), ())), preferred_element_type=F32)


def _rmsnorm(x, g):
    return x * lax.rsqrt(jnp.mean(x * x, axis=-1, keepdims=True) + EPS) * g


def _swiglu_residual(x, g_ref, wg_ref, wu_ref, wd_ref, act_ref, between=None):
    h = _rmsnorm(x, g_ref[...]).astype(BF16)
    for c in range(D_FF // FF_CHUNK):
        sl = slice(c * FF_CHUNK, (c + 1) * FF_CHUNK)
        gate = _dot(h, wg_ref[:, sl].astype(BF16))
        up = _dot(h, wu_ref[:, sl].astype(BF16))
        act_ref[:, sl] = (gate * jax.nn.sigmoid(gate) * up).astype(BF16)
        if between is not None:
            between(c)
    return x + 0.5 * _dot(act_ref[...], wd_ref[...].astype(BF16))


def _ffn_kernel(xa_ref, xb_ref, g_ref, wg_ref, wu_ref, wd_ref, o_ref, act_ref, *, steps_a):
    x = jnp.where(pl.program_id(0) < steps_a, xa_ref[...], xb_ref[...])
    o_ref[...] = _swiglu_residual(x, g_ref, wg_ref, wu_ref, wd_ref, act_ref)


def _ffn(xa, xb, g, wg, wu, wd, tm):
    steps_a, steps_b = xa.shape[0] // tm, xb.shape[0] // tm
    assert xa.shape[0] % tm == 0 and xb.shape[0] % tm == 0
    return pl.pallas_call(
        functools.partial(_ffn_kernel, steps_a=steps_a),
        grid=(steps_a + steps_b,),
        in_specs=[
            pl.BlockSpec((tm, D_MODEL), lambda i: (jnp.minimum(i, steps_a - 1), 0)),
            pl.BlockSpec((tm, D_MODEL), lambda i: (jnp.maximum(i - steps_a, 0), 0)),
            _resident((1, D_MODEL)),
            _resident((D_MODEL, D_FF)),
            _resident((D_MODEL, D_FF)),
            _resident((D_FF, D_MODEL)),
        ],
        out_specs=pl.BlockSpec((tm, D_MODEL), lambda i: (i, 0)),
        out_shape=jax.ShapeDtypeStruct((xa.shape[0] + xb.shape[0], D_MODEL), F32),
        scratch_shapes=[pltpu.VMEM((tm, D_FF), BF16)],
        compiler_params=_params(1),
        name="ffn1",
    )(xa, xb, g, wg, wu, wd)


def _log_sigmoid(z):
    return jnp.minimum(z, 0.0) - jnp.log1p(jnp.exp(-jnp.abs(z)))


def _store_blocked(x, d, out_ref, perm_ref):
    tm = x.shape[0]
    if d == 1:
        for j in range(tm // BAND):
            out_ref[j] = x[j * BAND:(j + 1) * BAND].astype(BF16)
        return
    for s in range(A_WIDTH // LANES):
        perm_ref[s] = x[:, s * LANES:(s + 1) * LANES]
    for r in range(d):
        rows = jnp.concatenate(
            [perm_ref[s, pl.ds(r, tm // d, stride=d), :] for s in range(A_WIDTH // LANES)], axis=1)
        out_ref[r] = rows.astype(BF16)


def _proj_kernel(*refs, blocked, tiles_per_seq):
    x_ref, g_ref, w_ref, up_ref, bias_ref, qgain_ref, kgain_ref, hmean_ref = refs[:8]
    outs = refs[8:]
    if blocked:
        qb_refs, kb_refs, vb_refs, row_refs = outs[0:3], outs[3:6], outs[6:9], outs[9:12]
        qg_ref, kg_ref, vg_ref, rg_ref, la_ref, gate_ref, perm_ref = outs[12:]
        tm = x_ref.shape[0]
    else:
        qa_ref, ka_ref, va_ref, qg_ref, kg_ref, vg_ref, rg_ref, la_ref, gate_ref = outs

    h = _rmsnorm(x_ref[...], g_ref[...]).astype(BF16)
    hmean = hmean_ref[...]

    aw = N_GROUPS * A_WIDTH
    groups = range(N_GROUPS)

    def proj(lo, width):
        return _dot_nt(h, w_ref[lo:lo + width, :])

    q_raw = [proj(g * A_WIDTH, A_WIDTH) for g in groups]
    k_raw = [proj(aw + g * A_WIDTH, A_WIDTH) for g in groups]
    vals = [proj(2 * aw + g * A_WIDTH, A_WIDTH) for g in groups]

    qg_ref[...] = (proj(W_IN_GLA, G_KW) * (G_DK ** -0.5)).astype(BF16)
    kg_ref[...] = proj(W_IN_GLA + G_KW, G_KW).astype(BF16)

    q_ms = [_dot((x * x).astype(BF16), hmean) for x in q_raw]
    k_ms = [_dot((x * x).astype(BF16), hmean) for x in k_raw]

    def attn_epilogue(g):
        d = SWA_DILATIONS[g]
        sl = slice(g * A_WIDTH, (g + 1) * A_WIDTH)
        q = q_raw[g] * lax.rsqrt(q_ms[g] + EPS) * qgain_ref[:, sl] * (A_HEAD_DIM ** -0.5)
        k = k_raw[g] * lax.rsqrt(k_ms[g] + EPS) * kgain_ref[:, sl]
        v = vals[g]
        if blocked:
            for i, (x, o_refs) in enumerate(((q, qb_refs), (k, kb_refs), (v, vb_refs))):
                _store_blocked(x, d, o_refs[g], perm_ref.at[(g - 1) * 3 + i] if d > 1 else None)
            keep = min(BAND * d, tm)
            row_refs[g][0] = k[tm - keep:, :].T
            row_refs[g][1] = v[tm - keep:, :].T
        else:
            ka_ref[:, sl] = k
            va_ref[:, sl] = v
            qa_ref[:, sl] = q.astype(BF16)

    n_gate = 2 * D_MODEL // N_CHUNK
    for c in range(n_gate):
        sl = slice(c * N_CHUNK, (c + 1) * N_CHUNK)
        gate_ref[:, sl] = jax.nn.sigmoid(proj(W_IN_GATE + c * N_CHUNK, N_CHUNK)).astype(BF16)
        if c < N_GROUPS:
            attn_epilogue(c)
    assert n_gate >= N_GROUPS

    for c in range(G_VW // N_CHUNK):
        sl = slice(c * N_CHUNK, (c + 1) * N_CHUNK)
        off = W_IN_GLA + 2 * G_KW + c * N_CHUNK
        vg_ref[:, sl] = proj(off, N_CHUNK).astype(BF16)
        rg_ref[:, sl] = proj(G_VW + off, N_CHUNK).astype(BF16)

    lr = proj(W_IN_LR, LANES).astype(BF16)
    z = _dot(lr, up_ref[...]) + bias_ref[...]
    la_ref[...] = _log_sigmoid(z) * (1.0 / G_TAU)


def _unit_spec(d, tm):
    per_class = tm // d
    if per_class >= BAND:
        return pl.BlockSpec((tm // BAND, BAND, A_WIDTH), lambda i: (i, 0, 0))
    parts = BAND // per_class
    return pl.BlockSpec((d, per_class, A_WIDTH), lambda i: (i // parts, i % parts, 0))


def _row_spec(d, tm, tiles_per_seq):
    window = BAND * d
    if window <= tm:
        return pl.BlockSpec((None, 2, A_WIDTH, window), lambda i: (i // tiles_per_seq, 0, 0, 0))
    first_tile = tiles_per_seq - window // tm
    return pl.BlockSpec(
        (None, 2, A_WIDTH, tm),
        lambda i: (i // tiles_per_seq, 0, 0, jnp.maximum(i % tiles_per_seq - first_tile, 0)))


def _proj(x1, g, w_in_t, up, bias, qgain, kgain, hmean, tm, rows, seq=None):
    t = rows.stop - rows.start
    assert rows.start % tm == 0 and t % tm == 0
    block0 = rows.start // tm
    aw = N_GROUPS * A_WIDTH
    blocked = seq is not None
    widths = [(G_KW, BF16), (G_KW, BF16), (G_VW, BF16), (G_VW, BF16), (G_KW, F32),
              (2 * D_MODEL, BF16)]
    scratch = []
    tiles_per_seq = None
    if blocked:
        assert tm == 4 * BAND and SWA_DILATIONS == (1, 4, 16)
        assert seq % tm == 0 and BAND * SWA_DILATIONS[-1] <= seq
        tiles_per_seq = seq // tm
        out_specs = ([_unit_spec(d, tm) for _ in range(3) for d in SWA_DILATIONS]
                     + [_row_spec(d, tm, tiles_per_seq) for d in SWA_DILATIONS])
        out_shape = ([jax.ShapeDtypeStruct((t // BAND, BAND, A_WIDTH), BF16)] * 9
                     + [jax.ShapeDtypeStruct((t // seq, 2, A_WIDTH, BAND * d), F32)
                        for d in SWA_DILATIONS])
        scratch = [pltpu.VMEM((3 * (N_GROUPS - 1), A_WIDTH // LANES, tm, LANES), F32)]
    else:
        widths = [(aw, BF16), (aw, F32), (aw, F32)] + widths
        out_specs, out_shape = [], []
    out_specs = out_specs + [pl.BlockSpec((tm, w), lambda i: (i, 0)) for w, _ in widths]
    out_shape = out_shape + [jax.ShapeDtypeStruct((t, w), dt) for w, dt in widths]
    assert w_in_t.shape == (D_IN, D_MODEL) and W_IN_GATE % 16 == 0
    consts = [g, w_in_t, up, bias, qgain, kgain, hmean]
    return pl.pallas_call(
        functools.partial(_proj_kernel, blocked=blocked, tiles_per_seq=tiles_per_seq),
        grid=(t // tm,),
        in_specs=[pl.BlockSpec((tm, D_MODEL), lambda i: (i + block0, 0))]
        + [_resident(c.shape) for c in consts],
        out_specs=out_specs,
        out_shape=out_shape,
        scratch_shapes=scratch,
        compiler_params=_params(1),
        name="in_proj",
    )(x1, *consts)


def _log2(n):
    assert n > 0 and n & (n - 1) == 0, n
    return n.bit_length() - 1


def _head_masks(rows):
    lane = lax.broadcasted_iota(jnp.int32, (rows, A_WIDTH), 1)
    return [(lane >> _log2(A_HEAD_DIM)) == h for h in range(A_HEADS)]


ATTN_SPAN = 2048
ATTN_UNROLL = 2


def _attn_prompt_kernel(*refs, units):
    q_refs, k_refs, v_refs = refs[0:3], refs[3:6], refs[6:9]
    o_ref, osc_ref, lsc_ref, bias_ref = refs[9:]
    span = pl.program_id(1)
    n_slab = A_WIDTH // LANES

    row = lax.broadcasted_iota(jnp.int32, (A_HEADS * BAND, 2 * BAND), 0) & (BAND - 1)
    col = lax.broadcasted_iota(jnp.int32, (A_HEADS * BAND, 2 * BAND), 1)
    own = (col >= BAND) & (col - BAND <= row)
    bias_ref[0] = jnp.where(own, 0.0, -jnp.inf)
    bias_ref[1] = jnp.where(own | ((col < BAND) & (col >= row)), 0.0, -jnp.inf)

    for g, d in enumerate(SWA_DILATIONS):
        q_ref, k_ref, v_ref = q_refs[g], k_refs[g], v_refs[g]

        def scores(ul, d=d, q_ref=q_ref, k_ref=k_ref):
            u = span * units + ul
            up = jnp.maximum(u - d, 0)
            kb = jnp.concatenate([k_ref[up], k_ref[u]], axis=0)
            q = q_ref[ul]
            qm = jnp.concatenate(
                [q * m.astype(F32).astype(BF16) for m in _head_masks(BAND)], axis=0)
            has_prev = jnp.where(u >= d, 1, 0)
            return _dot_nt(qm, kb) + bias_ref[has_prev]

        def softmax(s):
            m = jnp.max(s, axis=-1, keepdims=True)
            p = jnp.exp(s - m)
            den = jnp.sum(p, axis=-1, keepdims=True)
            return p.astype(BF16), 1.0 / den, m + jnp.log(den)

        def values(ul, p, d=d, v_ref=v_ref):
            u = span * units + ul
            vb = jnp.concatenate([v_ref[jnp.maximum(u - d, 0)], v_ref[u]], axis=0)
            return _dot(p, vb)

        def store(ul, o_all, inv_den, lse_all, d=d, g=g):
            hmask = _head_masks(BAND)
            o_all = o_all * inv_den
            o_acc = jnp.zeros((BAND, A_WIDTH), F32)
            lse_acc = jnp.zeros((BAND, A_WIDTH), F32)
            for h in range(A_HEADS):
                rs = slice(h * BAND, (h + 1) * BAND)
                o_acc = jnp.where(hmask[h], o_all[rs], o_acc)
                lse_acc = jnp.where(hmask[h], lse_all[rs], lse_acc)
            start = (ul >> _log2(d)) * (BAND * d) + (ul & (d - 1))
            if d == 1:
                rows = pl.ds(pl.multiple_of(start, BAND), BAND)
            else:
                rows = pl.ds(start, BAND, stride=d)
            for sl in range(n_slab):
                osc_ref[g, sl, rows, :] = o_acc[:, sl * LANES:(sl + 1) * LANES]
                lsc_ref[g, sl, rows, :] = lse_acc[:, sl * LANES:(sl + 1) * LANES]

        def unit_group(i, carry, scores=scores, softmax=softmax, values=values, store=store):
            uls = [ATTN_UNROLL * i + j for j in range(ATTN_UNROLL)]
            ss = [scores(ul) for ul in uls]
            sm = [softmax(s) for s in ss]
            os_ = [values(ul, p) for ul, (p, _, _) in zip(uls, sm)]
            for ul, o_all, (_, inv_den, lse_all) in zip(uls, os_, sm):
                store(ul, o_all, inv_den, lse_all)
            return carry

        assert units % ATTN_UNROLL == 0
        lax.fori_loop(0, units // ATTN_UNROLL, unit_group, 0)

    def merge(i, carry):
        rows = pl.ds(pl.multiple_of(i * BAND, BAND), BAND)

        def natural(ref, g):
            return jnp.concatenate([ref[g, sl, rows, :] for sl in range(n_slab)], axis=1)

        outs = [natural(osc_ref, g) for g in range(N_GROUPS)]
        lses = [natural(lsc_ref, g) for g in range(N_GROUPS)]
        m = functools.reduce(jnp.maximum, lses)
        ws = [jnp.exp(l - m) for l in lses]
        a = functools.reduce(jnp.add, [w * o for w, o in zip(ws, outs)])
        o_ref[rows, :] = (a * (1.0 / functools.reduce(jnp.add, ws))).astype(BF16)
        return carry

    lax.fori_loop(0, ATTN_SPAN // BAND, merge, 0)


def _attn_prompt(qkv_units, batch, seq):
    assert seq % ATTN_SPAN == 0
    units = ATTN_SPAN // BAND
    per_seq = seq // BAND
    spans = seq // ATTN_SPAN
    q_spec = pl.BlockSpec((units, BAND, A_WIDTH), lambda b, s: (b * spans + s, 0, 0))
    kv_spec = pl.BlockSpec((per_seq, BAND, A_WIDTH), lambda b, s: (b, 0, 0))
    return pl.pallas_call(
        functools.partial(_attn_prompt_kernel, units=units),
        grid=(batch, spans),
        in_specs=[q_spec] * 3 + [kv_spec] * 6,
        out_specs=pl.BlockSpec((ATTN_SPAN, A_WIDTH), lambda b, s: (b * spans + s, 0)),
        out_shape=jax.ShapeDtypeStruct((batch * seq, A_WIDTH), BF16),
        scratch_shapes=[pltpu.VMEM((N_GROUPS, A_WIDTH // LANES, ATTN_SPAN, LANES), F32)] * 2
        + [pltpu.VMEM((2, A_HEADS * BAND, 2 * BAND), F32)],
        compiler_params=_params(2),
        name="attn_prompt",
    )(*qkv_units)


def _attn_sample_kernel(*refs, t_new, bb):
    for _ in _attn_sample_stages(*refs, t_new=t_new, bb=bb):
        pass


def _attn_sample_stages(q_ref, k_ref, v_ref, c1_ref, c4_ref, c16_ref, o_ref, *, t_new, bb):
    hmask = _head_masks(t_new)
    rows = A_HEADS * t_new
    row_t = lax.broadcasted_iota(jnp.int32, (rows, LANES), 0) & (t_new - 1)
    col = lax.broadcasted_iota(jnp.int32, (rows, LANES), 1)
    pad = jnp.zeros((LANES - t_new, A_WIDTH), F32)
    caches = (c1_ref, c4_ref, c16_ref)
    jobs = [(i, g) for i in range(bb) for g in range(N_GROUPS)]

    ok_old, ok_new = [], []
    for d, c_ref in zip(SWA_DILATIONS, caches):
        window = c_ref.shape[-1]
        assert window == BAND * d, (window, d)
        pos_t = lax.broadcasted_iota(jnp.int32, (rows, window), 0) & (t_new - 1)
        pos = lax.broadcasted_iota(jnp.int32, (rows, window), 1)
        ok_old.append((pos >= pos_t) & (((pos - pos_t) & (d - 1)) == 0))
        ok_new.append((col <= row_t) & (((row_t - col) & (d - 1)) == 0))

    def new_rows(ref, i, g):
        sl = slice(g * A_WIDTH, (g + 1) * A_WIDTH)
        return jnp.concatenate([ref[i, :, sl], pad], axis=0).astype(BF16)

    def masked_q(i, g):
        q = q_ref[i, :, g * A_WIDTH:(g + 1) * A_WIDTH].astype(F32)
        return jnp.concatenate([q * m.astype(F32) for m in hmask], axis=0).astype(BF16)

    qms = [masked_q(i, g) for i, g in jobs]
    s_old = [jnp.where(ok_old[g], _dot(qm, caches[g][i, 0].astype(BF16)), -jnp.inf)
             for qm, (i, g) in zip(qms, jobs)]
    s_new = [jnp.where(ok_new[g], _dot_nt(qm, new_rows(k_ref, i, g)), -jnp.inf)
             for qm, (i, g) in zip(qms, jobs)]
    yield
    ms = [jnp.maximum(jnp.max(so, axis=-1, keepdims=True), jnp.max(sn, axis=-1, keepdims=True))
          for so, sn in zip(s_old, s_new)]
    p_old = [jnp.exp(so - m) for so, m in zip(s_old, ms)]
    p_new = [jnp.exp(sn - m) for sn, m in zip(s_new, ms)]
    dens = [jnp.sum(po, axis=-1, keepdims=True) + jnp.sum(pn, axis=-1, keepdims=True)
            for po, pn in zip(p_old, p_new)]
    yield
    accs = [_dot_nt(po.astype(BF16), caches[g][i, 1].astype(BF16))
            + _dot(pn.astype(BF16), new_rows(v_ref, i, g))
            for po, pn, (i, g) in zip(p_old, p_new, jobs)]
    yield

    for i in range(bb):
        num = jnp.zeros((t_new, A_WIDTH), F32)
        den_all = jnp.zeros((t_new, A_WIDTH), F32)
        m_all = jnp.full((t_new, A_WIDTH), -jnp.inf, F32)
        for g in range(N_GROUPS):
            j = jobs.index((i, g))
            o_g = jnp.zeros((t_new, A_WIDTH), F32)
            lse_g = jnp.zeros((t_new, A_WIDTH), F32)
            for h in range(A_HEADS):
                rs = slice(h * t_new, (h + 1) * t_new)
                o_g = jnp.where(hmask[h], accs[j][rs] * (1.0 / dens[j][rs]), o_g)
                lse_g = jnp.where(hmask[h], ms[j][rs] + jnp.log(dens[j][rs]), lse_g)
            m_next = jnp.maximum(m_all, lse_g)
            scale_old = jnp.exp(m_all - m_next)
            w = jnp.exp(lse_g - m_next)
            num = num * scale_old + w * o_g
            den_all = den_all * scale_old + w
            m_all = m_next
        o_ref[i] = (num * (1.0 / den_all)).astype(BF16)


def _attn_sample(qa, ka, va, c128, c512, c2048, batch, t_new, bb):
    w3 = N_GROUPS * A_WIDTH
    tok_spec = pl.BlockSpec((bb, t_new, w3), lambda b: (b, 0, 0))
    cache_views = [jnp.transpose(c, (0, 2, 3, 4, 1)).reshape(batch, 2, A_WIDTH, c.shape[1])
                   for c in (c128, c512, c2048)]
    cache_specs = [pl.BlockSpec((bb, 2, A_WIDTH, c.shape[-1]), lambda b: (b, 0, 0, 0))
                   for c in cache_views]
    out = pl.pallas_call(
        functools.partial(_attn_sample_kernel, t_new=t_new, bb=bb),
        grid=(batch // bb,),
        in_specs=[tok_spec, tok_spec, tok_spec] + cache_specs,
        out_specs=pl.BlockSpec((bb, t_new, A_WIDTH), lambda b: (b, 0, 0)),
        out_shape=jax.ShapeDtypeStruct((batch, t_new, A_WIDTH), BF16),
        compiler_params=_params(1),
        name="attn_sample",
    )(qa.reshape(batch, t_new, w3), ka.reshape(batch, t_new, w3), va.reshape(batch, t_new, w3),
      *cache_views)
    return out.reshape(batch * t_new, A_WIDTH)


def _split2(x):
    hi = x.astype(BF16)
    return hi, (x - hi.astype(F32)).astype(BF16)


def _gla_out(o, rg, gain):
    on = o * lax.rsqrt(jnp.mean(o * o, axis=-1, keepdims=True) + EPS) * gain
    return (on * rg * jax.nn.sigmoid(rg)).astype(BF16)


def _col_bcast(vec_row, width):
    sq = jnp.broadcast_to(vec_row, (LANES, LANES)).T
    return jnp.concatenate([sq] * (width // LANES), axis=1)


GLA_CHUNK = 512
GLA_SEQS = 4


def _gla_prompt_kernel(q_ref, k_ref, v_ref, la_ref, rg_ref, gn_ref, o_ref, s_ref, *, n_blocks):
    cb, sub = GLA_BLOCK, GLA_SUB
    n_sub = cb // sub
    n_seq = q_ref.shape[0]

    @pl.when(pl.program_id(1) == 0)
    def _():
        s_ref[...] = jnp.zeros_like(s_ref)

    jobs = [(n, h) for n in range(n_seq) for h in range(G_HEADS)]

    def hk(x, h):
        return x[:, h * G_DK:(h + 1) * G_DK]

    def decays(n, rows, tri, rowk):
        hi, lo = _split2(la_ref[n, rows, :])
        b = _dot(tri, hi) + _dot(tri, lo)
        q = q_ref[n, rows, :].astype(F32)
        k = k_ref[n, rows, :].astype(F32)
        b_last = b[cb - 1:cb, :]
        qe = (q * jnp.exp(b)).astype(BF16)
        sub_end = [b[(j + 1) * sub - 1:(j + 1) * sub, :] for j in range(n_sub)]
        own_end = jnp.concatenate(
            [jnp.broadcast_to(e, (sub, G_KW)) for e in sub_end], axis=0)
        kk = k * jnp.exp(own_end - b)
        q_parts, k_parts = [], []
        for j in range(n_sub):
            lo_row = j * sub
            qj = q[lo_row:, :] * jnp.exp(b[lo_row:, :] - sub_end[j])
            q_parts.append(jnp.concatenate(
                [jnp.zeros((lo_row, G_KW), F32), qj], axis=0).astype(BF16) if lo_row else
                qj.astype(BF16))
            in_sub = (rowk >= lo_row) & (rowk < lo_row + sub)
            k_parts.append(jnp.where(in_sub, kk, 0.0).astype(BF16))
        return qe, q_parts, k_parts, k * jnp.exp(b_last - b), jnp.exp(b_last)

    def step(i, carry):
        rows = pl.ds(pl.multiple_of(i * cb, cb), cb)
        row = lax.broadcasted_iota(jnp.int32, (cb, cb), 0)
        col = lax.broadcasted_iota(jnp.int32, (cb, cb), 1)
        tri = (row >= col).astype(F32).astype(BF16)
        rowk = lax.broadcasted_iota(jnp.int32, (cb, G_KW), 0)

        per_seq = [decays(n, rows, tri, rowk) for n in range(n_seq)]
        states = [s_ref[n, h] for n, h in jobs]
        vals = [v_ref[n, rows, h * G_DV:(h + 1) * G_DV] for n, h in jobs]
        inter = [_dot(hk(per_seq[n][0], h), s.astype(BF16)) for s, (n, h) in zip(states, jobs)]
        atts = [_dot_nt(jnp.concatenate([hk(p, h) for p in per_seq[n][1]], axis=1),
                        jnp.concatenate([hk(p, h) for p in per_seq[n][2]], axis=1))
                for n, h in jobs]
        atts = [jnp.where(col <= row, a, 0.0).astype(BF16) for a in atts]
        outs = [x + _dot(a, v) for x, a, v in zip(inter, atts, vals)]
        for s, v, (n, h) in zip(states, vals, jobs):
            kd_t = hk(per_seq[n][3], h).T.astype(BF16)
            s_ref[n, h] = s * _col_bcast(hk(per_seq[n][4], h), G_DV) + _dot(kd_t, v)
        for o, (n, h) in zip(outs, jobs):
            vs = slice(h * G_DV, (h + 1) * G_DV)
            o_ref[n, rows, vs] = _gla_out(o, rg_ref[n, rows, vs].astype(F32), gn_ref[...])
        return carry

    lax.fori_loop(0, n_blocks, step, 0)


def _gla_prompt(qg, kg, vg, la, rg, gn, batch, seq):
    assert seq % GLA_CHUNK == 0 and batch % GLA_SEQS == 0
    chunks = seq // GLA_CHUNK

    def spec(width):
        return pl.BlockSpec((GLA_SEQS, GLA_CHUNK, width), lambda b, c: (b, c, 0))

    def per_seq(x):
        return x.reshape(batch, seq, x.shape[-1])
    o, s = pl.pallas_call(
        functools.partial(_gla_prompt_kernel, n_blocks=GLA_CHUNK // GLA_BLOCK),
        grid=(batch // GLA_SEQS, chunks),
        in_specs=[spec(G_KW), spec(G_KW), spec(G_VW), spec(G_KW), spec(G_VW),
                  _resident((1, G_DV))],
        out_specs=[spec(G_VW),
                   pl.BlockSpec((GLA_SEQS, G_HEADS, G_DK, G_DV), lambda b, c: (b, 0, 0, 0))],
        out_shape=[jax.ShapeDtypeStruct((batch, seq, G_VW), BF16),
                   jax.ShapeDtypeStruct((batch, G_HEADS, G_DK, G_DV), F32)],
        compiler_params=_params(2),
        name="gla_prompt",
    )(per_seq(qg), per_seq(kg), per_seq(vg), per_seq(la), per_seq(rg), gn)
    return o.reshape(batch * seq, G_VW), s


def _gla_sample_kernel(q_ref, k_ref, v_ref, la_ref, rg_ref, gn_ref, s0_ref, o_ref, s_ref,
                       *, bb, t_new):
    per_body = 2

    def hk(x, h):
        return x[:, h * G_DK:(h + 1) * G_DK]

    def hv(x, h):
        return x[:, h * G_DV:(h + 1) * G_DV]

    def body(i0, carry):
        trow = lax.broadcasted_iota(jnp.int32, (t_new, G_KW), 0)
        arow = lax.broadcasted_iota(jnp.int32, (t_new, LANES), 0)
        acol = lax.broadcasted_iota(jnp.int32, (t_new, LANES), 1)
        zk = jnp.zeros((LANES - t_new, G_KW), F32)
        zv = jnp.zeros((LANES - t_new, G_VW), F32)
        seqs = [per_body * i0 + j for j in range(per_body)]
        jobs = [(n, h) for n in range(per_body) for h in range(G_HEADS)]

        bs = []
        for i in seqs:
            b = la_ref[i]
            shift = 1
            while shift < t_new:
                b = b + jnp.where(trow >= shift, pltpu.roll(b, shift, 0), 0.0)
                shift *= 2
            bs.append(b)
        qs = [q_ref[i].astype(F32) for i in seqs]
        ks = [k_ref[i].astype(F32) for i in seqs]
        vs = [jnp.concatenate([v_ref[i].astype(F32), zv], axis=0).astype(BF16) for i in seqs]
        b_last = [b[t_new - 1:t_new, :] for b in bs]
        qe = [(q * jnp.exp(b)).astype(BF16) for q, b in zip(qs, bs)]
        ke = [jnp.concatenate([k * jnp.exp(-b), zk], axis=0).astype(BF16) for k, b in zip(ks, bs)]
        kd = [jnp.concatenate([k * jnp.exp(bl - b), zk], axis=0)
              for k, b, bl in zip(ks, bs, b_last)]
        decay = [jnp.exp(bl) for bl in b_last]

        s0 = [s0_ref[seqs[n], h] for n, h in jobs]
        att = [jnp.where(acol <= arow, _dot_nt(hk(qe[n], h), hk(ke[n], h)), 0.0).astype(BF16)
               for n, h in jobs]
        out = [_dot(a, hv(vs[n], h)) + _dot(hk(qe[n], h), s.astype(BF16))
               for a, s, (n, h) in zip(att, s0, jobs)]
        for s, (n, h) in zip(s0, jobs):
            s_ref[seqs[n], h] = (s * _col_bcast(hk(decay[n], h), G_DV)
                                 + _dot(hk(kd[n], h).T.astype(BF16), hv(vs[n], h)))
        for o, (n, h) in zip(out, jobs):
            sl = slice(h * G_DV, (h + 1) * G_DV)
            o_ref[seqs[n], :, sl] = _gla_out(o, rg_ref[seqs[n], :, sl].astype(F32), gn_ref[...])
        return carry

    assert bb % per_body == 0
    lax.fori_loop(0, bb // per_body, body, 0)


def _gla_sample(qg, kg, vg, la, rg, gn, s0, batch, t_new, bb):
    def spec(width):
        return pl.BlockSpec((bb, t_new, width), lambda b: (b, 0, 0))
    st_spec = pl.BlockSpec((bb, G_HEADS, G_DK, G_DV), lambda b: (b, 0, 0, 0))
    o, s = pl.pallas_call(
        functools.partial(_gla_sample_kernel, bb=bb, t_new=t_new),
        grid=(batch // bb,),
        in_specs=[spec(G_KW), spec(G_KW), spec(G_VW), spec(G_KW), spec(G_VW),
                  _resident((1, G_DV)), st_spec],
        out_specs=[spec(G_VW), st_spec],
        out_shape=[jax.ShapeDtypeStruct((batch, t_new, G_VW), BF16),
                   jax.ShapeDtypeStruct((batch, G_HEADS, G_DK, G_DV), F32)],
        compiler_params=_params(1),
        name="gla_sample",
    )(qg.reshape(batch, t_new, G_KW), kg.reshape(batch, t_new, G_KW),
      vg.reshape(batch, t_new, G_VW), la.reshape(batch, t_new, G_KW),
      rg.reshape(batch, t_new, G_VW), gn, s0)
    return o.reshape(batch * t_new, G_VW), s


def _out_kernel(x_ref, a_ref, gb_ref, gate_ref, wa_ref, wb_ref, wo_ref, g_ref, wg_ref, wu_ref,
                wd_ref, o_ref, act_ref, between=None):
    a_out = _dot(a_ref[...], wa_ref[...])
    b_out = _dot(gb_ref[...], wb_ref[...])
    mix = (gate_ref[:, 0:D_MODEL].astype(F32) * a_out
           + gate_ref[:, D_MODEL:2 * D_MODEL].astype(F32) * b_out)
    x2 = x_ref[...] + _dot(mix.astype(BF16), wo_ref[...])
    o_ref[...] = _swiglu_residual(x2, g_ref, wg_ref, wu_ref, wd_ref, act_ref, between)


def _out(x1, attn, gb, gates, wa, wb, wo, g, wg, wu, wd, tm, row0=0):
    t = attn.shape[0]
    assert row0 % tm == 0 and t % tm == 0
    block0 = row0 // tm

    def tok(width):
        return pl.BlockSpec((tm, width), lambda i: (i, 0))
    consts = [wa, wb, wo, g, wg, wu, wd]
    return pl.pallas_call(
        _out_kernel,
        grid=(t // tm,),
        in_specs=[pl.BlockSpec((tm, D_MODEL), lambda i: (i + block0, 0)),
                  tok(A_WIDTH), tok(G_VW), tok(2 * D_MODEL)]
        + [_resident(c.shape) for c in consts],
        out_specs=tok(D_MODEL),
        out_shape=jax.ShapeDtypeStruct((t, D_MODEL), F32),
        scratch_shapes=[pltpu.VMEM((tm, D_FF), BF16)],
        compiler_params=_params(1),
        name="out_ffn2",
    )(x1, attn, gb, gates, *consts)


def _out_with_sample_attn_kernel(*refs, t_new, bb):
    out_in, attn_in = refs[:11], refs[11:17]
    y_ref, a_ref, act_ref = refs[17:]
    stages = _attn_sample_stages(*attn_in, a_ref, t_new=t_new, bb=bb)
    every = 3

    def between(c):
        if c % every == every - 1:
            next(stages, None)

    _out_kernel(*out_in, y_ref, act_ref, between=between)
    for _ in stages:
        pass


def _out_with_sample_attn(x1, attn, gb, gates, wa, wb, wo, g, wg, wu, wd,
                          qa, ka, va, c128, c512, c2048, batch, t_new, tm):
    t = attn.shape[0]
    steps = t // tm
    assert batch % steps == 0
    bb = batch // steps
    w3 = N_GROUPS * A_WIDTH

    def tok(width):
        return pl.BlockSpec((tm, width), lambda i: (i, 0))
    consts = [wa, wb, wo, g, wg, wu, wd]
    new_spec = pl.BlockSpec((bb, t_new, w3), lambda i: (i, 0, 0))
    cache_views = [jnp.transpose(c, (0, 2, 3, 4, 1)).reshape(batch, 2, A_WIDTH, c.shape[1])
                   for c in (c128, c512, c2048)]
    cache_specs = [pl.BlockSpec((bb, 2, A_WIDTH, c.shape[-1]), lambda i: (i, 0, 0, 0))
                   for c in cache_views]
    y, a_s = pl.pallas_call(
        functools.partial(_out_with_sample_attn_kernel, t_new=t_new, bb=bb),
        grid=(steps,),
        in_specs=[tok(D_MODEL), tok(A_WIDTH), tok(G_VW), tok(2 * D_MODEL)]
        + [_resident(c.shape) for c in consts] + [new_spec] * 3 + cache_specs,
        out_specs=[tok(D_MODEL), pl.BlockSpec((bb, t_new, A_WIDTH), lambda i: (i, 0, 0))],
        out_shape=[jax.ShapeDtypeStruct((t, D_MODEL), F32),
                   jax.ShapeDtypeStruct((batch, t_new, A_WIDTH), BF16)],
        scratch_shapes=[pltpu.VMEM((tm, D_FF), BF16)],
        compiler_params=_params(1),
        name="out_ffn2_attn_sample",
    )(x1, attn, gb, gates, *consts,
      qa.reshape(batch, t_new, w3), ka.reshape(batch, t_new, w3), va.reshape(batch, t_new, w3),
      *cache_views)
    return y, a_s.reshape(batch * t_new, A_WIDTH)


def _sample_rows_kernel(k_ref, v_ref, *refs, batch, t_new):
    o_refs, slab_ref = refs[:N_GROUPS], refs[N_GROUPS]
    n_slab = N_GROUPS * A_WIDTH // LANES
    per_group = A_WIDTH // LANES
    for kv, src in enumerate((k_ref, v_ref)):
        for s in range(n_slab):
            slab_ref[kv, s] = src[:, s * LANES:(s + 1) * LANES]
    for kv in range(2):
        for s in range(n_slab):
            g, part = divmod(s, per_group)
            for t in range(t_new):
                rows = slab_ref[kv, s, pl.ds(t, batch, stride=t_new), :]
                o_refs[g][t, kv, part * LANES:(part + 1) * LANES, :] = rows.T


def _sample_rows(ka, va, batch, t_new):
    assert batch == LANES
    n_slab = N_GROUPS * A_WIDTH // LANES
    outs = pl.pallas_call(
        functools.partial(_sample_rows_kernel, batch=batch, t_new=t_new),
        out_shape=[jax.ShapeDtypeStruct((t_new, 2, A_WIDTH, batch), F32)] * N_GROUPS,
        scratch_shapes=[pltpu.VMEM((2, n_slab, batch * t_new, LANES), F32)],
        compiler_params=pltpu.CompilerParams(vmem_limit_bytes=VMEM_LIMIT),
        name="sample_rows",
    )(ka, va)
    return [jnp.transpose(o.reshape(t_new, 2, A_HEADS, A_HEAD_DIM, batch), (4, 0, 1, 2, 3))
            for o in outs]


def _position_major(rows, batch):
    window = rows.shape[-1]
    return jnp.transpose(rows.reshape(batch, 2, A_HEADS, A_HEAD_DIM, window), (0, 4, 1, 2, 3))


def kernel(x_prompt, x_sample, cache_swa128_kv, cache_swa512_kv, cache_swa2048_kv, state_gla,
           norm_ffn1, ffn1_gate, ffn1_up, ffn1_down, norm_mix, w_in, a_q_norm, a_k_norm,
           g_alpha_up, g_alpha_bias, g_out_norm, w_a_out, w_b_out, w_out,
           norm_ffn2, ffn2_gate, ffn2_up, ffn2_down):
    batch, seq, _ = x_prompt.shape
    dec_batch, dec_seq, _ = x_sample.shape

    aw = N_GROUPS * A_WIDTH
    w_in_t = w_in.T.astype(BF16)
    up = jnp.pad(g_alpha_up, ((0, LANES - G_LOWRANK), (0, 0))).astype(BF16)
    bias = g_alpha_bias.reshape(1, G_KW)
    qgain = jnp.tile(a_q_norm, (1, A_HEADS)).reshape(1, aw)
    kgain = jnp.tile(a_k_norm, (1, A_HEADS)).reshape(1, aw)
    lane = jnp.arange(A_WIDTH) // A_HEAD_DIM
    hmean = jnp.where(lane[:, None] == lane[None, :], 1.0 / A_HEAD_DIM, 0.0).astype(BF16)
    gn = g_out_norm.reshape(1, G_DV)
    ffn1 = (norm_ffn1.reshape(1, D_MODEL), ffn1_gate, ffn1_up, ffn1_down)
    ffn2 = (norm_ffn2.reshape(1, D_MODEL), ffn2_gate.astype(BF16), ffn2_up.astype(BF16),
            ffn2_down.astype(BF16))
    proj_w = (norm_mix.reshape(1, D_MODEL), w_in_t, up, bias, qgain, kgain, hmean)
    out_w = (w_a_out.astype(BF16), w_b_out.astype(BF16), w_out.astype(BF16))

    t_p, t_s = batch * seq, dec_batch * dec_seq
    x1 = _ffn(x_prompt.reshape(t_p, D_MODEL), x_sample.reshape(t_s, D_MODEL), *ffn1, tm=512)

    qa, ka, va, qg_s, kg_s, vg_s, rg_s, la_s, gates_s = _proj(
        x1, *proj_w, tm=512, rows=range(t_p, t_p + t_s))
    rows_s = _sample_rows(ka, va, dec_batch, dec_seq)

    proj_p = _proj(x1, *proj_w, tm=512, rows=range(0, t_p), seq=seq)
    qkv_units, rows_p, (qg, kg, vg, rg, la, gates) = proj_p[:9], proj_p[9:12], proj_p[12:]
    a_p = _attn_prompt(qkv_units, batch, seq)
    gbp, state_p = _gla_prompt(qg, kg, vg, la, rg, gn, batch, seq)
    y_prompt, a_s = _out_with_sample_attn(
        x1, a_p, gbp, gates, *out_w, *ffn2, qa, ka, va,
        cache_swa128_kv, cache_swa512_kv, cache_swa2048_kv, dec_batch, dec_seq, tm=256)
    y_prompt = y_prompt.reshape(batch, seq, D_MODEL)
    rows_p = [_position_major(r, batch) for r in rows_p]

    gbs, state_s = _gla_sample(qg_s, kg_s, vg_s, la_s, rg_s, gn, state_gla, dec_batch, dec_seq,
                               bb=8)
    y_sample = _out(x1, a_s, gbs, gates_s, *out_w, *ffn2, tm=512, row0=t_p).reshape(
        dec_batch, dec_seq, D_MODEL)

    return (y_prompt, y_sample, rows_p[0], rows_p[1], rows_p[2], state_p,
            rows_s[0], rows_s[1], rows_s[2], state_s)
```

```python
import functools

import jax
import jax.numpy as jnp
from jax import lax
from jax.experimental import pallas as pl
from jax.experimental.pallas import tpu as pltpu

F32 = jnp.float32
BF16 = jnp.bfloat16

D_MODEL = 1024
D_FF = 2816
SWA_DILATIONS = (1, 4, 16)
N_GROUPS = 3
A_HEADS = 4
A_HEAD_DIM = 64
A_WIDTH = A_HEADS * A_HEAD_DIM
BAND = 128
G_HEADS = 4
G_DK = 128
G_DV = 256
G_KW = G_HEADS * G_DK
G_VW = G_HEADS * G_DV
G_LOWRANK = 16
G_TAU = 16.0
EPS = 1e-6
W_IN_GLA = 3 * N_GROUPS * A_WIDTH
W_IN_LR = W_IN_GLA + 2 * G_KW + 2 * G_VW
W_IN_GATE = W_IN_LR + G_LOWRANK
D_IN = W_IN_GATE + 2 * D_MODEL

LANES = 128
GLA_BLOCK = 128
GLA_SUB = 32
FF_CHUNK = 256
N_CHUNK = 512
VMEM_LIMIT = 58 * 1024 * 1024


def _resident(shape):
    nd = len(shape)
    return pl.BlockSpec(shape, lambda *_: (0,) * nd, pipeline_mode=pl.Buffered(1))


def _params(n_axes):
    return pltpu.CompilerParams(
        dimension_semantics=("arbitrary",) * n_axes, vmem_limit_bytes=VMEM_LIMIT)


def _dot(a, b):
    return jnp.dot(a, b, preferred_element_type=F32)


def _dot_nt(a, b):
    return lax.dot_general(a, b, (((1,), (1,)), ((), ())), preferred_element_type=F32)


def _rmsnorm(x, g):
    return x * lax.rsqrt(jnp.mean(x * x, axis=-1, keepdims=True) + EPS) * g


def _swiglu_residual(x, g_ref, wg_ref, wu_ref, wd_ref, act_ref, between=None):
    h = _rmsnorm(x, g_ref[...]).astype(BF16)
    for c in range(D_FF // FF_CHUNK):
        sl = slice(c * FF_CHUNK, (c + 1) * FF_CHUNK)
        gate = _dot(h, wg_ref[:, sl].astype(BF16))
        up = _dot(h, wu_ref[:, sl].astype(BF16))
        act_ref[:, sl] = (gate * jax.nn.sigmoid(gate) * up).astype(BF16)
        if between is not None:
            between(c)
    return x + 0.5 * _dot(act_ref[...], wd_ref[...].astype(BF16))


def _ffn_kernel(xa_ref, xb_ref, g_ref, wg_ref, wu_ref, wd_ref, o_ref, act_ref, *, steps_a):
    x = jnp.where(pl.program_id(0) < steps_a, xa_ref[...], xb_ref[...])
    o_ref[...] = _swiglu_residual(x, g_ref, wg_ref, wu_ref, wd_ref, act_ref)


def _ffn(xa, xb, g, wg, wu, wd, tm):
    steps_a, steps_b = xa.shape[0] // tm, xb.shape[0] // tm
    assert xa.shape[0] % tm == 0 and xb.shape[0] % tm == 0
    return pl.pallas_call(
        functools.partial(_ffn_kernel, steps_a=steps_a),
        grid=(steps_a + steps_b,),
        in_specs=[
            pl.BlockSpec((tm, D_MODEL), lambda i: (jnp.minimum(i, steps_a - 1), 0)),
            pl.BlockSpec((tm, D_MODEL), lambda i: (jnp.maximum(i - steps_a, 0), 0)),
            _resident((1, D_MODEL)),
            _resident((D_MODEL, D_FF)),
            _resident((D_MODEL, D_FF)),
            _resident((D_FF, D_MODEL)),
        ],
        out_specs=pl.BlockSpec((tm, D_MODEL), lambda i: (i, 0)),
        out_shape=jax.ShapeDtypeStruct((xa.shape[0] + xb.shape[0], D_MODEL), F32),
        scratch_shapes=[pltpu.VMEM((tm, D_FF), BF16)],
        compiler_params=_params(1),
        name="ffn1",
    )(xa, xb, g, wg, wu, wd)


def _log_sigmoid(z):
    return jnp.minimum(z, 0.0) - jnp.log1p(jnp.exp(-jnp.abs(z)))


def _store_blocked(x, d, out_ref, perm_ref):
    tm = x.shape[0]
    if d == 1:
        for j in range(tm // BAND):
            out_ref[j] = x[j * BAND:(j + 1) * BAND].astype(BF16)
        return
    for s in range(A_WIDTH // LANES):
        perm_ref[s] = x[:, s * LANES:(s + 1) * LANES]
    for r in range(d):
        rows = jnp.concatenate(
            [perm_ref[s, pl.ds(r, tm // d, stride=d), :] for s in range(A_WIDTH // LANES)], axis=1)
        out_ref[r] = rows.astype(BF16)


def _proj_kernel(*refs, blocked, tiles_per_seq):
    x_ref, g_ref, w_ref, up_ref, bias_ref, qgain_ref, kgain_ref, hmean_ref = refs[:8]
    outs = refs[8:]
    if blocked:
        qb_refs, kb_refs, vb_refs, row_refs = outs[0:3], outs[3:6], outs[6:9], outs[9:12]
        qg_ref, kg_ref, vg_ref, rg_ref, la_ref, gate_ref, perm_ref = outs[12:]
        tm = x_ref.shape[0]
    else:
        qa_ref, ka_ref, va_ref, qg_ref, kg_ref, vg_ref, rg_ref, la_ref, gate_ref = outs

    h = _rmsnorm(x_ref[...], g_ref[...]).astype(BF16)
    hmean = hmean_ref[...]

    aw = N_GROUPS * A_WIDTH
    groups = range(N_GROUPS)

    def proj(lo, width):
        return _dot_nt(h, w_ref[lo:lo + width, :])

    q_raw = [proj(g * A_WIDTH, A_WIDTH) for g in groups]
    k_raw = [proj(aw + g * A_WIDTH, A_WIDTH) for g in groups]
    vals = [proj(2 * aw + g * A_WIDTH, A_WIDTH) for g in groups]

    qg_ref[...] = (proj(W_IN_GLA, G_KW) * (G_DK ** -0.5)).astype(BF16)
    kg_ref[...] = proj(W_IN_GLA + G_KW, G_KW).astype(BF16)

    q_ms = [_dot((x * x).astype(BF16), hmean) for x in q_raw]
    k_ms = [_dot((x * x).astype(BF16), hmean) for x in k_raw]

    def attn_epilogue(g):
        d = SWA_DILATIONS[g]
        sl = slice(g * A_WIDTH, (g + 1) * A_WIDTH)
        q = q_raw[g] * lax.rsqrt(q_ms[g] + EPS) * qgain_ref[:, sl] * (A_HEAD_DIM ** -0.5)
        k = k_raw[g] * lax.rsqrt(k_ms[g] + EPS) * kgain_ref[:, sl]
        v = vals[g]
        if blocked:
            for i, (x, o_refs) in enumerate(((q, qb_refs), (k, kb_refs), (v, vb_refs))):
                _store_blocked(x, d, o_refs[g], perm_ref.at[(g - 1) * 3 + i] if d > 1 else None)
            keep = min(BAND * d, tm)
            row_refs[g][0] = k[tm - keep:, :].T
            row_refs[g][1] = v[tm - keep:, :].T
        else:
            ka_ref[:, sl] = k
            va_ref[:, sl] = v
            qa_ref[:, sl] = q.astype(BF16)

    n_gate = 2 * D_MODEL // N_CHUNK
    for c in range(n_gate):
        sl = slice(c * N_CHUNK, (c + 1) * N_CHUNK)
        gate_ref[:, sl] = jax.nn.sigmoid(proj(W_IN_GATE + c * N_CHUNK, N_CHUNK)).astype(BF16)
        if c < N_GROUPS:
            attn_epilogue(c)
    assert n_gate >= N_GROUPS

    for c in range(G_VW // N_CHUNK):
        sl = slice(c * N_CHUNK, (c + 1) * N_CHUNK)
        off = W_IN_GLA + 2 * G_KW + c * N_CHUNK
        vg_ref[:, sl] = proj(off, N_CHUNK).astype(BF16)
        rg_ref[:, sl] = proj(G_VW + off, N_CHUNK).astype(BF16)

    lr = proj(W_IN_LR, LANES).astype(BF16)
    z = _dot(lr, up_ref[...]) + bias_ref[...]
    la_ref[...] = _log_sigmoid(z) * (1.0 / G_TAU)


def _unit_spec(d, tm):
    per_class = tm // d
    if per_class >= BAND:
        return pl.BlockSpec((tm // BAND, BAND, A_WIDTH), lambda i: (i, 0, 0))
    parts = BAND // per_class
    return pl.BlockSpec((d, per_class, A_WIDTH), lambda i: (i // parts, i % parts, 0))


def _row_spec(d, tm, tiles_per_seq):
    window = BAND * d
    if window <= tm:
        return pl.BlockSpec((None, 2, A_WIDTH, window), lambda i: (i // tiles_per_seq, 0, 0, 0))
    first_tile = tiles_per_seq - window // tm
    return pl.BlockSpec(
        (None, 2, A_WIDTH, tm),
        lambda i: (i // tiles_per_seq, 0, 0, jnp.maximum(i % tiles_per_seq - first_tile, 0)))


def _proj(x1, g, w_in_t, up, bias, qgain, kgain, hmean, tm, rows, seq=None):
    t = rows.stop - rows.start
    assert rows.start % tm == 0 and t % tm == 0
    block0 = rows.start // tm
    aw = N_GROUPS * A_WIDTH
    blocked = seq is not None
    widths = [(G_KW, BF16), (G_KW, BF16), (G_VW, BF16), (G_VW, BF16), (G_KW, F32),
              (2 * D_MODEL, BF16)]
    scratch = []
    tiles_per_seq = None
    if blocked:
        assert tm == 4 * BAND and SWA_DILATIONS == (1, 4, 16)
        assert seq % tm == 0 and BAND * SWA_DILATIONS[-1] <= seq
        tiles_per_seq = seq // tm
        out_specs = ([_unit_spec(d, tm) for _ in range(3) for d in SWA_DILATIONS]
                     + [_row_spec(d, tm, tiles_per_seq) for d in SWA_DILATIONS])
        out_shape = ([jax.ShapeDtypeStruct((t // BAND, BAND, A_WIDTH), BF16)] * 9
                     + [jax.ShapeDtypeStruct((t // seq, 2, A_WIDTH, BAND * d), F32)
                        for d in SWA_DILATIONS])
        scratch = [pltpu.VMEM((3 * (N_GROUPS - 1), A_WIDTH // LANES, tm, LANES), F32)]
    else:
        widths = [(aw, BF16), (aw, F32), (aw, F32)] + widths
        out_specs, out_shape = [], []
    out_specs = out_specs + [pl.BlockSpec((tm, w), lambda i: (i, 0)) for w, _ in widths]
    out_shape = out_shape + [jax.ShapeDtypeStruct((t, w), dt) for w, dt in widths]
    assert w_in_t.shape == (D_IN, D_MODEL) and W_IN_GATE % 16 == 0
    consts = [g, w_in_t, up, bias, qgain, kgain, hmean]
    return pl.pallas_call(
        functools.partial(_proj_kernel, blocked=blocked, tiles_per_seq=tiles_per_seq),
        grid=(t // tm,),
        in_specs=[pl.BlockSpec((tm, D_MODEL), lambda i: (i + block0, 0))]
        + [_resident(c.shape) for c in consts],
        out_specs=out_specs,
        out_shape=out_shape,
        scratch_shapes=scratch,
        compiler_params=_params(1),
        name="in_proj",
    )(x1, *consts)


def _log2(n):
    assert n > 0 and n & (n - 1) == 0, n
    return n.bit_length() - 1


def _head_masks(rows):
    lane = lax.broadcasted_iota(jnp.int32, (rows, A_WIDTH), 1)
    return [(lane >> _log2(A_HEAD_DIM)) == h for h in range(A_HEADS)]


ATTN_SPAN = 2048
ATTN_UNROLL = 4


def _attn_prompt_kernel(*refs, units):
    q_refs, k_refs, v_refs = refs[0:3], refs[3:6], refs[6:9]
    o_ref, osc_ref, lsc_ref, bias_ref = refs[9:]
    span = pl.program_id(1)
    n_slab = A_WIDTH // LANES

    row = lax.broadcasted_iota(jnp.int32, (A_HEADS * BAND, 2 * BAND), 0) & (BAND - 1)
    col = lax.broadcasted_iota(jnp.int32, (A_HEADS * BAND, 2 * BAND), 1)
    own = (col >= BAND) & (col - BAND <= row)
    bias_ref[0] = jnp.where(own, 0.0, -jnp.inf)
    bias_ref[1] = jnp.where(own | ((col < BAND) & (col >= row)), 0.0, -jnp.inf)

    for g, d in enumerate(SWA_DILATIONS):
        q_ref, k_ref, v_ref = q_refs[g], k_refs[g], v_refs[g]

        def scores(ul, d=d, q_ref=q_ref, k_ref=k_ref):
            u = span * units + ul
            up = jnp.maximum(u - d, 0)
            kb = jnp.concatenate([k_ref[up], k_ref[u]], axis=0)
            q = q_ref[ul]
            qm = jnp.concatenate(
                [q * m.astype(F32).astype(BF16) for m in _head_masks(BAND)], axis=0)
            has_prev = jnp.where(u >= d, 1, 0)
            return _dot_nt(qm, kb) + bias_ref[has_prev]

        def softmax(s):
            m = jnp.max(s, axis=-1, keepdims=True)
            p = jnp.exp(s - m)
            return p.astype(BF16), m, jnp.sum(p, axis=-1, keepdims=True)

        def values(ul, p, d=d, v_ref=v_ref):
            u = span * units + ul
            vb = jnp.concatenate([v_ref[jnp.maximum(u - d, 0)], v_ref[u]], axis=0)
            return _dot(p, vb)

        def store(ul, o_all, m, den, d=d, g=g):
            start = (ul >> _log2(d)) * (BAND * d) + (ul & (d - 1))
            if d == 1:
                rows = pl.ds(pl.multiple_of(start, BAND), BAND)
            else:
                rows = pl.ds(start, BAND, stride=d)
            per_slab = LANES // A_HEAD_DIM
            first = lax.broadcasted_iota(jnp.int32, (BAND, LANES), 1) < A_HEAD_DIM
            assert per_slab == 2
            for sl in range(n_slab):
                lanes = slice(sl * LANES, (sl + 1) * LANES)
                ra, rb = [slice(h * BAND, (h + 1) * BAND)
                          for h in (per_slab * sl, per_slab * sl + 1)]
                den_s = jnp.where(first, den[ra], den[rb])
                o_s = jnp.where(first, o_all[ra, lanes], o_all[rb, lanes])
                osc_ref[g, sl, rows, :] = o_s * (1.0 / den_s)
                lsc_ref[g, sl, rows, :] = jnp.where(first, m[ra], m[rb]) + jnp.log(den_s)

        def unit_group(i, carry, scores=scores, softmax=softmax, values=values, store=store):
            uls = [ATTN_UNROLL * i + j for j in range(ATTN_UNROLL)]
            ss = [scores(ul) for ul in uls]
            sm = [softmax(s) for s in ss]
            os_ = [values(ul, p) for ul, (p, _, _) in zip(uls, sm)]
            for ul, o_all, (_, m, den) in zip(uls, os_, sm):
                store(ul, o_all, m, den)
            return carry

        assert units % ATTN_UNROLL == 0
        lax.fori_loop(0, units // ATTN_UNROLL, unit_group, 0)

    def merge(i, carry):
        rows = pl.ds(pl.multiple_of(i * BAND, BAND), BAND)

        def natural(ref, g):
            return jnp.concatenate([ref[g, sl, rows, :] for sl in range(n_slab)], axis=1)

        outs = [natural(osc_ref, g) for g in range(N_GROUPS)]
        lses = [natural(lsc_ref, g) for g in range(N_GROUPS)]
        m = functools.reduce(jnp.maximum, lses)
        ws = [jnp.exp(l - m) for l in lses]
        a = functools.reduce(jnp.add, [w * o for w, o in zip(ws, outs)])
        o_ref[rows, :] = (a * (1.0 / functools.reduce(jnp.add, ws))).astype(BF16)
        return carry

    lax.fori_loop(0, ATTN_SPAN // BAND, merge, 0)


def _attn_prompt(qkv_units, batch, seq):
    assert seq % ATTN_SPAN == 0
    units = ATTN_SPAN // BAND
    per_seq = seq // BAND
    spans = seq // ATTN_SPAN
    q_spec = pl.BlockSpec((units, BAND, A_WIDTH), lambda b, s: (b * spans + s, 0, 0))
    kv_spec = pl.BlockSpec((per_seq, BAND, A_WIDTH), lambda b, s: (b, 0, 0))
    return pl.pallas_call(
        functools.partial(_attn_prompt_kernel, units=units),
        grid=(batch, spans),
        in_specs=[q_spec] * 3 + [kv_spec] * 6,
        out_specs=pl.BlockSpec((ATTN_SPAN, A_WIDTH), lambda b, s: (b * spans + s, 0)),
        out_shape=jax.ShapeDtypeStruct((batch * seq, A_WIDTH), BF16),
        scratch_shapes=[pltpu.VMEM((N_GROUPS, A_WIDTH // LANES, ATTN_SPAN, LANES), F32)] * 2
        + [pltpu.VMEM((2, A_HEADS * BAND, 2 * BAND), F32)],
        compiler_params=_params(2),
        name="attn_prompt",
    )(*qkv_units)


def _attn_sample_kernel(*refs, t_new, bb):
    for _ in _attn_sample_stages(*refs, t_new=t_new, bb=bb):
        pass


def _attn_sample_stages(q_ref, k_ref, v_ref, c1_ref, c4_ref, c16_ref, o_ref, *, t_new, bb):
    hmask = _head_masks(t_new)
    rows = A_HEADS * t_new
    row_t = lax.broadcasted_iota(jnp.int32, (rows, LANES), 0) & (t_new - 1)
    col = lax.broadcasted_iota(jnp.int32, (rows, LANES), 1)
    pad = jnp.zeros((LANES - t_new, A_WIDTH), F32)
    caches = (c1_ref, c4_ref, c16_ref)
    jobs = [(i, g) for i in range(bb) for g in range(N_GROUPS)]

    ok_old, ok_new = [], []
    for d, c_ref in zip(SWA_DILATIONS, caches):
        window = c_ref.shape[-1]
        assert window == BAND * d, (window, d)
        pos_t = lax.broadcasted_iota(jnp.int32, (rows, window), 0) & (t_new - 1)
        pos = lax.broadcasted_iota(jnp.int32, (rows, window), 1)
        ok_old.append((pos >= pos_t) & (((pos - pos_t) & (d - 1)) == 0))
        ok_new.append((col <= row_t) & (((row_t - col) & (d - 1)) == 0))

    def new_rows(ref, i, g):
        sl = slice(g * A_WIDTH, (g + 1) * A_WIDTH)
        return jnp.concatenate([ref[i, :, sl], pad], axis=0).astype(BF16)

    def masked_q(i, g):
        q = q_ref[i, :, g * A_WIDTH:(g + 1) * A_WIDTH].astype(F32)
        return jnp.concatenate([q * m.astype(F32) for m in hmask], axis=0).astype(BF16)

    qms = [masked_q(i, g) for i, g in jobs]
    s_old = [jnp.where(ok_old[g], _dot(qm, caches[g][i, 0].astype(BF16)), -jnp.inf)
             for qm, (i, g) in zip(qms, jobs)]
    s_new = [jnp.where(ok_new[g], _dot_nt(qm, new_rows(k_ref, i, g)), -jnp.inf)
             for qm, (i, g) in zip(qms, jobs)]
    yield
    ms = [jnp.maximum(jnp.max(so, axis=-1, keepdims=True), jnp.max(sn, axis=-1, keepdims=True))
          for so, sn in zip(s_old, s_new)]
    p_old = [jnp.exp(so - m) for so, m in zip(s_old, ms)]
    p_new = [jnp.exp(sn - m) for sn, m in zip(s_new, ms)]
    dens = [jnp.sum(po, axis=-1, keepdims=True) + jnp.sum(pn, axis=-1, keepdims=True)
            for po, pn in zip(p_old, p_new)]
    yield
    accs = [_dot_nt(po.astype(BF16), caches[g][i, 1].astype(BF16))
            + _dot(pn.astype(BF16), new_rows(v_ref, i, g))
            for po, pn, (i, g) in zip(p_old, p_new, jobs)]
    yield

    for i in range(bb):
        num = jnp.zeros((t_new, A_WIDTH), F32)
        den_all = jnp.zeros((t_new, A_WIDTH), F32)
        m_all = jnp.full((t_new, A_WIDTH), -jnp.inf, F32)
        for g in range(N_GROUPS):
            j = jobs.index((i, g))
            o_g = jnp.zeros((t_new, A_WIDTH), F32)
            lse_g = jnp.zeros((t_new, A_WIDTH), F32)
            for h in range(A_HEADS):
                rs = slice(h * t_new, (h + 1) * t_new)
                o_g = jnp.where(hmask[h], accs[j][rs] * (1.0 / dens[j][rs]), o_g)
                lse_g = jnp.where(hmask[h], ms[j][rs] + jnp.log(dens[j][rs]), lse_g)
            m_next = jnp.maximum(m_all, lse_g)
            scale_old = jnp.exp(m_all - m_next)
            w = jnp.exp(lse_g - m_next)
            num = num * scale_old + w * o_g
            den_all = den_all * scale_old + w
            m_all = m_next
        o_ref[i] = (num * (1.0 / den_all)).astype(BF16)


def _attn_sample(qa, ka, va, c128, c512, c2048, batch, t_new, bb):
    w3 = N_GROUPS * A_WIDTH
    tok_spec = pl.BlockSpec((bb, t_new, w3), lambda b: (b, 0, 0))
    cache_views = [jnp.transpose(c, (0, 2, 3, 4, 1)).reshape(batch, 2, A_WIDTH, c.shape[1])
                   for c in (c128, c512, c2048)]
    cache_specs = [pl.BlockSpec((bb, 2, A_WIDTH, c.shape[-1]), lambda b: (b, 0, 0, 0))
                   for c in cache_views]
    out = pl.pallas_call(
        functools.partial(_attn_sample_kernel, t_new=t_new, bb=bb),
        grid=(batch // bb,),
        in_specs=[tok_spec, tok_spec, tok_spec] + cache_specs,
        out_specs=pl.BlockSpec((bb, t_new, A_WIDTH), lambda b: (b, 0, 0)),
        out_shape=jax.ShapeDtypeStruct((batch, t_new, A_WIDTH), BF16),
        compiler_params=_params(1),
        name="attn_sample",
    )(qa.reshape(batch, t_new, w3), ka.reshape(batch, t_new, w3), va.reshape(batch, t_new, w3),
      *cache_views)
    return out.reshape(batch * t_new, A_WIDTH)


def _split2(x):
    hi = x.astype(BF16)
    return hi, (x - hi.astype(F32)).astype(BF16)


def _gla_out(o, rg, gain):
    on = o * lax.rsqrt(jnp.mean(o * o, axis=-1, keepdims=True) + EPS) * gain
    return (on * rg * jax.nn.sigmoid(rg)).astype(BF16)


def _col_bcast(vec_row, width):
    sq = jnp.broadcast_to(vec_row, (LANES, LANES)).T
    return jnp.concatenate([sq] * (width // LANES), axis=1)


GLA_CHUNK = 512
GLA_SEQS = 4


def _gla_prompt_kernel(q_ref, k_ref, v_ref, la_ref, rg_ref, gn_ref, o_ref, s_ref, *, n_blocks):
    cb, sub = GLA_BLOCK, GLA_SUB
    n_sub = cb // sub
    n_seq = q_ref.shape[0]

    @pl.when(pl.program_id(1) == 0)
    def _():
        s_ref[...] = jnp.zeros_like(s_ref)

    jobs = [(n, h) for n in range(n_seq) for h in range(G_HEADS)]

    def hk(x, h):
        return x[:, h * G_DK:(h + 1) * G_DK]

    def decays(n, rows, tri, rowk):
        hi, lo = _split2(la_ref[n, rows, :])
        b = _dot(tri, hi) + _dot(tri, lo)
        q = q_ref[n, rows, :].astype(F32)
        k = k_ref[n, rows, :].astype(F32)
        b_last = b[cb - 1:cb, :]
        qe = (q * jnp.exp(b)).astype(BF16)
        sub_end = [b[(j + 1) * sub - 1:(j + 1) * sub, :] for j in range(n_sub)]
        own_end = jnp.concatenate(
            [jnp.broadcast_to(e, (sub, G_KW)) for e in sub_end], axis=0)
        kk = k * jnp.exp(own_end - b)
        q_parts, k_parts = [], []
        for j in range(n_sub):
            lo_row = j * sub
            qj = q[lo_row:, :] * jnp.exp(b[lo_row:, :] - sub_end[j])
            q_parts.append(jnp.concatenate(
                [jnp.zeros((lo_row, G_KW), F32), qj], axis=0).astype(BF16) if lo_row else
                qj.astype(BF16))
            in_sub = (rowk >= lo_row) & (rowk < lo_row + sub)
            k_parts.append(jnp.where(in_sub, kk, 0.0).astype(BF16))
        return qe, q_parts, k_parts, k * jnp.exp(b_last - b), jnp.exp(b_last)

    def step(i, carry):
        rows = pl.ds(pl.multiple_of(i * cb, cb), cb)
        row = lax.broadcasted_iota(jnp.int32, (cb, cb), 0)
        col = lax.broadcasted_iota(jnp.int32, (cb, cb), 1)
        tri = (row >= col).astype(F32).astype(BF16)
        rowk = lax.broadcasted_iota(jnp.int32, (cb, G_KW), 0)

        per_seq = [decays(n, rows, tri, rowk) for n in range(n_seq)]
        states = [s_ref[n, h] for n, h in jobs]
        vals = [v_ref[n, rows, h * G_DV:(h + 1) * G_DV] for n, h in jobs]
        inter = [_dot(hk(per_seq[n][0], h), s.astype(BF16)) for s, (n, h) in zip(states, jobs)]
        atts = [_dot_nt(jnp.concatenate([hk(p, h) for p in per_seq[n][1]], axis=1),
                        jnp.concatenate([hk(p, h) for p in per_seq[n][2]], axis=1))
                for n, h in jobs]
        atts = [jnp.where(col <= row, a, 0.0).astype(BF16) for a in atts]
        outs = [x + _dot(a, v) for x, a, v in zip(inter, atts, vals)]
        for s, v, (n, h) in zip(states, vals, jobs):
            kd_t = hk(per_seq[n][3], h).T.astype(BF16)
            s_ref[n, h] = s * _col_bcast(hk(per_seq[n][4], h), G_DV) + _dot(kd_t, v)
        for o, (n, h) in zip(outs, jobs):
            vs = slice(h * G_DV, (h + 1) * G_DV)
            o_ref[n, rows, vs] = _gla_out(o, rg_ref[n, rows, vs].astype(F32), gn_ref[...])
        return carry

    lax.fori_loop(0, n_blocks, step, 0)


def _gla_prompt(qg, kg, vg, la, rg, gn, batch, seq):
    assert seq % GLA_CHUNK == 0 and batch % GLA_SEQS == 0
    chunks = seq // GLA_CHUNK

    def spec(width):
        return pl.BlockSpec((GLA_SEQS, GLA_CHUNK, width), lambda b, c: (b, c, 0))

    def per_seq(x):
        return x.reshape(batch, seq, x.shape[-1])
    o, s = pl.pallas_call(
        functools.partial(_gla_prompt_kernel, n_blocks=GLA_CHUNK // GLA_BLOCK),
        grid=(batch // GLA_SEQS, chunks),
        in_specs=[spec(G_KW), spec(G_KW), spec(G_VW), spec(G_KW), spec(G_VW),
                  _resident((1, G_DV))],
        out_specs=[spec(G_VW),
                   pl.BlockSpec((GLA_SEQS, G_HEADS, G_DK, G_DV), lambda b, c: (b, 0, 0, 0))],
        out_shape=[jax.ShapeDtypeStruct((batch, seq, G_VW), BF16),
                   jax.ShapeDtypeStruct((batch, G_HEADS, G_DK, G_DV), F32)],
        compiler_params=_params(2),
        name="gla_prompt",
    )(per_seq(qg), per_seq(kg), per_seq(vg), per_seq(la), per_seq(rg), gn)
    return o.reshape(batch * seq, G_VW), s


def _gla_sample_kernel(q_ref, k_ref, v_ref, la_ref, rg_ref, gn_ref, s0_ref, o_ref, s_ref,
                       *, bb, t_new):
    per_body = 2

    def hk(x, h):
        return x[:, h * G_DK:(h + 1) * G_DK]

    def hv(x, h):
        return x[:, h * G_DV:(h + 1) * G_DV]

    def body(i0, carry):
        trow = lax.broadcasted_iota(jnp.int32, (t_new, G_KW), 0)
        arow = lax.broadcasted_iota(jnp.int32, (t_new, LANES), 0)
        acol = lax.broadcasted_iota(jnp.int32, (t_new, LANES), 1)
        zk = jnp.zeros((LANES - t_new, G_KW), F32)
        zv = jnp.zeros((LANES - t_new, G_VW), F32)
        seqs = [per_body * i0 + j for j in range(per_body)]
        jobs = [(n, h) for n in range(per_body) for h in range(G_HEADS)]

        bs = []
        for i in seqs:
            b = la_ref[i]
            shift = 1
            while shift < t_new:
                b = b + jnp.where(trow >= shift, pltpu.roll(b, shift, 0), 0.0)
                shift *= 2
            bs.append(b)
        qs = [q_ref[i].astype(F32) for i in seqs]
        ks = [k_ref[i].astype(F32) for i in seqs]
        vs = [jnp.concatenate([v_ref[i].astype(F32), zv], axis=0).astype(BF16) for i in seqs]
        b_last = [b[t_new - 1:t_new, :] for b in bs]
        qe = [(q * jnp.exp(b)).astype(BF16) for q, b in zip(qs, bs)]
        ke = [jnp.concatenate([k * jnp.exp(-b), zk], axis=0).astype(BF16) for k, b in zip(ks, bs)]
        kd = [jnp.concatenate([k * jnp.exp(bl - b), zk], axis=0)
              for k, b, bl in zip(ks, bs, b_last)]
        decay = [jnp.exp(bl) for bl in b_last]

        s0 = [s0_ref[seqs[n], h] for n, h in jobs]
        att = [jnp.where(acol <= arow, _dot_nt(hk(qe[n], h), hk(ke[n], h)), 0.0).astype(BF16)
               for n, h in jobs]
        out = [_dot(a, hv(vs[n], h)) + _dot(hk(qe[n], h), s.astype(BF16))
               for a, s, (n, h) in zip(att, s0, jobs)]
        for s, (n, h) in zip(s0, jobs):
            s_ref[seqs[n], h] = (s * _col_bcast(hk(decay[n], h), G_DV)
                                 + _dot(hk(kd[n], h).T.astype(BF16), hv(vs[n], h)))
        for o, (n, h) in zip(out, jobs):
            sl = slice(h * G_DV, (h + 1) * G_DV)
            o_ref[seqs[n], :, sl] = _gla_out(o, rg_ref[seqs[n], :, sl].astype(F32), gn_ref[...])
        return carry

    assert bb % per_body == 0
    lax.fori_loop(0, bb // per_body, body, 0)


def _gla_sample(qg, kg, vg, la, rg, gn, s0, batch, t_new, bb):
    def spec(width):
        return pl.BlockSpec((bb, t_new, width), lambda b: (b, 0, 0))
    st_spec = pl.BlockSpec((bb, G_HEADS, G_DK, G_DV), lambda b: (b, 0, 0, 0))
    o, s = pl.pallas_call(
        functools.partial(_gla_sample_kernel, bb=bb, t_new=t_new),
        grid=(batch // bb,),
        in_specs=[spec(G_KW), spec(G_KW), spec(G_VW), spec(G_KW), spec(G_VW),
                  _resident((1, G_DV)), st_spec],
        out_specs=[spec(G_VW), st_spec],
        out_shape=[jax.ShapeDtypeStruct((batch, t_new, G_VW), BF16),
                   jax.ShapeDtypeStruct((batch, G_HEADS, G_DK, G_DV), F32)],
        compiler_params=_params(1),
        name="gla_sample",
    )(qg.reshape(batch, t_new, G_KW), kg.reshape(batch, t_new, G_KW),
      vg.reshape(batch, t_new, G_VW), la.reshape(batch, t_new, G_KW),
      rg.reshape(batch, t_new, G_VW), gn, s0)
    return o.reshape(batch * t_new, G_VW), s


def _out_kernel(x_ref, a_ref, gb_ref, gate_ref, wa_ref, wb_ref, wo_ref, g_ref, wg_ref, wu_ref,
                wd_ref, o_ref, act_ref, between=None):
    a_out = _dot(a_ref[...], wa_ref[...])
    b_out = _dot(gb_ref[...], wb_ref[...])
    mix = (gate_ref[:, 0:D_MODEL].astype(F32) * a_out
           + gate_ref[:, D_MODEL:2 * D_MODEL].astype(F32) * b_out)
    x2 = x_ref[...] + _dot(mix.astype(BF16), wo_ref[...])
    o_ref[...] = _swiglu_residual(x2, g_ref, wg_ref, wu_ref, wd_ref, act_ref, between)


def _out(x1, attn, gb, gates, wa, wb, wo, g, wg, wu, wd, tm, row0=0):
    t = attn.shape[0]
    assert row0 % tm == 0 and t % tm == 0
    block0 = row0 // tm

    def tok(width):
        return pl.BlockSpec((tm, width), lambda i: (i, 0))
    consts = [wa, wb, wo, g, wg, wu, wd]
    return pl.pallas_call(
        _out_kernel,
        grid=(t // tm,),
        in_specs=[pl.BlockSpec((tm, D_MODEL), lambda i: (i + block0, 0)),
                  tok(A_WIDTH), tok(G_VW), tok(2 * D_MODEL)]
        + [_resident(c.shape) for c in consts],
        out_specs=tok(D_MODEL),
        out_shape=jax.ShapeDtypeStruct((t, D_MODEL), F32),
        scratch_shapes=[pltpu.VMEM((tm, D_FF), BF16)],
        compiler_params=_params(1),
        name="out_ffn2",
    )(x1, attn, gb, gates, *consts)


def _out_with_sample_attn_kernel(*refs, t_new, bb):
    out_in, attn_in = refs[:11], refs[11:17]
    y_ref, a_ref, act_ref = refs[17:]
    stages = _attn_sample_stages(*attn_in, a_ref, t_new=t_new, bb=bb)
    every = 3

    def between(c):
        if c % every == every - 1:
            next(stages, None)

    _out_kernel(*out_in, y_ref, act_ref, between=between)
    for _ in stages:
        pass


def _out_with_sample_attn(x1, attn, gb, gates, wa, wb, wo, g, wg, wu, wd,
                          qa, ka, va, c128, c512, c2048, batch, t_new, tm):
    t = attn.shape[0]
    steps = t // tm
    assert batch % steps == 0
    bb = batch // steps
    w3 = N_GROUPS * A_WIDTH

    def tok(width):
        return pl.BlockSpec((tm, width), lambda i: (i, 0))
    consts = [wa, wb, wo, g, wg, wu, wd]
    new_spec = pl.BlockSpec((bb, t_new, w3), lambda i: (i, 0, 0))
    cache_views = [jnp.transpose(c, (0, 2, 3, 4, 1)).reshape(batch, 2, A_WIDTH, c.shape[1])
                   for c in (c128, c512, c2048)]
    cache_specs = [pl.BlockSpec((bb, 2, A_WIDTH, c.shape[-1]), lambda i: (i, 0, 0, 0))
                   for c in cache_views]
    y, a_s = pl.pallas_call(
        functools.partial(_out_with_sample_attn_kernel, t_new=t_new, bb=bb),
        grid=(steps,),
        in_specs=[tok(D_MODEL), tok(A_WIDTH), tok(G_VW), tok(2 * D_MODEL)]
        + [_resident(c.shape) for c in consts] + [new_spec] * 3 + cache_specs,
        out_specs=[tok(D_MODEL), pl.BlockSpec((bb, t_new, A_WIDTH), lambda i: (i, 0, 0))],
        out_shape=[jax.ShapeDtypeStruct((t, D_MODEL), F32),
                   jax.ShapeDtypeStruct((batch, t_new, A_WIDTH), BF16)],
        scratch_shapes=[pltpu.VMEM((tm, D_FF), BF16)],
        compiler_params=_params(1),
        name="out_ffn2_attn_sample",
    )(x1, attn, gb, gates, *consts,
      qa.reshape(batch, t_new, w3), ka.reshape(batch, t_new, w3), va.reshape(batch, t_new, w3),
      *cache_views)
    return y, a_s.reshape(batch * t_new, A_WIDTH)


def _sample_rows_kernel(k_ref, v_ref, *refs, batch, t_new):
    o_refs, slab_ref = refs[:N_GROUPS], refs[N_GROUPS]
    n_slab = N_GROUPS * A_WIDTH // LANES
    per_group = A_WIDTH // LANES
    for kv, src in enumerate((k_ref, v_ref)):
        for s in range(n_slab):
            slab_ref[kv, s] = src[:, s * LANES:(s + 1) * LANES]
    for kv in range(2):
        for s in range(n_slab):
            g, part = divmod(s, per_group)
            for t in range(t_new):
                rows = slab_ref[kv, s, pl.ds(t, batch, stride=t_new), :]
                o_refs[g][t, kv, part * LANES:(part + 1) * LANES, :] = rows.T


def _sample_rows(ka, va, batch, t_new):
    assert batch == LANES
    n_slab = N_GROUPS * A_WIDTH // LANES
    outs = pl.pallas_call(
        functools.partial(_sample_rows_kernel, batch=batch, t_new=t_new),
        out_shape=[jax.ShapeDtypeStruct((t_new, 2, A_WIDTH, batch), F32)] * N_GROUPS,
        scratch_shapes=[pltpu.VMEM((2, n_slab, batch * t_new, LANES), F32)],
        compiler_params=pltpu.CompilerParams(vmem_limit_bytes=VMEM_LIMIT),
        name="sample_rows",
    )(ka, va)
    return [jnp.transpose(o.reshape(t_new, 2, A_HEADS, A_HEAD_DIM, batch), (4, 0, 1, 2, 3))
            for o in outs]


def _position_major(rows, batch):
    window = rows.shape[-1]
    return jnp.transpose(rows.reshape(batch, 2, A_HEADS, A_HEAD_DIM, window), (0, 4, 1, 2, 3))


def kernel(x_prompt, x_sample, cache_swa128_kv, cache_swa512_kv, cache_swa2048_kv, state_gla,
           norm_ffn1, ffn1_gate, ffn1_up, ffn1_down, norm_mix, w_in, a_q_norm, a_k_norm,
           g_alpha_up, g_alpha_bias, g_out_norm, w_a_out, w_b_out, w_out,
           norm_ffn2, ffn2_gate, ffn2_up, ffn2_down):
    batch, seq, _ = x_prompt.shape
    dec_batch, dec_seq, _ = x_sample.shape

    aw = N_GROUPS * A_WIDTH
    w_in_t = w_in.T.astype(BF16)
    up = jnp.pad(g_alpha_up, ((0, LANES - G_LOWRANK), (0, 0))).astype(BF16)
    bias = g_alpha_bias.reshape(1, G_KW)
    qgain = jnp.tile(a_q_norm, (1, A_HEADS)).reshape(1, aw)
    kgain = jnp.tile(a_k_norm, (1, A_HEADS)).reshape(1, aw)
    lane = jnp.arange(A_WIDTH) // A_HEAD_DIM
    hmean = jnp.where(lane[:, None] == lane[None, :], 1.0 / A_HEAD_DIM, 0.0).astype(BF16)
    gn = g_out_norm.reshape(1, G_DV)
    ffn1 = (norm_ffn1.reshape(1, D_MODEL), ffn1_gate, ffn1_up, ffn1_down)
    ffn2 = (norm_ffn2.reshape(1, D_MODEL), ffn2_gate.astype(BF16), ffn2_up.astype(BF16),
            ffn2_down.astype(BF16))
    proj_w = (norm_mix.reshape(1, D_MODEL), w_in_t, up, bias, qgain, kgain, hmean)
    out_w = (w_a_out.astype(BF16), w_b_out.astype(BF16), w_out.astype(BF16))

    t_p, t_s = batch * seq, dec_batch * dec_seq
    x1 = _ffn(x_prompt.reshape(t_p, D_MODEL), x_sample.reshape(t_s, D_MODEL), *ffn1, tm=512)

    qa, ka, va, qg_s, kg_s, vg_s, rg_s, la_s, gates_s = _proj(
        x1, *proj_w, tm=512, rows=range(t_p, t_p + t_s))
    rows_s = _sample_rows(ka, va, dec_batch, dec_seq)

    proj_p = _proj(x1, *proj_w, tm=512, rows=range(0, t_p), seq=seq)
    qkv_units, rows_p, (qg, kg, vg, rg, la, gates) = proj_p[:9], proj_p[9:12], proj_p[12:]
    a_p = _attn_prompt(qkv_units, batch, seq)
    gbp, state_p = _gla_prompt(qg, kg, vg, la, rg, gn, batch, seq)
    y_prompt, a_s = _out_with_sample_attn(
        x1, a_p, gbp, gates, *out_w, *ffn2, qa, ka, va,
        cache_swa128_kv, cache_swa512_kv, cache_swa2048_kv, dec_batch, dec_seq, tm=256)
    y_prompt = y_prompt.reshape(batch, seq, D_MODEL)
    rows_p = [_position_major(r, batch) for r in rows_p]

    gbs, state_s = _gla_sample(qg_s, kg_s, vg_s, la_s, rg_s, gn, state_gla, dec_batch, dec_seq,
                               bb=8)
    y_sample = _out(x1, a_s, gbs, gates_s, *out_w, *ffn2, tm=512, row0=t_p).reshape(
        dec_batch, dec_seq, D_MODEL)

    return (y_prompt, y_sample, rows_p[0], rows_p[1], rows_p[2], state_p,
            rows_s[0], rows_s[1], rows_s[2], state_s)
```

```python
import functools

import jax
import jax.numpy as jnp
from jax import lax
from jax.experimental import pallas as pl
from jax.experimental.pallas import tpu as pltpu

F32 = jnp.float32
BF16 = jnp.bfloat16

D_MODEL = 1024
D_FF = 2816
SWA_DILATIONS = (1, 4, 16)
N_GROUPS = 3
A_HEADS = 4
A_HEAD_DIM = 64
A_WIDTH = A_HEADS * A_HEAD_DIM
BAND = 128
G_HEADS = 4
G_DK = 128
G_DV = 256
G_KW = G_HEADS * G_DK
G_VW = G_HEADS * G_DV
G_LOWRANK = 16
G_TAU = 16.0
EPS = 1e-6
W_IN_GLA = 3 * N_GROUPS * A_WIDTH
W_IN_LR = W_IN_GLA + 2 * G_KW + 2 * G_VW
W_IN_GATE = W_IN_LR + G_LOWRANK
D_IN = W_IN_GATE + 2 * D_MODEL

LANES = 128
GLA_BLOCK = 128
GLA_SUB = 32
FF_CHUNK = 256
N_CHUNK = 512
VMEM_LIMIT = 58 * 1024 * 1024


def _resident(shape):
    nd = len(shape)
    return pl.BlockSpec(shape, lambda *_: (0,) * nd, pipeline_mode=pl.Buffered(1))


def _params(n_axes):
    return pltpu.CompilerParams(
        dimension_semantics=("arbitrary",) * n_axes, vmem_limit_bytes=VMEM_LIMIT)


def _dot(a, b):
    return jnp.dot(a, b, preferred_element_type=F32)


def _dot_nt(a, b):
    return lax.dot_general(a, b, (((1,), (1,)), ((), ())), preferred_element_type=F32)


def _rmsnorm(x, g):
    return x * lax.rsqrt(jnp.mean(x * x, axis=-1, keepdims=True) + EPS) * g


def _swiglu_residual(x, g_ref, wg_ref, wu_ref, wd_ref, act_ref, between=None):
    h = _rmsnorm(x, g_ref[...]).astype(BF16)
    for c in range(D_FF // FF_CHUNK):
        sl = slice(c * FF_CHUNK, (c + 1) * FF_CHUNK)
        gate = _dot(h, wg_ref[:, sl].astype(BF16))
        up = _dot(h, wu_ref[:, sl].astype(BF16))
        act_ref[:, sl] = (gate * jax.nn.sigmoid(gate) * up).astype(BF16)
        if between is not None:
            between(c)
    return x + 0.5 * _dot(act_ref[...], wd_ref[...].astype(BF16))


def _ffn_kernel(xa_ref, xb_ref, g_ref, wg_ref, wu_ref, wd_ref, o_ref, act_ref, *, steps_a):
    x = jnp.where(pl.program_id(0) < steps_a, xa_ref[...], xb_ref[...])
    o_ref[...] = _swiglu_residual(x, g_ref, wg_ref, wu_ref, wd_ref, act_ref)


def _ffn(xa, xb, g, wg, wu, wd, tm):
    steps_a, steps_b = xa.shape[0] // tm, xb.shape[0] // tm
    assert xa.shape[0] % tm == 0 and xb.shape[0] % tm == 0
    return pl.pallas_call(
        functools.partial(_ffn_kernel, steps_a=steps_a),
        grid=(steps_a + steps_b,),
        in_specs=[
            pl.BlockSpec((tm, D_MODEL), lambda i: (jnp.minimum(i, steps_a - 1), 0)),
            pl.BlockSpec((tm, D_MODEL), lambda i: (jnp.maximum(i - steps_a, 0), 0)),
            _resident((1, D_MODEL)),
            _resident((D_MODEL, D_FF)),
            _resident((D_MODEL, D_FF)),
            _resident((D_FF, D_MODEL)),
        ],
        out_specs=pl.BlockSpec((tm, D_MODEL), lambda i: (i, 0)),
        out_shape=jax.ShapeDtypeStruct((xa.shape[0] + xb.shape[0], D_MODEL), F32),
        scratch_shapes=[pltpu.VMEM((tm, D_FF), BF16)],
        compiler_params=_params(1),
        name="ffn1",
    )(xa, xb, g, wg, wu, wd)


def _log_sigmoid(z):
    return jnp.minimum(z, 0.0) - jnp.log1p(jnp.exp(-jnp.abs(z)))


def _store_blocked(x, d, out_ref, perm_ref):
    tm = x.shape[0]
    if d == 1:
        for j in range(tm // BAND):
            out_ref[j] = x[j * BAND:(j + 1) * BAND].astype(BF16)
        return
    for s in range(A_WIDTH // LANES):
        perm_ref[s] = x[:, s * LANES:(s + 1) * LANES]
    for r in range(d):
        rows = jnp.concatenate(
            [perm_ref[s, pl.ds(r, tm // d, stride=d), :] for s in range(A_WIDTH // LANES)], axis=1)
        out_ref[r] = rows.astype(BF16)


def _proj_kernel(*refs, blocked, tiles_per_seq):
    (x_ref, g_ref, w_ref, up_ref, bias_ref, qgain_ref, kgain_ref, ggain_ref,
     hmean_ref) = refs[:9]
    outs = refs[9:]
    if blocked:
        qb_refs, kb_refs, vb_refs, row_refs = outs[0:3], outs[3:6], outs[6:9], outs[9:12]
        qg_ref, kg_ref, vg_ref, rg_ref, la_ref, gate_ref, perm_ref = outs[12:]
        tm = x_ref.shape[0]
    else:
        qa_ref, ka_ref, va_ref, qg_ref, kg_ref, vg_ref, rg_ref, la_ref, gate_ref = outs

    h = _rmsnorm(x_ref[...], g_ref[...]).astype(BF16)
    hmean = hmean_ref[...]

    aw = N_GROUPS * A_WIDTH
    groups = range(N_GROUPS)

    def proj(lo, width):
        return _dot_nt(h, w_ref[lo:lo + width, :])

    q_raw = [proj(g * A_WIDTH, A_WIDTH) for g in groups]
    k_raw = [proj(aw + g * A_WIDTH, A_WIDTH) for g in groups]
    vals = [proj(2 * aw + g * A_WIDTH, A_WIDTH) for g in groups]

    qg_ref[...] = (proj(W_IN_GLA, G_KW) * (G_DK ** -0.5)).astype(BF16)
    kg_ref[...] = proj(W_IN_GLA + G_KW, G_KW).astype(BF16)

    q_ms = [_dot((x * x).astype(BF16), hmean) for x in q_raw]
    k_ms = [_dot((x * x).astype(BF16), hmean) for x in k_raw]

    def attn_epilogue(g):
        d = SWA_DILATIONS[g]
        sl = slice(g * A_WIDTH, (g + 1) * A_WIDTH)
        q = q_raw[g] * lax.rsqrt(q_ms[g] + EPS) * qgain_ref[:, sl] * (A_HEAD_DIM ** -0.5)
        k = k_raw[g] * lax.rsqrt(k_ms[g] + EPS) * kgain_ref[:, sl]
        v = vals[g]
        if blocked:
            for i, (x, o_refs) in enumerate(((q, qb_refs), (k, kb_refs), (v, vb_refs))):
                _store_blocked(x, d, o_refs[g], perm_ref.at[(g - 1) * 3 + i] if d > 1 else None)
            keep = min(BAND * d, tm)
            row_refs[g][0] = k[tm - keep:, :].T
            row_refs[g][1] = v[tm - keep:, :].T
        else:
            ka_ref[:, sl] = k
            va_ref[:, sl] = v
            qa_ref[:, sl] = q.astype(BF16)

    n_gate = 2 * D_MODEL // N_CHUNK
    for c in range(n_gate):
        sl = slice(c * N_CHUNK, (c + 1) * N_CHUNK)
        gate_ref[:, sl] = jax.nn.sigmoid(proj(W_IN_GATE + c * N_CHUNK, N_CHUNK)).astype(BF16)
        if c < N_GROUPS:
            attn_epilogue(c)
    assert n_gate >= N_GROUPS

    for c in range(G_VW // N_CHUNK):
        sl = slice(c * N_CHUNK, (c + 1) * N_CHUNK)
        off = W_IN_GLA + 2 * G_KW + c * N_CHUNK
        vg_ref[:, sl] = proj(off, N_CHUNK).astype(BF16)
        r = proj(G_VW + off, N_CHUNK)
        rg_ref[:, sl] = (r * jax.nn.sigmoid(r) * ggain_ref[:, sl]).astype(BF16)

    lr = proj(W_IN_LR, LANES).astype(BF16)
    z = _dot(lr, up_ref[...]) + bias_ref[...]
    la_ref[...] = _log_sigmoid(z) * (1.0 / G_TAU)


def _unit_spec(d, tm):
    per_class = tm // d
    if per_class >= BAND:
        return pl.BlockSpec((tm // BAND, BAND, A_WIDTH), lambda i: (i, 0, 0))
    parts = BAND // per_class
    return pl.BlockSpec((d, per_class, A_WIDTH), lambda i: (i // parts, i % parts, 0))


def _row_spec(d, tm, tiles_per_seq):
    window = BAND * d
    if window <= tm:
        return pl.BlockSpec((None, 2, A_WIDTH, window), lambda i: (i // tiles_per_seq, 0, 0, 0))
    first_tile = tiles_per_seq - window // tm
    return pl.BlockSpec(
        (None, 2, A_WIDTH, tm),
        lambda i: (i // tiles_per_seq, 0, 0, jnp.maximum(i % tiles_per_seq - first_tile, 0)))


def _proj(x1, g, w_in_t, up, bias, qgain, kgain, ggain, hmean, tm, rows, seq=None):
    t = rows.stop - rows.start
    assert rows.start % tm == 0 and t % tm == 0
    block0 = rows.start // tm
    aw = N_GROUPS * A_WIDTH
    blocked = seq is not None
    widths = [(G_KW, BF16), (G_KW, BF16), (G_VW, BF16), (G_VW, BF16), (G_KW, F32),
              (2 * D_MODEL, BF16)]
    scratch = []
    tiles_per_seq = None
    if blocked:
        assert tm == 4 * BAND and SWA_DILATIONS == (1, 4, 16)
        assert seq % tm == 0 and BAND * SWA_DILATIONS[-1] <= seq
        tiles_per_seq = seq // tm
        out_specs = ([_unit_spec(d, tm) for _ in range(3) for d in SWA_DILATIONS]
                     + [_row_spec(d, tm, tiles_per_seq) for d in SWA_DILATIONS])
        out_shape = ([jax.ShapeDtypeStruct((t // BAND, BAND, A_WIDTH), BF16)] * 9
                     + [jax.ShapeDtypeStruct((t // seq, 2, A_WIDTH, BAND * d), F32)
                        for d in SWA_DILATIONS])
        scratch = [pltpu.VMEM((3 * (N_GROUPS - 1), A_WIDTH // LANES, tm, LANES), F32)]
    else:
        widths = [(aw, BF16), (aw, F32), (aw, F32)] + widths
        out_specs, out_shape = [], []
    out_specs = out_specs + [pl.BlockSpec((tm, w), lambda i: (i, 0)) for w, _ in widths]
    out_shape = out_shape + [jax.ShapeDtypeStruct((t, w), dt) for w, dt in widths]
    assert w_in_t.shape == (D_IN, D_MODEL) and W_IN_GATE % 16 == 0
    consts = [g, w_in_t, up, bias, qgain, kgain, ggain, hmean]
    return pl.pallas_call(
        functools.partial(_proj_kernel, blocked=blocked, tiles_per_seq=tiles_per_seq),
        grid=(t // tm,),
        in_specs=[pl.BlockSpec((tm, D_MODEL), lambda i: (i + block0, 0))]
        + [_resident(c.shape) for c in consts],
        out_specs=out_specs,
        out_shape=out_shape,
        scratch_shapes=scratch,
        compiler_params=_params(1),
        name="in_proj",
    )(x1, *consts)


def _log2(n):
    assert n > 0 and n & (n - 1) == 0, n
    return n.bit_length() - 1


def _head_masks(rows):
    lane = lax.broadcasted_iota(jnp.int32, (rows, A_WIDTH), 1)
    return [(lane >> _log2(A_HEAD_DIM)) == h for h in range(A_HEADS)]


ATTN_SPAN = 2048
ATTN_UNROLL = 4


def _attn_prompt_kernel(*refs, units):
    q_refs, k_refs, v_refs = refs[0:3], refs[3:6], refs[6:9]
    o_ref, osc_ref, lsc_ref, bias_ref = refs[9:]
    span = pl.program_id(1)
    n_slab = A_WIDTH // LANES

    row = lax.broadcasted_iota(jnp.int32, (A_HEADS * BAND, 2 * BAND), 0) & (BAND - 1)
    col = lax.broadcasted_iota(jnp.int32, (A_HEADS * BAND, 2 * BAND), 1)
    own = (col >= BAND) & (col - BAND <= row)
    bias_ref[0] = jnp.where(own, 0.0, -jnp.inf)
    bias_ref[1] = jnp.where(own | ((col < BAND) & (col >= row)), 0.0, -jnp.inf)

    for g, d in enumerate(SWA_DILATIONS):
        q_ref, k_ref, v_ref = q_refs[g], k_refs[g], v_refs[g]

        def scores(ul, d=d, q_ref=q_ref, k_ref=k_ref):
            u = span * units + ul
            up = jnp.maximum(u - d, 0)
            kb = jnp.concatenate([k_ref[up], k_ref[u]], axis=0)
            q = q_ref[ul]
            qm = jnp.concatenate(
                [q * m.astype(F32).astype(BF16) for m in _head_masks(BAND)], axis=0)
            has_prev = jnp.where(u >= d, 1, 0)
            return _dot_nt(qm, kb) + bias_ref[has_prev]

        def softmax(s):
            m = jnp.max(s, axis=-1, keepdims=True)
            p = jnp.exp(s - m)
            return p.astype(BF16), m, jnp.sum(p, axis=-1, keepdims=True)

        def values(ul, p, d=d, v_ref=v_ref):
            u = span * units + ul
            vb = jnp.concatenate([v_ref[jnp.maximum(u - d, 0)], v_ref[u]], axis=0)
            return _dot(p, vb)

        def store(ul, o_all, m, den, d=d, g=g):
            start = (ul >> _log2(d)) * (BAND * d) + (ul & (d - 1))
            if d == 1:
                rows = pl.ds(pl.multiple_of(start, BAND), BAND)
            else:
                rows = pl.ds(start, BAND, stride=d)
            per_slab = LANES // A_HEAD_DIM
            first = lax.broadcasted_iota(jnp.int32, (BAND, LANES), 1) < A_HEAD_DIM
            assert per_slab == 2
            for sl in range(n_slab):
                lanes = slice(sl * LANES, (sl + 1) * LANES)
                ra, rb = [slice(h * BAND, (h + 1) * BAND)
                          for h in (per_slab * sl, per_slab * sl + 1)]
                den_s = jnp.where(first, den[ra], den[rb])
                o_s = jnp.where(first, o_all[ra, lanes], o_all[rb, lanes])
                osc_ref[g, sl, rows, :] = o_s * (1.0 / den_s)
                lsc_ref[g, sl, rows, :] = jnp.where(first, m[ra], m[rb]) + jnp.log(den_s)

        def unit_group(i, carry, scores=scores, softmax=softmax, values=values, store=store):
            uls = [ATTN_UNROLL * i + j for j in range(ATTN_UNROLL)]
            ss = [scores(ul) for ul in uls]
            sm = [softmax(s) for s in ss]
            os_ = [values(ul, p) for ul, (p, _, _) in zip(uls, sm)]
            for ul, o_all, (_, m, den) in zip(uls, os_, sm):
                store(ul, o_all, m, den)
            return carry

        assert units % ATTN_UNROLL == 0
        lax.fori_loop(0, units // ATTN_UNROLL, unit_group, 0)

    def merge(i, carry):
        rows = pl.ds(pl.multiple_of(i * BAND, BAND), BAND)

        def natural(ref, g):
            return jnp.concatenate([ref[g, sl, rows, :] for sl in range(n_slab)], axis=1)

        outs = [natural(osc_ref, g) for g in range(N_GROUPS)]
        lses = [natural(lsc_ref, g) for g in range(N_GROUPS)]
        m = functools.reduce(jnp.maximum, lses)
        ws = [jnp.exp(l - m) for l in lses]
        a = functools.reduce(jnp.add, [w * o for w, o in zip(ws, outs)])
        o_ref[rows, :] = (a * (1.0 / functools.reduce(jnp.add, ws))).astype(BF16)
        return carry

    lax.fori_loop(0, ATTN_SPAN // BAND, merge, 0)


def _attn_prompt(qkv_units, batch, seq):
    assert seq % ATTN_SPAN == 0
    units = ATTN_SPAN // BAND
    per_seq = seq // BAND
    spans = seq // ATTN_SPAN
    q_spec = pl.BlockSpec((units, BAND, A_WIDTH), lambda b, s: (b * spans + s, 0, 0))
    kv_spec = pl.BlockSpec((per_seq, BAND, A_WIDTH), lambda b, s: (b, 0, 0))
    return pl.pallas_call(
        functools.partial(_attn_prompt_kernel, units=units),
        grid=(batch, spans),
        in_specs=[q_spec] * 3 + [kv_spec] * 6,
        out_specs=pl.BlockSpec((ATTN_SPAN, A_WIDTH), lambda b, s: (b * spans + s, 0)),
        out_shape=jax.ShapeDtypeStruct((batch * seq, A_WIDTH), BF16),
        scratch_shapes=[pltpu.VMEM((N_GROUPS, A_WIDTH // LANES, ATTN_SPAN, LANES), F32)] * 2
        + [pltpu.VMEM((2, A_HEADS * BAND, 2 * BAND), F32)],
        compiler_params=_params(2),
        name="attn_prompt",
    )(*qkv_units)


def _attn_sample_stages(q_ref, k_ref, v_ref, c1_ref, c4_ref, c16_ref, o_ref, *, t_new, bb):
    hmask = _head_masks(t_new)
    rows = A_HEADS * t_new
    row_t = lax.broadcasted_iota(jnp.int32, (rows, LANES), 0) & (t_new - 1)
    col = lax.broadcasted_iota(jnp.int32, (rows, LANES), 1)
    pad = jnp.zeros((LANES - t_new, A_WIDTH), F32)
    caches = (c1_ref, c4_ref, c16_ref)
    jobs = [(i, g) for i in range(bb) for g in range(N_GROUPS)]

    ok_old, ok_new = [], []
    for d, c_ref in zip(SWA_DILATIONS, caches):
        window = c_ref.shape[-1]
        assert window == BAND * d, (window, d)
        pos_t = lax.broadcasted_iota(jnp.int32, (rows, window), 0) & (t_new - 1)
        pos = lax.broadcasted_iota(jnp.int32, (rows, window), 1)
        ok_old.append((pos >= pos_t) & (((pos - pos_t) & (d - 1)) == 0))
        ok_new.append((col <= row_t) & (((row_t - col) & (d - 1)) == 0))

    def new_rows(ref, i, g):
        sl = slice(g * A_WIDTH, (g + 1) * A_WIDTH)
        return jnp.concatenate([ref[i, :, sl], pad], axis=0).astype(BF16)

    def masked_q(i, g):
        q = q_ref[i, :, g * A_WIDTH:(g + 1) * A_WIDTH].astype(F32)
        return jnp.concatenate([q * m.astype(F32) for m in hmask], axis=0).astype(BF16)

    qms = [masked_q(i, g) for i, g in jobs]
    s_old = [jnp.where(ok_old[g], _dot(qm, caches[g][i, 0].astype(BF16)), -jnp.inf)
             for qm, (i, g) in zip(qms, jobs)]
    s_new = [jnp.where(ok_new[g], _dot_nt(qm, new_rows(k_ref, i, g)), -jnp.inf)
             for qm, (i, g) in zip(qms, jobs)]
    yield
    ms = [jnp.maximum(jnp.max(so, axis=-1, keepdims=True), jnp.max(sn, axis=-1, keepdims=True))
          for so, sn in zip(s_old, s_new)]
    p_old = [jnp.exp(so - m) for so, m in zip(s_old, ms)]
    p_new = [jnp.exp(sn - m) for sn, m in zip(s_new, ms)]
    dens = [jnp.sum(po, axis=-1, keepdims=True) + jnp.sum(pn, axis=-1, keepdims=True)
            for po, pn in zip(p_old, p_new)]
    yield
    accs = [_dot_nt(po.astype(BF16), caches[g][i, 1].astype(BF16))
            + _dot(pn.astype(BF16), new_rows(v_ref, i, g))
            for po, pn, (i, g) in zip(p_old, p_new, jobs)]
    yield

    for i in range(bb):
        num = jnp.zeros((t_new, A_WIDTH), F32)
        den_all = jnp.zeros((t_new, A_WIDTH), F32)
        m_all = jnp.full((t_new, A_WIDTH), -jnp.inf, F32)
        for g in range(N_GROUPS):
            j = jobs.index((i, g))
            o_g = jnp.zeros((t_new, A_WIDTH), F32)
            lse_g = jnp.zeros((t_new, A_WIDTH), F32)
            for h in range(A_HEADS):
                rs = slice(h * t_new, (h + 1) * t_new)
                o_g = jnp.where(hmask[h], accs[j][rs] * (1.0 / dens[j][rs]), o_g)
                lse_g = jnp.where(hmask[h], ms[j][rs] + jnp.log(dens[j][rs]), lse_g)
            m_next = jnp.maximum(m_all, lse_g)
            scale_old = jnp.exp(m_all - m_next)
            w = jnp.exp(lse_g - m_next)
            num = num * scale_old + w * o_g
            den_all = den_all * scale_old + w
            m_all = m_next
        o_ref[i] = (num * (1.0 / den_all)).astype(BF16)


def _split2(x):
    hi = x.astype(BF16)
    return hi, (x - hi.astype(F32)).astype(BF16)


def _gla_out(o, gate):
    return (o * lax.rsqrt(jnp.mean(o * o, axis=-1, keepdims=True) + EPS) * gate).astype(BF16)


def _col_bcast(vec_row, width):
    sq = jnp.broadcast_to(vec_row, (LANES, LANES)).T
    return jnp.concatenate([sq] * (width // LANES), axis=1)


GLA_CHUNK = 512
GLA_SEQS = 4


def _gla_prompt_kernel(q_ref, k_ref, v_ref, la_ref, rg_ref, o_ref, s_ref, *, n_blocks):
    cb, sub = GLA_BLOCK, GLA_SUB
    n_sub = cb // sub
    n_seq = q_ref.shape[0]

    @pl.when(pl.program_id(1) == 0)
    def _():
        s_ref[...] = jnp.zeros_like(s_ref)

    jobs = [(n, h) for n in range(n_seq) for h in range(G_HEADS)]

    def hk(x, h):
        return x[:, h * G_DK:(h + 1) * G_DK]

    def rows_at(x, lo_row):
        pieces = [jnp.zeros((lo_row, x.shape[1]), BF16)] if lo_row else []
        pieces.append(x)
        rest = cb - lo_row - x.shape[0]
        if rest:
            pieces.append(jnp.zeros((rest, x.shape[1]), BF16))
        return jnp.concatenate(pieces, axis=0) if len(pieces) > 1 else x

    def decays(n, rows, tri):
        hi, lo = _split2(la_ref[n, rows, :])
        b = _dot(tri, hi) + _dot(tri, lo)
        q = q_ref[n, rows, :].astype(F32)
        k = k_ref[n, rows, :].astype(F32)
        b_last = b[cb - 1:cb, :]
        qe = (q * jnp.exp(b)).astype(BF16)
        sub_end = [b[(j + 1) * sub - 1:(j + 1) * sub, :] for j in range(n_sub)]
        own_end = jnp.concatenate(
            [jnp.broadcast_to(e, (sub, G_KW)) for e in sub_end], axis=0)
        kk = (k * jnp.exp(own_end - b)).astype(BF16)
        q_parts, k_parts = [], []
        for j in range(n_sub):
            lo_row = j * sub
            qj = q[lo_row:, :] * jnp.exp(b[lo_row:, :] - sub_end[j])
            q_parts.append(rows_at(qj.astype(BF16), lo_row))
            k_parts.append(rows_at(kk[lo_row:lo_row + sub], lo_row))
        return qe, q_parts, k_parts, k * jnp.exp(b_last - b), jnp.exp(b_last)

    def step(i, carry):
        rows = pl.ds(pl.multiple_of(i * cb, cb), cb)
        row = lax.broadcasted_iota(jnp.int32, (cb, cb), 0)
        col = lax.broadcasted_iota(jnp.int32, (cb, cb), 1)
        tri = (row >= col).astype(F32).astype(BF16)

        per_seq = [decays(n, rows, tri) for n in range(n_seq)]
        states = [s_ref[n, h] for n, h in jobs]
        vals = [v_ref[n, rows, h * G_DV:(h + 1) * G_DV] for n, h in jobs]
        inter = [_dot(hk(per_seq[n][0], h), s.astype(BF16)) for s, (n, h) in zip(states, jobs)]
        atts = [_dot_nt(jnp.concatenate([hk(p, h) for p in per_seq[n][1]], axis=1),
                        jnp.concatenate([hk(p, h) for p in per_seq[n][2]], axis=1))
                for n, h in jobs]
        atts = [jnp.where(col <= row, a, 0.0).astype(BF16) for a in atts]
        outs = [x + _dot(a, v) for x, a, v in zip(inter, atts, vals)]
        for s, v, (n, h) in zip(states, vals, jobs):
            kd_t = hk(per_seq[n][3], h).T.astype(BF16)
            s_ref[n, h] = s * _col_bcast(hk(per_seq[n][4], h), G_DV) + _dot(kd_t, v)
        for o, (n, h) in zip(outs, jobs):
            vs = slice(h * G_DV, (h + 1) * G_DV)
            o_ref[n, rows, vs] = _gla_out(o, rg_ref[n, rows, vs].astype(F32))
        return carry

    lax.fori_loop(0, n_blocks, step, 0)


def _gla_prompt(qg, kg, vg, la, rg, batch, seq):
    assert seq % GLA_CHUNK == 0 and batch % GLA_SEQS == 0
    chunks = seq // GLA_CHUNK

    def spec(width):
        return pl.BlockSpec((GLA_SEQS, GLA_CHUNK, width), lambda b, c: (b, c, 0))

    def per_seq(x):
        return x.reshape(batch, seq, x.shape[-1])
    o, s = pl.pallas_call(
        functools.partial(_gla_prompt_kernel, n_blocks=GLA_CHUNK // GLA_BLOCK),
        grid=(batch // GLA_SEQS, chunks),
        in_specs=[spec(G_KW), spec(G_KW), spec(G_VW), spec(G_KW), spec(G_VW)],
        out_specs=[spec(G_VW),
                   pl.BlockSpec((GLA_SEQS, G_HEADS, G_DK, G_DV), lambda b, c: (b, 0, 0, 0))],
        out_shape=[jax.ShapeDtypeStruct((batch, seq, G_VW), BF16),
                   jax.ShapeDtypeStruct((batch, G_HEADS, G_DK, G_DV), F32)],
        compiler_params=_params(2),
        name="gla_prompt",
    )(per_seq(qg), per_seq(kg), per_seq(vg), per_seq(la), per_seq(rg))
    return o.reshape(batch * seq, G_VW), s


def _gla_sample_kernel(q_ref, k_ref, v_ref, la_ref, rg_ref, s0_ref, o_ref, s_ref,
                       *, bb, t_new):
    per_body = 4

    def hk(x, h):
        return x[:, h * G_DK:(h + 1) * G_DK]

    def hv(x, h):
        return x[:, h * G_DV:(h + 1) * G_DV]

    def body(i0, carry):
        trow = lax.broadcasted_iota(jnp.int32, (t_new, G_KW), 0)
        arow = lax.broadcasted_iota(jnp.int32, (t_new, LANES), 0)
        acol = lax.broadcasted_iota(jnp.int32, (t_new, LANES), 1)
        zk = jnp.zeros((LANES - t_new, G_KW), F32)
        zv = jnp.zeros((LANES - t_new, G_VW), F32)
        seqs = [per_body * i0 + j for j in range(per_body)]
        jobs = [(n, h) for n in range(per_body) for h in range(G_HEADS)]

        bs = []
        for i in seqs:
            b = la_ref[i]
            shift = 1
            while shift < t_new:
                b = b + jnp.where(trow >= shift, pltpu.roll(b, shift, 0), 0.0)
                shift *= 2
            bs.append(b)
        qs = [q_ref[i].astype(F32) for i in seqs]
        ks = [k_ref[i].astype(F32) for i in seqs]
        vs = [jnp.concatenate([v_ref[i].astype(F32), zv], axis=0).astype(BF16) for i in seqs]
        b_last = [b[t_new - 1:t_new, :] for b in bs]
        qe = [(q * jnp.exp(b)).astype(BF16) for q, b in zip(qs, bs)]
        ke = [jnp.concatenate([k * jnp.exp(-b), zk], axis=0).astype(BF16) for k, b in zip(ks, bs)]
        kd = [jnp.concatenate([k * jnp.exp(bl - b), zk], axis=0)
              for k, b, bl in zip(ks, bs, b_last)]
        decay = [jnp.exp(bl) for bl in b_last]

        s0 = [s0_ref[seqs[n], h] for n, h in jobs]
        att = [jnp.where(acol <= arow, _dot_nt(hk(qe[n], h), hk(ke[n], h)), 0.0).astype(BF16)
               for n, h in jobs]
        out = [_dot(a, hv(vs[n], h)) + _dot(hk(qe[n], h), s.astype(BF16))
               for a, s, (n, h) in zip(att, s0, jobs)]
        for s, (n, h) in zip(s0, jobs):
            s_ref[seqs[n], h] = (s * _col_bcast(hk(decay[n], h), G_DV)
                                 + _dot(hk(kd[n], h).T.astype(BF16), hv(vs[n], h)))
        for o, (n, h) in zip(out, jobs):
            sl = slice(h * G_DV, (h + 1) * G_DV)
            o_ref[seqs[n], :, sl] = _gla_out(o, rg_ref[seqs[n], :, sl].astype(F32))
        return carry

    assert bb % per_body == 0
    lax.fori_loop(0, bb // per_body, body, 0)


def _gla_sample(qg, kg, vg, la, rg, s0, batch, t_new, bb):
    def spec(width):
        return pl.BlockSpec((bb, t_new, width), lambda b: (b, 0, 0))
    st_spec = pl.BlockSpec((bb, G_HEADS, G_DK, G_DV), lambda b: (b, 0, 0, 0))
    o, s = pl.pallas_call(
        functools.partial(_gla_sample_kernel, bb=bb, t_new=t_new),
        grid=(batch // bb,),
        in_specs=[spec(G_KW), spec(G_KW), spec(G_VW), spec(G_KW), spec(G_VW), st_spec],
        out_specs=[spec(G_VW), st_spec],
        out_shape=[jax.ShapeDtypeStruct((batch, t_new, G_VW), BF16),
                   jax.ShapeDtypeStruct((batch, G_HEADS, G_DK, G_DV), F32)],
        compiler_params=_params(1),
        name="gla_sample",
    )(qg.reshape(batch, t_new, G_KW), kg.reshape(batch, t_new, G_KW),
      vg.reshape(batch, t_new, G_VW), la.reshape(batch, t_new, G_KW),
      rg.reshape(batch, t_new, G_VW), s0)
    return o.reshape(batch * t_new, G_VW), s


def _out_kernel(x_ref, a_ref, gb_ref, gate_ref, wa_ref, wb_ref, wo_ref, g_ref, wg_ref, wu_ref,
                wd_ref, o_ref, act_ref, between=None):
    a_out = _dot(a_ref[...], wa_ref[...])
    b_out = _dot(gb_ref[...], wb_ref[...])
    mix = (gate_ref[:, 0:D_MODEL].astype(F32) * a_out
           + gate_ref[:, D_MODEL:2 * D_MODEL].astype(F32) * b_out)
    x2 = x_ref[...] + _dot(mix.astype(BF16), wo_ref[...])
    o_ref[...] = _swiglu_residual(x2, g_ref, wg_ref, wu_ref, wd_ref, act_ref, between)


def _out(x1, attn, gb, gates, wa, wb, wo, g, wg, wu, wd, tm, row0=0):
    t = attn.shape[0]
    assert row0 % tm == 0 and t % tm == 0
    block0 = row0 // tm

    def tok(width):
        return pl.BlockSpec((tm, width), lambda i: (i, 0))
    consts = [wa, wb, wo, g, wg, wu, wd]
    return pl.pallas_call(
        _out_kernel,
        grid=(t // tm,),
        in_specs=[pl.BlockSpec((tm, D_MODEL), lambda i: (i + block0, 0)),
                  tok(A_WIDTH), tok(G_VW), tok(2 * D_MODEL)]
        + [_resident(c.shape) for c in consts],
        out_specs=tok(D_MODEL),
        out_shape=jax.ShapeDtypeStruct((t, D_MODEL), F32),
        scratch_shapes=[pltpu.VMEM((tm, D_FF), BF16)],
        compiler_params=_params(1),
        name="out_ffn2",
    )(x1, attn, gb, gates, *consts)


def _out_with_sample_attn_kernel(*refs, t_new, bb):
    out_in, attn_in = refs[:11], refs[11:17]
    y_ref, a_ref, act_ref = refs[17:]
    stages = _attn_sample_stages(*attn_in, a_ref, t_new=t_new, bb=bb)
    every = 3

    def between(c):
        if c % every == every - 1:
            next(stages, None)

    _out_kernel(*out_in, y_ref, act_ref, between=between)
    for _ in stages:
        pass


def _out_with_sample_attn(x1, attn, gb, gates, wa, wb, wo, g, wg, wu, wd,
                          qa, ka, va, c128, c512, c2048, batch, t_new, tm):
    t = attn.shape[0]
    steps = t // tm
    assert batch % steps == 0
    bb = batch // steps
    w3 = N_GROUPS * A_WIDTH

    def tok(width):
        return pl.BlockSpec((tm, width), lambda i: (i, 0))
    consts = [wa, wb, wo, g, wg, wu, wd]
    new_spec = pl.BlockSpec((bb, t_new, w3), lambda i: (i, 0, 0))
    cache_views = [jnp.transpose(c, (0, 2, 3, 4, 1)).reshape(batch, 2, A_WIDTH, c.shape[1])
                   for c in (c128, c512, c2048)]
    cache_specs = [pl.BlockSpec((bb, 2, A_WIDTH, c.shape[-1]), lambda i: (i, 0, 0, 0))
                   for c in cache_views]
    y, a_s = pl.pallas_call(
        functools.partial(_out_with_sample_attn_kernel, t_new=t_new, bb=bb),
        grid=(steps,),
        in_specs=[tok(D_MODEL), tok(A_WIDTH), tok(G_VW), tok(2 * D_MODEL)]
        + [_resident(c.shape) for c in consts] + [new_spec] * 3 + cache_specs,
        out_specs=[tok(D_MODEL), pl.BlockSpec((bb, t_new, A_WIDTH), lambda i: (i, 0, 0))],
        out_shape=[jax.ShapeDtypeStruct((t, D_MODEL), F32),
                   jax.ShapeDtypeStruct((batch, t_new, A_WIDTH), BF16)],
        scratch_shapes=[pltpu.VMEM((tm, D_FF), BF16)],
        compiler_params=_params(1),
        name="out_ffn2_attn_sample",
    )(x1, attn, gb, gates, *consts,
      qa.reshape(batch, t_new, w3), ka.reshape(batch, t_new, w3), va.reshape(batch, t_new, w3),
      *cache_views)
    return y, a_s.reshape(batch * t_new, A_WIDTH)


def _sample_rows_kernel(k_ref, v_ref, *refs, batch, t_new):
    o_refs, slab_ref = refs[:N_GROUPS], refs[N_GROUPS]
    n_slab = N_GROUPS * A_WIDTH // LANES
    per_group = A_WIDTH // LANES
    for kv, src in enumerate((k_ref, v_ref)):
        for s in range(n_slab):
            slab_ref[kv, s] = src[:, s * LANES:(s + 1) * LANES]
    for kv in range(2):
        for s in range(n_slab):
            g, part = divmod(s, per_group)
            for t in range(t_new):
                rows = slab_ref[kv, s, pl.ds(t, batch, stride=t_new), :]
                o_refs[g][t, kv, part * LANES:(part + 1) * LANES, :] = rows.T


def _sample_rows(ka, va, batch, t_new):
    assert batch == LANES
    n_slab = N_GROUPS * A_WIDTH // LANES
    outs = pl.pallas_call(
        functools.partial(_sample_rows_kernel, batch=batch, t_new=t_new),
        out_shape=[jax.ShapeDtypeStruct((t_new, 2, A_WIDTH, batch), F32)] * N_GROUPS,
        scratch_shapes=[pltpu.VMEM((2, n_slab, batch * t_new, LANES), F32)],
        compiler_params=pltpu.CompilerParams(vmem_limit_bytes=VMEM_LIMIT),
        name="sample_rows",
    )(ka, va)
    return [jnp.transpose(o.reshape(t_new, 2, A_HEADS, A_HEAD_DIM, batch), (4, 0, 1, 2, 3))
            for o in outs]


def _position_major(rows, batch):
    window = rows.shape[-1]
    return jnp.transpose(rows.reshape(batch, 2, A_HEADS, A_HEAD_DIM, window), (0, 4, 1, 2, 3))


def kernel(x_prompt, x_sample, cache_swa128_kv, cache_swa512_kv, cache_swa2048_kv, state_gla,
           norm_ffn1, ffn1_gate, ffn1_up, ffn1_down, norm_mix, w_in, a_q_norm, a_k_norm,
           g_alpha_up, g_alpha_bias, g_out_norm, w_a_out, w_b_out, w_out,
           norm_ffn2, ffn2_gate, ffn2_up, ffn2_down):
    batch, seq, _ = x_prompt.shape
    dec_batch, dec_seq, _ = x_sample.shape

    aw = N_GROUPS * A_WIDTH
    w_in_t = w_in.T.astype(BF16)
    up = jnp.pad(g_alpha_up, ((0, LANES - G_LOWRANK), (0, 0))).astype(BF16)
    bias = g_alpha_bias.reshape(1, G_KW)
    qgain = jnp.tile(a_q_norm, (1, A_HEADS)).reshape(1, aw)
    kgain = jnp.tile(a_k_norm, (1, A_HEADS)).reshape(1, aw)
    lane = jnp.arange(A_WIDTH) // A_HEAD_DIM
    hmean = jnp.where(lane[:, None] == lane[None, :], 1.0 / A_HEAD_DIM, 0.0).astype(BF16)
    ggain = jnp.tile(g_out_norm, G_HEADS).reshape(1, G_VW)
    ffn1 = (norm_ffn1.reshape(1, D_MODEL), ffn1_gate, ffn1_up, ffn1_down)
    ffn2 = (norm_ffn2.reshape(1, D_MODEL), ffn2_gate.astype(BF16), ffn2_up.astype(BF16),
            ffn2_down.astype(BF16))
    proj_w = (norm_mix.reshape(1, D_MODEL), w_in_t, up, bias, qgain, kgain, ggain, hmean)
    out_w = (w_a_out.astype(BF16), w_b_out.astype(BF16), w_out.astype(BF16))

    t_p, t_s = batch * seq, dec_batch * dec_seq
    x1 = _ffn(x_prompt.reshape(t_p, D_MODEL), x_sample.reshape(t_s, D_MODEL), *ffn1, tm=512)

    qa, ka, va, qg_s, kg_s, vg_s, rg_s, la_s, gates_s = _proj(
        x1, *proj_w, tm=512, rows=range(t_p, t_p + t_s))
    rows_s = _sample_rows(ka, va, dec_batch, dec_seq)

    proj_p = _proj(x1, *proj_w, tm=512, rows=range(0, t_p), seq=seq)
    qkv_units, rows_p, (qg, kg, vg, rg, la, gates) = proj_p[:9], proj_p[9:12], proj_p[12:]
    a_p = _attn_prompt(qkv_units, batch, seq)
    gbp, state_p = _gla_prompt(qg, kg, vg, la, rg, batch, seq)
    y_prompt, a_s = _out_with_sample_attn(
        x1, a_p, gbp, gates, *out_w, *ffn2, qa, ka, va,
        cache_swa128_kv, cache_swa512_kv, cache_swa2048_kv, dec_batch, dec_seq, tm=256)
    y_prompt = y_prompt.reshape(batch, seq, D_MODEL)
    rows_p = [_position_major(r, batch) for r in rows_p]

    gbs, state_s = _gla_sample(qg_s, kg_s, vg_s, la_s, rg_s, state_gla, dec_batch, dec_seq,
                               bb=8)
    y_sample = _out(x1, a_s, gbs, gates_s, *out_w, *ffn2, tm=512, row0=t_p).reshape(
        dec_batch, dec_seq, D_MODEL)

    return (y_prompt, y_sample, rows_p[0], rows_p[1], rows_p[2], state_p,
            rows_s[0], rows_s[1], rows_s[2], state_s)
```

```python
import functools

import jax
import jax.numpy as jnp
from jax import lax
from jax.experimental import pallas as pl
from jax.experimental.pallas import tpu as pltpu

F32 = jnp.float32
BF16 = jnp.bfloat16

D_MODEL = 1024
D_FF = 2816
SWA_DILATIONS = (1, 4, 16)
N_GROUPS = 3
A_HEADS = 4
A_HEAD_DIM = 64
A_WIDTH = A_HEADS * A_HEAD_DIM
BAND = 128
G_HEADS = 4
G_DK = 128
G_DV = 256
G_KW = G_HEADS * G_DK
G_VW = G_HEADS * G_DV
G_LOWRANK = 16
G_TAU = 16.0
EPS = 1e-6
W_IN_GLA = 3 * N_GROUPS * A_WIDTH
W_IN_LR = W_IN_GLA + 2 * G_KW + 2 * G_VW
W_IN_GATE = W_IN_LR + G_LOWRANK
D_IN = W_IN_GATE + 2 * D_MODEL

LANES = 128
GLA_BLOCK = 128
GLA_SUB = 32
FF_CHUNK = 256
N_CHUNK = 512
VMEM_LIMIT = 60 * 1024 * 1024


def _resident(shape):
    nd = len(shape)
    return pl.BlockSpec(shape, lambda *_: (0,) * nd, pipeline_mode=pl.Buffered(1))


def _params(n_axes):
    return pltpu.CompilerParams(
        dimension_semantics=("arbitrary",) * n_axes, vmem_limit_bytes=VMEM_LIMIT)


def _dot(a, b):
    return jnp.dot(a, b, preferred_element_type=F32)


def _dot_nt(a, b):
    return lax.dot_general(a, b, (((1,), (1,)), ((), ())), preferred_element_type=F32)


def _rmsnorm(x, g):
    return x * lax.rsqrt(jnp.mean(x * x, axis=-1, keepdims=True) + EPS) * g


def _swiglu_residual(x, g_ref, wg_ref, wu_ref, wd_ref, act_ref, between=None):
    h = _rmsnorm(x, g_ref[...]).astype(BF16)
    for c in range(D_FF // FF_CHUNK):
        sl = slice(c * FF_CHUNK, (c + 1) * FF_CHUNK)
        gate = _dot(h, wg_ref[:, sl].astype(BF16))
        up = _dot(h, wu_ref[:, sl].astype(BF16))
        act_ref[:, sl] = (gate * jax.nn.sigmoid(gate) * up).astype(BF16)
        if between is not None:
            between(c)
    return x + 0.5 * _dot(act_ref[...], wd_ref[...].astype(BF16))


def _ffn_kernel(xa_ref, xb_ref, g_ref, wg_ref, wu_ref, wd_ref, o_ref, act_ref, *, steps_a):
    x = jnp.where(pl.program_id(0) < steps_a, xa_ref[...], xb_ref[...])
    o_ref[...] = _swiglu_residual(x, g_ref, wg_ref, wu_ref, wd_ref, act_ref)


def _ffn(xa, xb, g, wg, wu, wd, tm):
    steps_a, steps_b = xa.shape[0] // tm, xb.shape[0] // tm
    assert xa.shape[0] % tm == 0 and xb.shape[0] % tm == 0
    return pl.pallas_call(
        functools.partial(_ffn_kernel, steps_a=steps_a),
        grid=(steps_a + steps_b,),
        in_specs=[
            pl.BlockSpec((tm, D_MODEL), lambda i: (jnp.minimum(i, steps_a - 1), 0)),
            pl.BlockSpec((tm, D_MODEL), lambda i: (jnp.maximum(i - steps_a, 0), 0)),
            _resident((1, D_MODEL)),
            _resident((D_MODEL, D_FF)),
            _resident((D_MODEL, D_FF)),
            _resident((D_FF, D_MODEL)),
        ],
        out_specs=pl.BlockSpec((tm, D_MODEL), lambda i: (i, 0)),
        out_shape=jax.ShapeDtypeStruct((xa.shape[0] + xb.shape[0], D_MODEL), F32),
        scratch_shapes=[pltpu.VMEM((tm, D_FF), BF16)],
        compiler_params=_params(1),
        name="ffn1",
    )(xa, xb, g, wg, wu, wd)


def _log_sigmoid(z):
    return jnp.minimum(z, 0.0) - jnp.log1p(jnp.exp(-jnp.abs(z)))


def _store_blocked(x, d, out_ref, perm_ref):
    tm = x.shape[0]
    if d == 1:
        for j in range(tm // BAND):
            out_ref[j] = x[j * BAND:(j + 1) * BAND].astype(BF16)
        return
    for s in range(A_WIDTH // LANES):
        perm_ref[s] = x[:, s * LANES:(s + 1) * LANES]
    for r in range(d):
        rows = jnp.concatenate(
            [perm_ref[s, pl.ds(r, tm // d, stride=d), :] for s in range(A_WIDTH // LANES)], axis=1)
        out_ref[r] = rows.astype(BF16)


def _proj_kernel(*refs, blocked, tiles_per_seq):
    (x_ref, g_ref, w_ref, up_ref, bias_ref, qgain_ref, kgain_ref, ggain_ref,
     hmean_ref) = refs[:9]
    outs = refs[9:]
    if blocked:
        qb_refs, kb_refs, vb_refs, row_refs = outs[0:3], outs[3:6], outs[6:9], outs[9:12]
        qg_ref, kg_ref, vg_ref, rg_ref, la_ref, gate_ref, perm_ref = outs[12:]
        tm = x_ref.shape[0]
    else:
        qa_ref, ka_ref, va_ref, qg_ref, kg_ref, vg_ref, rg_ref, la_ref, gate_ref = outs

    h = _rmsnorm(x_ref[...], g_ref[...]).astype(BF16)
    hmean = hmean_ref[...]

    aw = N_GROUPS * A_WIDTH
    groups = range(N_GROUPS)

    def proj(lo, width):
        return _dot_nt(h, w_ref[lo:lo + width, :])

    q_raw = [proj(g * A_WIDTH, A_WIDTH) for g in groups]
    k_raw = [proj(aw + g * A_WIDTH, A_WIDTH) for g in groups]
    vals = [proj(2 * aw + g * A_WIDTH, A_WIDTH) for g in groups]

    qg_ref[...] = (proj(W_IN_GLA, G_KW) * (G_DK ** -0.5)).astype(BF16)
    kg_ref[...] = proj(W_IN_GLA + G_KW, G_KW).astype(BF16)

    q_ms = [_dot((x * x).astype(BF16), hmean) for x in q_raw]
    k_ms = [_dot((x * x).astype(BF16), hmean) for x in k_raw]

    def attn_epilogue(g):
        d = SWA_DILATIONS[g]
        sl = slice(g * A_WIDTH, (g + 1) * A_WIDTH)
        q = q_raw[g] * lax.rsqrt(q_ms[g] + EPS) * qgain_ref[:, sl] * (A_HEAD_DIM ** -0.5)
        k = k_raw[g] * lax.rsqrt(k_ms[g] + EPS) * kgain_ref[:, sl]
        v = vals[g]
        if blocked:
            for i, (x, o_refs) in enumerate(((q, qb_refs), (k, kb_refs), (v, vb_refs))):
                _store_blocked(x, d, o_refs[g], perm_ref.at[(g - 1) * 3 + i] if d > 1 else None)
            keep = min(BAND * d, tm)
            row_refs[g][0] = k[tm - keep:, :].T
            row_refs[g][1] = v[tm - keep:, :].T
        else:
            ka_ref[:, sl] = k
            va_ref[:, sl] = v
            qa_ref[:, sl] = q.astype(BF16)

    n_gate = 2 * D_MODEL // N_CHUNK
    for c in range(n_gate):
        sl = slice(c * N_CHUNK, (c + 1) * N_CHUNK)
        gate_ref[:, sl] = jax.nn.sigmoid(proj(W_IN_GATE + c * N_CHUNK, N_CHUNK)).astype(BF16)
        if c < N_GROUPS:
            attn_epilogue(c)
    assert n_gate >= N_GROUPS

    for c in range(G_VW // N_CHUNK):
        sl = slice(c * N_CHUNK, (c + 1) * N_CHUNK)
        off = W_IN_GLA + 2 * G_KW + c * N_CHUNK
        vg_ref[:, sl] = proj(off, N_CHUNK).astype(BF16)
        r = proj(G_VW + off, N_CHUNK)
        rg_ref[:, sl] = (r * jax.nn.sigmoid(r) * ggain_ref[:, sl]).astype(BF16)

    lr = proj(W_IN_LR, LANES).astype(BF16)
    z = _dot(lr, up_ref[...]) + bias_ref[...]
    la_ref[...] = _log_sigmoid(z) * (1.0 / G_TAU)


def _unit_spec(d, tm):
    per_class = tm // d
    if per_class >= BAND:
        return pl.BlockSpec((tm // BAND, BAND, A_WIDTH), lambda i: (i, 0, 0))
    parts = BAND // per_class
    return pl.BlockSpec((d, per_class, A_WIDTH), lambda i: (i // parts, i % parts, 0))


def _row_spec(d, tm, tiles_per_seq):
    window = BAND * d
    if window <= tm:
        return pl.BlockSpec((None, 2, A_WIDTH, window), lambda i: (i // tiles_per_seq, 0, 0, 0))
    first_tile = tiles_per_seq - window // tm
    return pl.BlockSpec(
        (None, 2, A_WIDTH, tm),
        lambda i: (i // tiles_per_seq, 0, 0, jnp.maximum(i % tiles_per_seq - first_tile, 0)))


def _proj(x1, g, w_in_t, up, bias, qgain, kgain, ggain, hmean, tm, rows, seq=None, rider=None):
    t = rows.stop - rows.start
    assert rows.start % tm == 0 and t % tm == 0
    block0 = rows.start // tm
    aw = N_GROUPS * A_WIDTH
    blocked = seq is not None
    widths = [(G_KW, BF16), (G_KW, BF16), (G_VW, BF16), (G_VW, BF16), (G_KW, F32),
              (2 * D_MODEL, BF16)]
    scratch = []
    tiles_per_seq = None
    if blocked:
        assert tm == 4 * BAND and SWA_DILATIONS == (1, 4, 16)
        assert seq % tm == 0 and BAND * SWA_DILATIONS[-1] <= seq
        tiles_per_seq = seq // tm
        out_specs = ([_unit_spec(d, tm) for _ in range(3) for d in SWA_DILATIONS]
                     + [_row_spec(d, tm, tiles_per_seq) for d in SWA_DILATIONS])
        out_shape = ([jax.ShapeDtypeStruct((t // BAND, BAND, A_WIDTH), BF16)] * 9
                     + [jax.ShapeDtypeStruct((t // seq, 2, A_WIDTH, BAND * d), F32)
                        for d in SWA_DILATIONS])
        scratch = [pltpu.VMEM((3 * (N_GROUPS - 1), A_WIDTH // LANES, tm, LANES), F32)]
    else:
        widths = [(aw, BF16), (aw, F32), (aw, F32)] + widths
        out_specs, out_shape = [], []
    out_specs = out_specs + [pl.BlockSpec((tm, w), lambda i: (i, 0)) for w, _ in widths]
    out_shape = out_shape + [jax.ShapeDtypeStruct((t, w), dt) for w, dt in widths]
    assert w_in_t.shape == (D_IN, D_MODEL) and W_IN_GATE % 16 == 0
    consts = [g, w_in_t, up, bias, qgain, kgain, ggain, hmean]
    r_body, r_arrays, r_in_specs, r_out_specs, r_out_shapes = rider or (None, [], [], [], [])
    n_in, n_out, n_rin = 1 + len(consts), len(out_specs), len(r_arrays)

    def body(*refs):
        ins, r_ins, rest = refs[:n_in], refs[n_in:n_in + n_rin], refs[n_in + n_rin:]
        outs, r_outs, scr = rest[:n_out], rest[n_out:n_out + len(r_out_specs)], rest[
            n_out + len(r_out_specs):]
        if r_body is not None:
            r_body(*r_ins, *r_outs)
        _proj_kernel(*ins, *outs, *scr, blocked=blocked, tiles_per_seq=tiles_per_seq)

    return pl.pallas_call(
        body,
        grid=(t // tm,),
        in_specs=[pl.BlockSpec((tm, D_MODEL), lambda i: (i + block0, 0))]
        + [_resident(c.shape) for c in consts] + list(r_in_specs),
        out_specs=out_specs + list(r_out_specs),
        out_shape=out_shape + list(r_out_shapes),
        scratch_shapes=scratch,
        compiler_params=_params(1),
        name="in_proj" if rider is None else "in_proj_gla_sample",
    )(x1, *consts, *r_arrays)


def _log2(n):
    assert n > 0 and n & (n - 1) == 0, n
    return n.bit_length() - 1


def _head_masks(rows):
    lane = lax.broadcasted_iota(jnp.int32, (rows, A_WIDTH), 1)
    return [(lane >> _log2(A_HEAD_DIM)) == h for h in range(A_HEADS)]


ATTN_SPAN = 2048
ATTN_UNROLL = 4


def _attn_prompt_kernel(*refs, units):
    q_refs, k_refs, v_refs = refs[0:3], refs[3:6], refs[6:9]
    o_ref, osc_ref, lsc_ref, bias_ref = refs[9:]
    span = pl.program_id(1)
    n_slab = A_WIDTH // LANES

    row = lax.broadcasted_iota(jnp.int32, (A_HEADS * BAND, 2 * BAND), 0) & (BAND - 1)
    col = lax.broadcasted_iota(jnp.int32, (A_HEADS * BAND, 2 * BAND), 1)
    own = (col >= BAND) & (col - BAND <= row)
    bias_ref[0] = jnp.where(own, 0.0, -jnp.inf)
    bias_ref[1] = jnp.where(own | ((col < BAND) & (col >= row)), 0.0, -jnp.inf)

    for g, d in enumerate(SWA_DILATIONS):
        q_ref, k_ref, v_ref = q_refs[g], k_refs[g], v_refs[g]

        def scores(ul, d=d, q_ref=q_ref, k_ref=k_ref):
            u = span * units + ul
            up = jnp.maximum(u - d, 0)
            kb = jnp.concatenate([k_ref[up], k_ref[u]], axis=0)
            q = q_ref[ul]
            qm = jnp.concatenate(
                [q * m.astype(F32).astype(BF16) for m in _head_masks(BAND)], axis=0)
            has_prev = jnp.where(u >= d, 1, 0)
            return _dot_nt(qm, kb) + bias_ref[has_prev]

        def softmax(s):
            m = jnp.max(s, axis=-1, keepdims=True)
            p = jnp.exp(s - m)
            return p.astype(BF16), m, jnp.sum(p, axis=-1, keepdims=True)

        def values(ul, p, d=d, v_ref=v_ref):
            u = span * units + ul
            vb = jnp.concatenate([v_ref[jnp.maximum(u - d, 0)], v_ref[u]], axis=0)
            return _dot(p, vb)

        def store(ul, o_all, m, den, d=d, g=g):
            start = (ul >> _log2(d)) * (BAND * d) + (ul & (d - 1))
            if d == 1:
                rows = pl.ds(pl.multiple_of(start, BAND), BAND)
            else:
                rows = pl.ds(start, BAND, stride=d)
            per_slab = LANES // A_HEAD_DIM
            first = lax.broadcasted_iota(jnp.int32, (BAND, LANES), 1) < A_HEAD_DIM
            assert per_slab == 2
            for sl in range(n_slab):
                lanes = slice(sl * LANES, (sl + 1) * LANES)
                ra, rb = [slice(h * BAND, (h + 1) * BAND)
                          for h in (per_slab * sl, per_slab * sl + 1)]
                den_s = jnp.where(first, den[ra], den[rb])
                o_s = jnp.where(first, o_all[ra, lanes], o_all[rb, lanes])
                osc_ref[g, sl, rows, :] = o_s * (1.0 / den_s)
                lsc_ref[g, sl, rows, :] = jnp.where(first, m[ra], m[rb]) + jnp.log(den_s)

        def unit_group(i, carry, scores=scores, softmax=softmax, values=values, store=store):
            uls = [ATTN_UNROLL * i + j for j in range(ATTN_UNROLL)]
            ss = [scores(ul) for ul in uls]
            sm = [softmax(s) for s in ss]
            os_ = [values(ul, p) for ul, (p, _, _) in zip(uls, sm)]
            for ul, o_all, (_, m, den) in zip(uls, os_, sm):
                store(ul, o_all, m, den)
            return carry

        assert units % ATTN_UNROLL == 0
        lax.fori_loop(0, units // ATTN_UNROLL, unit_group, 0)

    def merge(i, carry):
        rows = pl.ds(pl.multiple_of(i * BAND, BAND), BAND)

        def natural(ref, g):
            return jnp.concatenate([ref[g, sl, rows, :] for sl in range(n_slab)], axis=1)

        outs = [natural(osc_ref, g) for g in range(N_GROUPS)]
        lses = [natural(lsc_ref, g) for g in range(N_GROUPS)]
        m = functools.reduce(jnp.maximum, lses)
        ws = [jnp.exp(l - m) for l in lses]
        a = functools.reduce(jnp.add, [w * o for w, o in zip(ws, outs)])
        o_ref[rows, :] = (a * (1.0 / functools.reduce(jnp.add, ws))).astype(BF16)
        return carry

    lax.fori_loop(0, ATTN_SPAN // BAND, merge, 0)


def _attn_prompt(qkv_units, batch, seq):
    assert seq % ATTN_SPAN == 0
    units = ATTN_SPAN // BAND
    per_seq = seq // BAND
    spans = seq // ATTN_SPAN
    q_spec = pl.BlockSpec((units, BAND, A_WIDTH), lambda b, s: (b * spans + s, 0, 0))
    kv_spec = pl.BlockSpec((per_seq, BAND, A_WIDTH), lambda b, s: (b, 0, 0))
    return pl.pallas_call(
        functools.partial(_attn_prompt_kernel, units=units),
        grid=(batch, spans),
        in_specs=[q_spec] * 3 + [kv_spec] * 6,
        out_specs=pl.BlockSpec((ATTN_SPAN, A_WIDTH), lambda b, s: (b * spans + s, 0)),
        out_shape=jax.ShapeDtypeStruct((batch * seq, A_WIDTH), BF16),
        scratch_shapes=[pltpu.VMEM((N_GROUPS, A_WIDTH // LANES, ATTN_SPAN, LANES), F32)] * 2
        + [pltpu.VMEM((2, A_HEADS * BAND, 2 * BAND), F32)],
        compiler_params=_params(2),
        name="attn_prompt",
    )(*qkv_units)


def _attn_sample_stages(q_ref, k_ref, v_ref, c1_ref, c4_ref, c16_ref, o_ref, *, t_new, bb):
    hmask = _head_masks(t_new)
    rows = A_HEADS * t_new
    row_t = lax.broadcasted_iota(jnp.int32, (rows, LANES), 0) & (t_new - 1)
    col = lax.broadcasted_iota(jnp.int32, (rows, LANES), 1)
    pad = jnp.zeros((LANES - t_new, A_WIDTH), F32)
    caches = (c1_ref, c4_ref, c16_ref)
    jobs = [(i, g) for i in range(bb) for g in range(N_GROUPS)]

    ok_old, ok_new = [], []
    for d, c_ref in zip(SWA_DILATIONS, caches):
        window = c_ref.shape[-1]
        assert window == BAND * d, (window, d)
        pos_t = lax.broadcasted_iota(jnp.int32, (rows, window), 0) & (t_new - 1)
        pos = lax.broadcasted_iota(jnp.int32, (rows, window), 1)
        ok_old.append((pos >= pos_t) & (((pos - pos_t) & (d - 1)) == 0))
        ok_new.append((col <= row_t) & (((row_t - col) & (d - 1)) == 0))

    def new_rows(ref, i, g):
        sl = slice(g * A_WIDTH, (g + 1) * A_WIDTH)
        return jnp.concatenate([ref[i, :, sl], pad], axis=0).astype(BF16)

    def masked_q(i, g):
        q = q_ref[i, :, g * A_WIDTH:(g + 1) * A_WIDTH].astype(F32)
        return jnp.concatenate([q * m.astype(F32) for m in hmask], axis=0).astype(BF16)

    qms = [masked_q(i, g) for i, g in jobs]
    s_old = [jnp.where(ok_old[g], _dot(qm, caches[g][i, 0].astype(BF16)), -jnp.inf)
             for qm, (i, g) in zip(qms, jobs)]
    s_new = [jnp.where(ok_new[g], _dot_nt(qm, new_rows(k_ref, i, g)), -jnp.inf)
             for qm, (i, g) in zip(qms, jobs)]
    yield
    ms = [jnp.maximum(jnp.max(so, axis=-1, keepdims=True), jnp.max(sn, axis=-1, keepdims=True))
          for so, sn in zip(s_old, s_new)]
    p_old = [jnp.exp(so - m) for so, m in zip(s_old, ms)]
    p_new = [jnp.exp(sn - m) for sn, m in zip(s_new, ms)]
    dens = [jnp.sum(po, axis=-1, keepdims=True) + jnp.sum(pn, axis=-1, keepdims=True)
            for po, pn in zip(p_old, p_new)]
    yield
    accs = [_dot_nt(po.astype(BF16), caches[g][i, 1].astype(BF16))
            + _dot(pn.astype(BF16), new_rows(v_ref, i, g))
            for po, pn, (i, g) in zip(p_old, p_new, jobs)]
    yield

    for i in range(bb):
        num = jnp.zeros((t_new, A_WIDTH), F32)
        den_all = jnp.zeros((t_new, A_WIDTH), F32)
        m_all = jnp.full((t_new, A_WIDTH), -jnp.inf, F32)
        for g in range(N_GROUPS):
            j = jobs.index((i, g))
            o_g = jnp.zeros((t_new, A_WIDTH), F32)
            lse_g = jnp.zeros((t_new, A_WIDTH), F32)
            for h in range(A_HEADS):
                rs = slice(h * t_new, (h + 1) * t_new)
                o_g = jnp.where(hmask[h], accs[j][rs] * (1.0 / dens[j][rs]), o_g)
                lse_g = jnp.where(hmask[h], ms[j][rs] + jnp.log(dens[j][rs]), lse_g)
            m_next = jnp.maximum(m_all, lse_g)
            scale_old = jnp.exp(m_all - m_next)
            w = jnp.exp(lse_g - m_next)
            num = num * scale_old + w * o_g
            den_all = den_all * scale_old + w
            m_all = m_next
        o_ref[i] = (num * (1.0 / den_all)).astype(BF16)


def _split2(x):
    hi = x.astype(BF16)
    return hi, (x - hi.astype(F32)).astype(BF16)


def _gla_out(o, gate):
    return (o * lax.rsqrt(jnp.mean(o * o, axis=-1, keepdims=True) + EPS) * gate).astype(BF16)


def _col_bcast(vec_row, width):
    sq = jnp.broadcast_to(vec_row, (LANES, LANES)).T
    return jnp.concatenate([sq] * (width // LANES), axis=1)


GLA_CHUNK = 512
GLA_SEQS = 4


def _gla_prompt_kernel(q_ref, k_ref, v_ref, la_ref, rg_ref, o_ref, s_ref, *, n_blocks):
    cb, sub = GLA_BLOCK, GLA_SUB
    n_sub = cb // sub
    n_seq = q_ref.shape[0]

    @pl.when(pl.program_id(1) == 0)
    def _():
        s_ref[...] = jnp.zeros_like(s_ref)

    jobs = [(n, h) for n in range(n_seq) for h in range(G_HEADS)]

    def hk(x, h):
        return x[:, h * G_DK:(h + 1) * G_DK]

    def rows_at(x, lo_row):
        pieces = [jnp.zeros((lo_row, x.shape[1]), BF16)] if lo_row else []
        pieces.append(x)
        rest = cb - lo_row - x.shape[0]
        if rest:
            pieces.append(jnp.zeros((rest, x.shape[1]), BF16))
        return jnp.concatenate(pieces, axis=0) if len(pieces) > 1 else x

    def decays(n, rows, tri):
        hi, lo = _split2(la_ref[n, rows, :])
        b = _dot(tri, hi) + _dot(tri, lo)
        q = q_ref[n, rows, :].astype(F32)
        k = k_ref[n, rows, :].astype(F32)
        b_last = b[cb - 1:cb, :]
        qe = (q * jnp.exp(b)).astype(BF16)
        sub_end = [b[(j + 1) * sub - 1:(j + 1) * sub, :] for j in range(n_sub)]
        own_end = jnp.concatenate(
            [jnp.broadcast_to(e, (sub, G_KW)) for e in sub_end], axis=0)
        kk = (k * jnp.exp(own_end - b)).astype(BF16)
        q_parts, k_parts = [], []
        for j in range(n_sub):
            lo_row = j * sub
            qj = q[lo_row:, :] * jnp.exp(b[lo_row:, :] - sub_end[j])
            q_parts.append(rows_at(qj.astype(BF16), lo_row))
            k_parts.append(rows_at(kk[lo_row:lo_row + sub], lo_row))
        return qe, q_parts, k_parts, k * jnp.exp(b_last - b), jnp.exp(b_last)

    def step(i, carry):
        rows = pl.ds(pl.multiple_of(i * cb, cb), cb)
        row = lax.broadcasted_iota(jnp.int32, (cb, cb), 0)
        col = lax.broadcasted_iota(jnp.int32, (cb, cb), 1)
        tri = (row >= col).astype(F32).astype(BF16)

        per_seq = [decays(n, rows, tri) for n in range(n_seq)]
        states = [s_ref[n, h] for n, h in jobs]
        vals = [v_ref[n, rows, h * G_DV:(h + 1) * G_DV] for n, h in jobs]
        inter = [_dot(hk(per_seq[n][0], h), s.astype(BF16)) for s, (n, h) in zip(states, jobs)]
        atts = [_dot_nt(jnp.concatenate([hk(p, h) for p in per_seq[n][1]], axis=1),
                        jnp.concatenate([hk(p, h) for p in per_seq[n][2]], axis=1))
                for n, h in jobs]
        atts = [jnp.where(col <= row, a, 0.0).astype(BF16) for a in atts]
        outs = [x + _dot(a, v) for x, a, v in zip(inter, atts, vals)]
        for s, v, (n, h) in zip(states, vals, jobs):
            kd_t = hk(per_seq[n][3], h).T.astype(BF16)
            s_ref[n, h] = s * _col_bcast(hk(per_seq[n][4], h), G_DV) + _dot(kd_t, v)
        for o, (n, h) in zip(outs, jobs):
            vs = slice(h * G_DV, (h + 1) * G_DV)
            o_ref[n, rows, vs] = _gla_out(o, rg_ref[n, rows, vs].astype(F32))
        return carry

    lax.fori_loop(0, n_blocks, step, 0)


def _gla_prompt(qg, kg, vg, la, rg, batch, seq):
    assert seq % GLA_CHUNK == 0 and batch % GLA_SEQS == 0
    chunks = seq // GLA_CHUNK

    def spec(width):
        return pl.BlockSpec((GLA_SEQS, GLA_CHUNK, width), lambda b, c: (b, c, 0))

    def per_seq(x):
        return x.reshape(batch, seq, x.shape[-1])
    o, s = pl.pallas_call(
        functools.partial(_gla_prompt_kernel, n_blocks=GLA_CHUNK // GLA_BLOCK),
        grid=(batch // GLA_SEQS, chunks),
        in_specs=[spec(G_KW), spec(G_KW), spec(G_VW), spec(G_KW), spec(G_VW)],
        out_specs=[spec(G_VW),
                   pl.BlockSpec((GLA_SEQS, G_HEADS, G_DK, G_DV), lambda b, c: (b, 0, 0, 0))],
        out_shape=[jax.ShapeDtypeStruct((batch, seq, G_VW), BF16),
                   jax.ShapeDtypeStruct((batch, G_HEADS, G_DK, G_DV), F32)],
        compiler_params=_params(2),
        name="gla_prompt",
    )(per_seq(qg), per_seq(kg), per_seq(vg), per_seq(la), per_seq(rg))
    return o.reshape(batch * seq, G_VW), s


def _gla_sample_kernel(q_ref, k_ref, v_ref, la_ref, rg_ref, s0_ref, o_ref, s_ref,
                       *, bb, t_new):
    per_body = 4

    def hk(x, h):
        return x[:, h * G_DK:(h + 1) * G_DK]

    def hv(x, h):
        return x[:, h * G_DV:(h + 1) * G_DV]

    def body(i0, carry):
        trow = lax.broadcasted_iota(jnp.int32, (t_new, G_KW), 0)
        arow = lax.broadcasted_iota(jnp.int32, (t_new, LANES), 0)
        acol = lax.broadcasted_iota(jnp.int32, (t_new, LANES), 1)
        zk = jnp.zeros((LANES - t_new, G_KW), F32)
        zv = jnp.zeros((LANES - t_new, G_VW), F32)
        seqs = [per_body * i0 + j for j in range(per_body)]
        jobs = [(n, h) for n in range(per_body) for h in range(G_HEADS)]

        bs = []
        for i in seqs:
            b = la_ref[i]
            shift = 1
            while shift < t_new:
                b = b + jnp.where(trow >= shift, pltpu.roll(b, shift, 0), 0.0)
                shift *= 2
            bs.append(b)
        qs = [q_ref[i].astype(F32) for i in seqs]
        ks = [k_ref[i].astype(F32) for i in seqs]
        vs = [jnp.concatenate([v_ref[i].astype(F32), zv], axis=0).astype(BF16) for i in seqs]
        b_last = [b[t_new - 1:t_new, :] for b in bs]
        qe = [(q * jnp.exp(b)).astype(BF16) for q, b in zip(qs, bs)]
        ke = [jnp.concatenate([k * jnp.exp(-b), zk], axis=0).astype(BF16) for k, b in zip(ks, bs)]
        kd = [jnp.concatenate([k * jnp.exp(bl - b), zk], axis=0)
              for k, b, bl in zip(ks, bs, b_last)]
        decay = [jnp.exp(bl) for bl in b_last]

        s0 = [s0_ref[seqs[n], h] for n, h in jobs]
        att = [jnp.where(acol <= arow, _dot_nt(hk(qe[n], h), hk(ke[n], h)), 0.0).astype(BF16)
               for n, h in jobs]
        out = [_dot(a, hv(vs[n], h)) + _dot(hk(qe[n], h), s.astype(BF16))
               for a, s, (n, h) in zip(att, s0, jobs)]
        for s, (n, h) in zip(s0, jobs):
            s_ref[seqs[n], h] = (s * _col_bcast(hk(decay[n], h), G_DV)
                                 + _dot(hk(kd[n], h).T.astype(BF16), hv(vs[n], h)))
        for o, (n, h) in zip(out, jobs):
            sl = slice(h * G_DV, (h + 1) * G_DV)
            o_ref[seqs[n], :, sl] = _gla_out(o, rg_ref[seqs[n], :, sl].astype(F32))
        return carry

    assert bb % per_body == 0
    lax.fori_loop(0, bb // per_body, body, 0)


def _gla_sample_rider(qg, kg, vg, la, rg, s0, batch, t_new, steps):
    assert batch % steps == 0
    bb = batch // steps

    def spec(width):
        return pl.BlockSpec((bb, t_new, width), lambda i: (i, 0, 0))
    st_spec = pl.BlockSpec((bb, G_HEADS, G_DK, G_DV), lambda i: (i, 0, 0, 0))
    arrays = [qg.reshape(batch, t_new, G_KW), kg.reshape(batch, t_new, G_KW),
              vg.reshape(batch, t_new, G_VW), la.reshape(batch, t_new, G_KW),
              rg.reshape(batch, t_new, G_VW), s0]
    return (functools.partial(_gla_sample_kernel, bb=bb, t_new=t_new), arrays,
            [spec(G_KW), spec(G_KW), spec(G_VW), spec(G_KW), spec(G_VW), st_spec],
            [spec(G_VW), st_spec],
            [jax.ShapeDtypeStruct((batch, t_new, G_VW), BF16),
             jax.ShapeDtypeStruct((batch, G_HEADS, G_DK, G_DV), F32)])


def _out_kernel(x_ref, a_ref, gb_ref, gate_ref, wa_ref, wb_ref, wo_ref, g_ref, wg_ref, wu_ref,
                wd_ref, o_ref, act_ref, between=None):
    a_out = _dot(a_ref[...], wa_ref[...])
    b_out = _dot(gb_ref[...], wb_ref[...])
    mix = (gate_ref[:, 0:D_MODEL].astype(F32) * a_out
           + gate_ref[:, D_MODEL:2 * D_MODEL].astype(F32) * b_out)
    x2 = x_ref[...] + _dot(mix.astype(BF16), wo_ref[...])
    o_ref[...] = _swiglu_residual(x2, g_ref, wg_ref, wu_ref, wd_ref, act_ref, between)


def _out(x1, attn, gb, gates, wa, wb, wo, g, wg, wu, wd, tm, row0=0):
    t = attn.shape[0]
    assert row0 % tm == 0 and t % tm == 0
    block0 = row0 // tm

    def tok(width):
        return pl.BlockSpec((tm, width), lambda i: (i, 0))
    consts = [wa, wb, wo, g, wg, wu, wd]
    return pl.pallas_call(
        _out_kernel,
        grid=(t // tm,),
        in_specs=[pl.BlockSpec((tm, D_MODEL), lambda i: (i + block0, 0)),
                  tok(A_WIDTH), tok(G_VW), tok(2 * D_MODEL)]
        + [_resident(c.shape) for c in consts],
        out_specs=tok(D_MODEL),
        out_shape=jax.ShapeDtypeStruct((t, D_MODEL), F32),
        scratch_shapes=[pltpu.VMEM((tm, D_FF), BF16)],
        compiler_params=_params(1),
        name="out_ffn2",
    )(x1, attn, gb, gates, *consts)


def _out_with_sample_attn_kernel(*refs, t_new, bb):
    out_in, attn_in = refs[:11], refs[11:17]
    y_ref, a_ref, act_ref = refs[17:]
    stages = _attn_sample_stages(*attn_in, a_ref, t_new=t_new, bb=bb)
    every = 3

    def between(c):
        if c % every == every - 1:
            next(stages, None)

    _out_kernel(*out_in, y_ref, act_ref, between=between)
    for _ in stages:
        pass


def _out_with_sample_attn(x1, attn, gb, gates, wa, wb, wo, g, wg, wu, wd,
                          qa, ka, va, c128, c512, c2048, batch, t_new, tm):
    t = attn.shape[0]
    steps = t // tm
    assert batch % steps == 0
    bb = batch // steps
    w3 = N_GROUPS * A_WIDTH

    def tok(width):
        return pl.BlockSpec((tm, width), lambda i: (i, 0))
    consts = [wa, wb, wo, g, wg, wu, wd]
    new_spec = pl.BlockSpec((bb, t_new, w3), lambda i: (i, 0, 0))
    cache_views = [jnp.transpose(c, (0, 2, 3, 4, 1)).reshape(batch, 2, A_WIDTH, c.shape[1])
                   for c in (c128, c512, c2048)]
    cache_specs = [pl.BlockSpec((bb, 2, A_WIDTH, c.shape[-1]), lambda i: (i, 0, 0, 0))
                   for c in cache_views]
    y, a_s = pl.pallas_call(
        functools.partial(_out_with_sample_attn_kernel, t_new=t_new, bb=bb),
        grid=(steps,),
        in_specs=[tok(D_MODEL), tok(A_WIDTH), tok(G_VW), tok(2 * D_MODEL)]
        + [_resident(c.shape) for c in consts] + [new_spec] * 3 + cache_specs,
        out_specs=[tok(D_MODEL), pl.BlockSpec((bb, t_new, A_WIDTH), lambda i: (i, 0, 0))],
        out_shape=[jax.ShapeDtypeStruct((t, D_MODEL), F32),
                   jax.ShapeDtypeStruct((batch, t_new, A_WIDTH), BF16)],
        scratch_shapes=[pltpu.VMEM((tm, D_FF), BF16)],
        compiler_params=_params(1),
        name="out_ffn2_attn_sample",
    )(x1, attn, gb, gates, *consts,
      qa.reshape(batch, t_new, w3), ka.reshape(batch, t_new, w3), va.reshape(batch, t_new, w3),
      *cache_views)
    return y, a_s.reshape(batch * t_new, A_WIDTH)


def _sample_rows_kernel(k_ref, v_ref, *refs, batch, t_new):
    o_refs, slab_ref = refs[:N_GROUPS], refs[N_GROUPS]
    n_slab = N_GROUPS * A_WIDTH // LANES
    per_group = A_WIDTH // LANES
    for kv, src in enumerate((k_ref, v_ref)):
        for s in range(n_slab):
            slab_ref[kv, s] = src[:, s * LANES:(s + 1) * LANES]
    for kv in range(2):
        for s in range(n_slab):
            g, part = divmod(s, per_group)
            for t in range(t_new):
                rows = slab_ref[kv, s, pl.ds(t, batch, stride=t_new), :]
                o_refs[g][t, kv, part * LANES:(part + 1) * LANES, :] = rows.T


def _sample_rows(ka, va, batch, t_new):
    assert batch == LANES
    n_slab = N_GROUPS * A_WIDTH // LANES
    outs = pl.pallas_call(
        functools.partial(_sample_rows_kernel, batch=batch, t_new=t_new),
        out_shape=[jax.ShapeDtypeStruct((t_new, 2, A_WIDTH, batch), F32)] * N_GROUPS,
        scratch_shapes=[pltpu.VMEM((2, n_slab, batch * t_new, LANES), F32)],
        compiler_params=pltpu.CompilerParams(vmem_limit_bytes=VMEM_LIMIT),
        name="sample_rows",
    )(ka, va)
    return [jnp.transpose(o.reshape(t_new, 2, A_HEADS, A_HEAD_DIM, batch), (4, 0, 1, 2, 3))
            for o in outs]


def _position_major(rows, batch):
    window = rows.shape[-1]
    return jnp.transpose(rows.reshape(batch, 2, A_HEADS, A_HEAD_DIM, window), (0, 4, 1, 2, 3))


def kernel(x_prompt, x_sample, cache_swa128_kv, cache_swa512_kv, cache_swa2048_kv, state_gla,
           norm_ffn1, ffn1_gate, ffn1_up, ffn1_down, norm_mix, w_in, a_q_norm, a_k_norm,
           g_alpha_up, g_alpha_bias, g_out_norm, w_a_out, w_b_out, w_out,
           norm_ffn2, ffn2_gate, ffn2_up, ffn2_down):
    batch, seq, _ = x_prompt.shape
    dec_batch, dec_seq, _ = x_sample.shape

    aw = N_GROUPS * A_WIDTH
    w_in_t = w_in.T.astype(BF16)
    up = jnp.pad(g_alpha_up, ((0, LANES - G_LOWRANK), (0, 0))).astype(BF16)
    bias = g_alpha_bias.reshape(1, G_KW)
    qgain = jnp.tile(a_q_norm, (1, A_HEADS)).reshape(1, aw)
    kgain = jnp.tile(a_k_norm, (1, A_HEADS)).reshape(1, aw)
    lane = jnp.arange(A_WIDTH) // A_HEAD_DIM
    hmean = jnp.where(lane[:, None] == lane[None, :], 1.0 / A_HEAD_DIM, 0.0).astype(BF16)
    ggain = jnp.tile(g_out_norm, G_HEADS).reshape(1, G_VW)
    ffn1 = (norm_ffn1.reshape(1, D_MODEL), ffn1_gate, ffn1_up, ffn1_down)
    ffn2 = (norm_ffn2.reshape(1, D_MODEL), ffn2_gate.astype(BF16), ffn2_up.astype(BF16),
            ffn2_down.astype(BF16))
    proj_w = (norm_mix.reshape(1, D_MODEL), w_in_t, up, bias, qgain, kgain, ggain, hmean)
    out_w = (w_a_out.astype(BF16), w_b_out.astype(BF16), w_out.astype(BF16))

    t_p, t_s = batch * seq, dec_batch * dec_seq
    x1 = _ffn(x_prompt.reshape(t_p, D_MODEL), x_sample.reshape(t_s, D_MODEL), *ffn1, tm=512)

    qa, ka, va, qg_s, kg_s, vg_s, rg_s, la_s, gates_s = _proj(
        x1, *proj_w, tm=512, rows=range(t_p, t_p + t_s))
    rows_s = _sample_rows(ka, va, dec_batch, dec_seq)

    rider = _gla_sample_rider(qg_s, kg_s, vg_s, la_s, rg_s, state_gla, dec_batch, dec_seq,
                              steps=t_p // 512)
    proj_p = _proj(x1, *proj_w, tm=512, rows=range(0, t_p), seq=seq, rider=rider)
    qkv_units, rows_p = proj_p[:9], proj_p[9:12]
    (qg, kg, vg, rg, la, gates), (gbs, state_s) = proj_p[12:18], proj_p[18:]
    a_p = _attn_prompt(qkv_units, batch, seq)
    gbp, state_p = _gla_prompt(qg, kg, vg, la, rg, batch, seq)
    y_prompt, a_s = _out_with_sample_attn(
        x1, a_p, gbp, gates, *out_w, *ffn2, qa, ka, va,
        cache_swa128_kv, cache_swa512_kv, cache_swa2048_kv, dec_batch, dec_seq, tm=256)
    y_prompt = y_prompt.reshape(batch, seq, D_MODEL)
    rows_p = [_position_major(r, batch) for r in rows_p]

    y_sample = _out(x1, a_s, gbs.reshape(t_s, G_VW), gates_s, *out_w, *ffn2, tm=512,
                    row0=t_p).reshape(dec_batch, dec_seq, D_MODEL)

    return (y_prompt, y_sample, rows_p[0], rows_p[1], rows_p[2], state_p,
            rows_s[0], rows_s[1], rows_s[2], state_s)
```

```python
import functools

import jax
import jax.numpy as jnp
from jax import lax
from jax.experimental import pallas as pl
from jax.experimental.pallas import tpu as pltpu

F32 = jnp.float32
BF16 = jnp.bfloat16

D_MODEL = 1024
D_FF = 2816
SWA_DILATIONS = (1, 4, 16)
N_GROUPS = 3
A_HEADS = 4
A_HEAD_DIM = 64
A_WIDTH = A_HEADS * A_HEAD_DIM
BAND = 128
G_HEADS = 4
G_DK = 128
G_DV = 256
G_KW = G_HEADS * G_DK
G_VW = G_HEADS * G_DV
G_LOWRANK = 16
G_TAU = 16.0
EPS = 1e-6
W_IN_GLA = 3 * N_GROUPS * A_WIDTH
W_IN_LR = W_IN_GLA + 2 * G_KW + 2 * G_VW
W_IN_GATE = W_IN_LR + G_LOWRANK
D_IN = W_IN_GATE + 2 * D_MODEL

LANES = 128
GLA_BLOCK = 128
GLA_SUB = 32
FF_CHUNK = 256
N_CHUNK = 512
VMEM_LIMIT = 60 * 1024 * 1024


def _resident(shape):
    nd = len(shape)
    return pl.BlockSpec(shape, lambda *_: (0,) * nd, pipeline_mode=pl.Buffered(1))


def _params(n_axes):
    return pltpu.CompilerParams(
        dimension_semantics=("arbitrary",) * n_axes, vmem_limit_bytes=VMEM_LIMIT)


def _dot(a, b):
    return jnp.dot(a, b, preferred_element_type=F32)


def _dot_nt(a, b):
    return lax.dot_general(a, b, (((1,), (1,)), ((), ())), preferred_element_type=F32)


def _rmsnorm(x, g):
    return x * lax.rsqrt(jnp.mean(x * x, axis=-1, keepdims=True) + EPS) * g


def _swiglu_residual(x, g_ref, wg_ref, wu_ref, wd_ref, act_ref, between=None):
    h = _rmsnorm(x, g_ref[...]).astype(BF16)
    for c in range(D_FF // FF_CHUNK):
        sl = slice(c * FF_CHUNK, (c + 1) * FF_CHUNK)
        gate = _dot(h, wg_ref[:, sl].astype(BF16))
        up = _dot(h, wu_ref[:, sl].astype(BF16))
        act_ref[:, sl] = (gate * jax.nn.sigmoid(gate) * up).astype(BF16)
        if between is not None:
            between(c)
    return x + 0.5 * _dot(act_ref[...], wd_ref[...].astype(BF16))


BF16_SUBLANES = 16


def _ffn_kernel(*refs, steps_a, n_casts):
    xa_ref, xb_ref, g_ref, wg_ref, wu_ref, wd_ref = refs[:6]
    cast_in, o_ref = refs[6:6 + n_casts], refs[6 + n_casts]
    cast_out, act_ref = refs[7 + n_casts:7 + 2 * n_casts], refs[7 + 2 * n_casts]
    x = jnp.where(pl.program_id(0) < steps_a, xa_ref[...], xb_ref[...])
    o_ref[...] = _swiglu_residual(x, g_ref, wg_ref, wu_ref, wd_ref, act_ref)
    for src, dst in zip(cast_in, cast_out):
        dst[...] = src[...].astype(BF16)


def _slab_spec(shape, steps):
    rows, cols = shape
    slab = next(r for r in range(BF16_SUBLANES, rows + 1, BF16_SUBLANES)
                if rows % r == 0 and rows // r <= steps)
    last = rows // slab - 1
    return pl.BlockSpec((slab, cols), lambda i: (jnp.minimum(i, last), 0))


def _ffn(xa, xb, g, wg, wu, wd, tm, casts=()):
    steps_a, steps_b = xa.shape[0] // tm, xb.shape[0] // tm
    assert xa.shape[0] % tm == 0 and xb.shape[0] % tm == 0
    steps = steps_a + steps_b
    cast_specs = [_slab_spec(w.shape, steps) for w in casts]
    return pl.pallas_call(
        functools.partial(_ffn_kernel, steps_a=steps_a, n_casts=len(casts)),
        grid=(steps,),
        in_specs=[
            pl.BlockSpec((tm, D_MODEL), lambda i: (jnp.minimum(i, steps_a - 1), 0)),
            pl.BlockSpec((tm, D_MODEL), lambda i: (jnp.maximum(i - steps_a, 0), 0)),
            _resident((1, D_MODEL)),
            _resident((D_MODEL, D_FF)),
            _resident((D_MODEL, D_FF)),
            _resident((D_FF, D_MODEL)),
        ] + cast_specs,
        out_specs=[pl.BlockSpec((tm, D_MODEL), lambda i: (i, 0))] + cast_specs,
        out_shape=[jax.ShapeDtypeStruct((xa.shape[0] + xb.shape[0], D_MODEL), F32)]
        + [jax.ShapeDtypeStruct(w.shape, BF16) for w in casts],
        scratch_shapes=[pltpu.VMEM((tm, D_FF), BF16)],
        compiler_params=_params(1),
        name="ffn1",
    )(xa, xb, g, wg, wu, wd, *casts)


def _log_sigmoid(z):
    return jnp.minimum(z, 0.0) - jnp.log1p(jnp.exp(-jnp.abs(z)))


def _store_blocked(x, d, out_ref, perm_ref):
    tm = x.shape[0]
    if d == 1:
        for j in range(tm // BAND):
            out_ref[j] = x[j * BAND:(j + 1) * BAND].astype(BF16)
        return
    for s in range(A_WIDTH // LANES):
        perm_ref[s] = x[:, s * LANES:(s + 1) * LANES]
    for r in range(d):
        rows = jnp.concatenate(
            [perm_ref[s, pl.ds(r, tm // d, stride=d), :] for s in range(A_WIDTH // LANES)], axis=1)
        out_ref[r] = rows.astype(BF16)


def _proj_kernel(*refs, blocked, tiles_per_seq):
    (x_ref, g_ref, w_ref, up_ref, bias_ref, qgain_ref, kgain_ref, ggain_ref,
     hmean_ref) = refs[:9]
    outs = refs[9:]
    if blocked:
        qb_refs, kb_refs, vb_refs, row_refs = outs[0:3], outs[3:6], outs[6:9], outs[9:12]
        qg_ref, kg_ref, vg_ref, rg_ref, la_ref, gate_ref, perm_ref = outs[12:]
        tm = x_ref.shape[0]
    else:
        qa_ref, ka_ref, va_ref, qg_ref, kg_ref, vg_ref, rg_ref, la_ref, gate_ref = outs

    h = _rmsnorm(x_ref[...], g_ref[...]).astype(BF16)
    hmean = hmean_ref[...]

    aw = N_GROUPS * A_WIDTH
    groups = range(N_GROUPS)

    def proj(lo, width):
        return _dot_nt(h, w_ref[lo:lo + width, :])

    q_raw = [proj(g * A_WIDTH, A_WIDTH) for g in groups]
    k_raw = [proj(aw + g * A_WIDTH, A_WIDTH) for g in groups]
    vals = [proj(2 * aw + g * A_WIDTH, A_WIDTH) for g in groups]

    qg_ref[...] = (proj(W_IN_GLA, G_KW) * (G_DK ** -0.5)).astype(BF16)
    kg_ref[...] = proj(W_IN_GLA + G_KW, G_KW).astype(BF16)

    q_ms = [_dot((x * x).astype(BF16), hmean) for x in q_raw]
    k_ms = [_dot((x * x).astype(BF16), hmean) for x in k_raw]

    def attn_epilogue(g):
        d = SWA_DILATIONS[g]
        sl = slice(g * A_WIDTH, (g + 1) * A_WIDTH)
        q = q_raw[g] * lax.rsqrt(q_ms[g] + EPS) * qgain_ref[:, sl] * (A_HEAD_DIM ** -0.5)
        k = k_raw[g] * lax.rsqrt(k_ms[g] + EPS) * kgain_ref[:, sl]
        v = vals[g]
        if blocked:
            for i, (x, o_refs) in enumerate(((q, qb_refs), (k, kb_refs), (v, vb_refs))):
                _store_blocked(x, d, o_refs[g], perm_ref.at[(g - 1) * 3 + i] if d > 1 else None)
            keep = min(BAND * d, tm)
            row_refs[g][0] = k[tm - keep:, :].T
            row_refs[g][1] = v[tm - keep:, :].T
        else:
            ka_ref[:, sl] = k
            va_ref[:, sl] = v
            qa_ref[:, sl] = q.astype(BF16)

    n_gate = 2 * D_MODEL // N_CHUNK
    for c in range(n_gate):
        sl = slice(c * N_CHUNK, (c + 1) * N_CHUNK)
        gate_ref[:, sl] = jax.nn.sigmoid(proj(W_IN_GATE + c * N_CHUNK, N_CHUNK)).astype(BF16)
        if c < N_GROUPS:
            attn_epilogue(c)
    assert n_gate >= N_GROUPS

    for c in range(G_VW // N_CHUNK):
        sl = slice(c * N_CHUNK, (c + 1) * N_CHUNK)
        off = W_IN_GLA + 2 * G_KW + c * N_CHUNK
        vg_ref[:, sl] = proj(off, N_CHUNK).astype(BF16)
        r = proj(G_VW + off, N_CHUNK)
        rg_ref[:, sl] = (r * jax.nn.sigmoid(r) * ggain_ref[:, sl]).astype(BF16)

    lr = proj(W_IN_LR, LANES).astype(BF16)
    z = _dot(lr, up_ref[...]) + bias_ref[...]
    la_ref[...] = _log_sigmoid(z) * (1.0 / G_TAU)


def _unit_spec(d, tm):
    per_class = tm // d
    if per_class >= BAND:
        return pl.BlockSpec((tm // BAND, BAND, A_WIDTH), lambda i: (i, 0, 0))
    parts = BAND // per_class
    return pl.BlockSpec((d, per_class, A_WIDTH), lambda i: (i // parts, i % parts, 0))


def _row_spec(d, tm, tiles_per_seq):
    window = BAND * d
    if window <= tm:
        return pl.BlockSpec((None, 2, A_WIDTH, window), lambda i: (i // tiles_per_seq, 0, 0, 0))
    first_tile = tiles_per_seq - window // tm
    return pl.BlockSpec(
        (None, 2, A_WIDTH, tm),
        lambda i: (i // tiles_per_seq, 0, 0, jnp.maximum(i % tiles_per_seq - first_tile, 0)))


def _proj(x1, g, w_in_t, up, bias, qgain, kgain, ggain, hmean, tm, rows, seq=None, rider=None):
    t = rows.stop - rows.start
    assert rows.start % tm == 0 and t % tm == 0
    block0 = rows.start // tm
    aw = N_GROUPS * A_WIDTH
    blocked = seq is not None
    widths = [(G_KW, BF16), (G_KW, BF16), (G_VW, BF16), (G_VW, BF16), (G_KW, F32),
              (2 * D_MODEL, BF16)]
    scratch = []
    tiles_per_seq = None
    if blocked:
        assert tm == 4 * BAND and SWA_DILATIONS == (1, 4, 16)
        assert seq % tm == 0 and BAND * SWA_DILATIONS[-1] <= seq
        tiles_per_seq = seq // tm
        out_specs = ([_unit_spec(d, tm) for _ in range(3) for d in SWA_DILATIONS]
                     + [_row_spec(d, tm, tiles_per_seq) for d in SWA_DILATIONS])
        out_shape = ([jax.ShapeDtypeStruct((t // BAND, BAND, A_WIDTH), BF16)] * 9
                     + [jax.ShapeDtypeStruct((t // seq, 2, A_WIDTH, BAND * d), F32)
                        for d in SWA_DILATIONS])
        scratch = [pltpu.VMEM((3 * (N_GROUPS - 1), A_WIDTH // LANES, tm, LANES), F32)]
    else:
        widths = [(aw, BF16), (aw, F32), (aw, F32)] + widths
        out_specs, out_shape = [], []
    out_specs = out_specs + [pl.BlockSpec((tm, w), lambda i: (i, 0)) for w, _ in widths]
    out_shape = out_shape + [jax.ShapeDtypeStruct((t, w), dt) for w, dt in widths]
    assert w_in_t.shape == (D_IN, D_MODEL) and W_IN_GATE % 16 == 0
    consts = [g, w_in_t, up, bias, qgain, kgain, ggain, hmean]
    r_body, r_arrays, r_in_specs, r_out_specs, r_out_shapes = rider or (None, [], [], [], [])
    n_in, n_out, n_rin = 1 + len(consts), len(out_specs), len(r_arrays)

    def body(*refs):
        ins, r_ins, rest = refs[:n_in], refs[n_in:n_in + n_rin], refs[n_in + n_rin:]
        outs, r_outs, scr = rest[:n_out], rest[n_out:n_out + len(r_out_specs)], rest[
            n_out + len(r_out_specs):]
        if r_body is not None:
            r_body(*r_ins, *r_outs)
        _proj_kernel(*ins, *outs, *scr, blocked=blocked, tiles_per_seq=tiles_per_seq)

    return pl.pallas_call(
        body,
        grid=(t // tm,),
        in_specs=[pl.BlockSpec((tm, D_MODEL), lambda i: (i + block0, 0))]
        + [_resident(c.shape) for c in consts] + list(r_in_specs),
        out_specs=out_specs + list(r_out_specs),
        out_shape=out_shape + list(r_out_shapes),
        scratch_shapes=scratch,
        compiler_params=_params(1),
        name="in_proj" if rider is None else "in_proj_gla_sample",
    )(x1, *consts, *r_arrays)


def _log2(n):
    assert n > 0 and n & (n - 1) == 0, n
    return n.bit_length() - 1


def _head_masks(rows):
    lane = lax.broadcasted_iota(jnp.int32, (rows, A_WIDTH), 1)
    return [(lane >> _log2(A_HEAD_DIM)) == h for h in range(A_HEADS)]


ATTN_SPAN = 2048
ATTN_UNROLL = 4


def _attn_prompt_kernel(*refs, units):
    q_refs, k_refs, v_refs = refs[0:3], refs[3:6], refs[6:9]
    o_ref, osc_ref, lsc_ref, bias_ref = refs[9:]
    span = pl.program_id(1)
    n_slab = A_WIDTH // LANES

    row = lax.broadcasted_iota(jnp.int32, (A_HEADS * BAND, 2 * BAND), 0) & (BAND - 1)
    col = lax.broadcasted_iota(jnp.int32, (A_HEADS * BAND, 2 * BAND), 1)
    own = (col >= BAND) & (col - BAND <= row)
    bias_ref[0] = jnp.where(own, 0.0, -jnp.inf)
    bias_ref[1] = jnp.where(own | ((col < BAND) & (col >= row)), 0.0, -jnp.inf)

    for g, d in enumerate(SWA_DILATIONS):
        q_ref, k_ref, v_ref = q_refs[g], k_refs[g], v_refs[g]

        def scores(ul, d=d, q_ref=q_ref, k_ref=k_ref):
            u = span * units + ul
            up = jnp.maximum(u - d, 0)
            kb = jnp.concatenate([k_ref[up], k_ref[u]], axis=0)
            q = q_ref[ul]
            qm = jnp.concatenate(
                [q * m.astype(F32).astype(BF16) for m in _head_masks(BAND)], axis=0)
            has_prev = jnp.where(u >= d, 1, 0)
            return _dot_nt(qm, kb) + bias_ref[has_prev]

        def softmax(s):
            m = jnp.max(s, axis=-1, keepdims=True)
            p = jnp.exp(s - m)
            return p.astype(BF16), m, jnp.sum(p, axis=-1, keepdims=True)

        def values(ul, p, d=d, v_ref=v_ref):
            u = span * units + ul
            vb = jnp.concatenate([v_ref[jnp.maximum(u - d, 0)], v_ref[u]], axis=0)
            return _dot(p, vb)

        def store(ul, o_all, m, den, d=d, g=g):
            start = (ul >> _log2(d)) * (BAND * d) + (ul & (d - 1))
            if d == 1:
                rows = pl.ds(pl.multiple_of(start, BAND), BAND)
            else:
                rows = pl.ds(start, BAND, stride=d)
            per_slab = LANES // A_HEAD_DIM
            first = lax.broadcasted_iota(jnp.int32, (BAND, LANES), 1) < A_HEAD_DIM
            assert per_slab == 2
            for sl in range(n_slab):
                lanes = slice(sl * LANES, (sl + 1) * LANES)
                ra, rb = [slice(h * BAND, (h + 1) * BAND)
                          for h in (per_slab * sl, per_slab * sl + 1)]
                den_s = jnp.where(first, den[ra], den[rb])
                o_s = jnp.where(first, o_all[ra, lanes], o_all[rb, lanes])
                osc_ref[g, sl, rows, :] = o_s * (1.0 / den_s)
                lsc_ref[g, sl, rows, :] = jnp.where(first, m[ra], m[rb]) + jnp.log(den_s)

        def unit_group(i, carry, scores=scores, softmax=softmax, values=values, store=store):
            uls = [ATTN_UNROLL * i + j for j in range(ATTN_UNROLL)]
            ss = [scores(ul) for ul in uls]
            sm = [softmax(s) for s in ss]
            os_ = [values(ul, p) for ul, (p, _, _) in zip(uls, sm)]
            for ul, o_all, (_, m, den) in zip(uls, os_, sm):
                store(ul, o_all, m, den)
            return carry

        assert units % ATTN_UNROLL == 0
        lax.fori_loop(0, units // ATTN_UNROLL, unit_group, 0)

    def merge(i, carry):
        rows = pl.ds(pl.multiple_of(i * BAND, BAND), BAND)

        def natural(ref, g):
            return jnp.concatenate([ref[g, sl, rows, :] for sl in range(n_slab)], axis=1)

        outs = [natural(osc_ref, g) for g in range(N_GROUPS)]
        lses = [natural(lsc_ref, g) for g in range(N_GROUPS)]
        m = functools.reduce(jnp.maximum, lses)
        ws = [jnp.exp(l - m) for l in lses]
        a = functools.reduce(jnp.add, [w * o for w, o in zip(ws, outs)])
        o_ref[rows, :] = (a * (1.0 / functools.reduce(jnp.add, ws))).astype(BF16)
        return carry

    lax.fori_loop(0, ATTN_SPAN // BAND, merge, 0)


def _attn_prompt(qkv_units, batch, seq):
    assert seq % ATTN_SPAN == 0
    units = ATTN_SPAN // BAND
    per_seq = seq // BAND
    spans = seq // ATTN_SPAN
    q_spec = pl.BlockSpec((units, BAND, A_WIDTH), lambda b, s: (b * spans + s, 0, 0))
    kv_spec = pl.BlockSpec((per_seq, BAND, A_WIDTH), lambda b, s: (b, 0, 0))
    return pl.pallas_call(
        functools.partial(_attn_prompt_kernel, units=units),
        grid=(batch, spans),
        in_specs=[q_spec] * 3 + [kv_spec] * 6,
        out_specs=pl.BlockSpec((ATTN_SPAN, A_WIDTH), lambda b, s: (b * spans + s, 0)),
        out_shape=jax.ShapeDtypeStruct((batch * seq, A_WIDTH), BF16),
        scratch_shapes=[pltpu.VMEM((N_GROUPS, A_WIDTH // LANES, ATTN_SPAN, LANES), F32)] * 2
        + [pltpu.VMEM((2, A_HEADS * BAND, 2 * BAND), F32)],
        compiler_params=_params(2),
        name="attn_prompt",
    )(*qkv_units)


def _attn_sample_stages(q_ref, k_ref, v_ref, c1_ref, c4_ref, c16_ref, o_ref, *, t_new, bb):
    hmask = _head_masks(t_new)
    rows = A_HEADS * t_new
    row_t = lax.broadcasted_iota(jnp.int32, (rows, LANES), 0) & (t_new - 1)
    col = lax.broadcasted_iota(jnp.int32, (rows, LANES), 1)
    pad = jnp.zeros((LANES - t_new, A_WIDTH), F32)
    caches = (c1_ref, c4_ref, c16_ref)
    jobs = [(i, g) for i in range(bb) for g in range(N_GROUPS)]

    ok_old, ok_new = [], []
    for d, c_ref in zip(SWA_DILATIONS, caches):
        window = c_ref.shape[-1]
        assert window == BAND * d, (window, d)
        pos_t = lax.broadcasted_iota(jnp.int32, (rows, window), 0) & (t_new - 1)
        pos = lax.broadcasted_iota(jnp.int32, (rows, window), 1)
        ok_old.append((pos >= pos_t) & (((pos - pos_t) & (d - 1)) == 0))
        ok_new.append((col <= row_t) & (((row_t - col) & (d - 1)) == 0))

    def new_rows(ref, i, g):
        sl = slice(g * A_WIDTH, (g + 1) * A_WIDTH)
        return jnp.concatenate([ref[i, :, sl], pad], axis=0).astype(BF16)

    def masked_q(i, g):
        q = q_ref[i, :, g * A_WIDTH:(g + 1) * A_WIDTH].astype(F32)
        return jnp.concatenate([q * m.astype(F32) for m in hmask], axis=0).astype(BF16)

    qms = [masked_q(i, g) for i, g in jobs]
    s_old = [jnp.where(ok_old[g], _dot(qm, caches[g][i, 0].astype(BF16)), -jnp.inf)
             for qm, (i, g) in zip(qms, jobs)]
    s_new = [jnp.where(ok_new[g], _dot_nt(qm, new_rows(k_ref, i, g)), -jnp.inf)
             for qm, (i, g) in zip(qms, jobs)]
    yield
    ms = [jnp.maximum(jnp.max(so, axis=-1, keepdims=True), jnp.max(sn, axis=-1, keepdims=True))
          for so, sn in zip(s_old, s_new)]
    p_old = [jnp.exp(so - m) for so, m in zip(s_old, ms)]
    p_new = [jnp.exp(sn - m) for sn, m in zip(s_new, ms)]
    dens = [jnp.sum(po, axis=-1, keepdims=True) + jnp.sum(pn, axis=-1, keepdims=True)
            for po, pn in zip(p_old, p_new)]
    yield
    accs = [_dot_nt(po.astype(BF16), caches[g][i, 1].astype(BF16))
            + _dot(pn.astype(BF16), new_rows(v_ref, i, g))
            for po, pn, (i, g) in zip(p_old, p_new, jobs)]
    yield

    for i in range(bb):
        num = jnp.zeros((t_new, A_WIDTH), F32)
        den_all = jnp.zeros((t_new, A_WIDTH), F32)
        m_all = jnp.full((t_new, A_WIDTH), -jnp.inf, F32)
        for g in range(N_GROUPS):
            j = jobs.index((i, g))
            o_g = jnp.zeros((t_new, A_WIDTH), F32)
            lse_g = jnp.zeros((t_new, A_WIDTH), F32)
            for h in range(A_HEADS):
                rs = slice(h * t_new, (h + 1) * t_new)
                o_g = jnp.where(hmask[h], accs[j][rs] * (1.0 / dens[j][rs]), o_g)
                lse_g = jnp.where(hmask[h], ms[j][rs] + jnp.log(dens[j][rs]), lse_g)
            m_next = jnp.maximum(m_all, lse_g)
            scale_old = jnp.exp(m_all - m_next)
            w = jnp.exp(lse_g - m_next)
            num = num * scale_old + w * o_g
            den_all = den_all * scale_old + w
            m_all = m_next
        o_ref[i] = (num * (1.0 / den_all)).astype(BF16)


def _split2(x):
    hi = x.astype(BF16)
    return hi, (x - hi.astype(F32)).astype(BF16)


def _gla_out(o, gate):
    return (o * lax.rsqrt(jnp.mean(o * o, axis=-1, keepdims=True) + EPS) * gate).astype(BF16)


def _col_bcast(vec_row, width):
    sq = jnp.broadcast_to(vec_row, (LANES, LANES)).T
    return jnp.concatenate([sq] * (width // LANES), axis=1)


GLA_CHUNK = 512
GLA_SEQS = 4


def _gla_prompt_kernel(q_ref, k_ref, v_ref, la_ref, rg_ref, o_ref, s_ref, *, n_blocks):
    cb, sub = GLA_BLOCK, GLA_SUB
    n_sub = cb // sub
    n_seq = q_ref.shape[0]

    @pl.when(pl.program_id(1) == 0)
    def _():
        s_ref[...] = jnp.zeros_like(s_ref)

    jobs = [(n, h) for n in range(n_seq) for h in range(G_HEADS)]

    def hk(x, h):
        return x[:, h * G_DK:(h + 1) * G_DK]

    def rows_at(x, lo_row):
        pieces = [jnp.zeros((lo_row, x.shape[1]), BF16)] if lo_row else []
        pieces.append(x)
        rest = cb - lo_row - x.shape[0]
        if rest:
            pieces.append(jnp.zeros((rest, x.shape[1]), BF16))
        return jnp.concatenate(pieces, axis=0) if len(pieces) > 1 else x

    def decays(n, rows, tri):
        hi, lo = _split2(la_ref[n, rows, :])
        b = _dot(tri, hi) + _dot(tri, lo)
        q = q_ref[n, rows, :].astype(F32)
        k = k_ref[n, rows, :].astype(F32)
        b_last = b[cb - 1:cb, :]
        qe = (q * jnp.exp(b)).astype(BF16)
        sub_end = [b[(j + 1) * sub - 1:(j + 1) * sub, :] for j in range(n_sub)]
        own_end = jnp.concatenate(
            [jnp.broadcast_to(e, (sub, G_KW)) for e in sub_end], axis=0)
        kk = (k * jnp.exp(own_end - b)).astype(BF16)
        q_parts, k_parts = [], []
        for j in range(n_sub):
            lo_row = j * sub
            qj = q[lo_row:, :] * jnp.exp(b[lo_row:, :] - sub_end[j])
            q_parts.append(rows_at(qj.astype(BF16), lo_row))
            k_parts.append(rows_at(kk[lo_row:lo_row + sub], lo_row))
        return qe, q_parts, k_parts, k * jnp.exp(b_last - b), jnp.exp(b_last)

    def step(i, carry):
        rows = pl.ds(pl.multiple_of(i * cb, cb), cb)
        row = lax.broadcasted_iota(jnp.int32, (cb, cb), 0)
        col = lax.broadcasted_iota(jnp.int32, (cb, cb), 1)
        tri = (row >= col).astype(F32).astype(BF16)

        per_seq = [decays(n, rows, tri) for n in range(n_seq)]
        states = [s_ref[n, h] for n, h in jobs]
        vals = [v_ref[n, rows, h * G_DV:(h + 1) * G_DV] for n, h in jobs]
        inter = [_dot(hk(per_seq[n][0], h), s.astype(BF16)) for s, (n, h) in zip(states, jobs)]
        atts = [_dot_nt(jnp.concatenate([hk(p, h) for p in per_seq[n][1]], axis=1),
                        jnp.concatenate([hk(p, h) for p in per_seq[n][2]], axis=1))
                for n, h in jobs]
        atts = [jnp.where(col <= row, a, 0.0).astype(BF16) for a in atts]
        outs = [x + _dot(a, v) for x, a, v in zip(inter, atts, vals)]
        for s, v, (n, h) in zip(states, vals, jobs):
            kd_t = hk(per_seq[n][3], h).T.astype(BF16)
            s_ref[n, h] = s * _col_bcast(hk(per_seq[n][4], h), G_DV) + _dot(kd_t, v)
        for o, (n, h) in zip(outs, jobs):
            vs = slice(h * G_DV, (h + 1) * G_DV)
            o_ref[n, rows, vs] = _gla_out(o, rg_ref[n, rows, vs].astype(F32))
        return carry

    lax.fori_loop(0, n_blocks, step, 0)


def _gla_prompt(qg, kg, vg, la, rg, batch, seq):
    assert seq % GLA_CHUNK == 0 and batch % GLA_SEQS == 0
    chunks = seq // GLA_CHUNK

    def spec(width):
        return pl.BlockSpec((GLA_SEQS, GLA_CHUNK, width), lambda b, c: (b, c, 0))

    def per_seq(x):
        return x.reshape(batch, seq, x.shape[-1])
    o, s = pl.pallas_call(
        functools.partial(_gla_prompt_kernel, n_blocks=GLA_CHUNK // GLA_BLOCK),
        grid=(batch // GLA_SEQS, chunks),
        in_specs=[spec(G_KW), spec(G_KW), spec(G_VW), spec(G_KW), spec(G_VW)],
        out_specs=[spec(G_VW),
                   pl.BlockSpec((GLA_SEQS, G_HEADS, G_DK, G_DV), lambda b, c: (b, 0, 0, 0))],
        out_shape=[jax.ShapeDtypeStruct((batch, seq, G_VW), BF16),
                   jax.ShapeDtypeStruct((batch, G_HEADS, G_DK, G_DV), F32)],
        compiler_params=_params(2),
        name="gla_prompt",
    )(per_seq(qg), per_seq(kg), per_seq(vg), per_seq(la), per_seq(rg))
    return o.reshape(batch * seq, G_VW), s


def _gla_sample_kernel(q_ref, k_ref, v_ref, la_ref, rg_ref, s0_ref, o_ref, s_ref,
                       *, bb, t_new):
    per_body = 4

    def hk(x, h):
        return x[:, h * G_DK:(h + 1) * G_DK]

    def hv(x, h):
        return x[:, h * G_DV:(h + 1) * G_DV]

    def body(i0, carry):
        trow = lax.broadcasted_iota(jnp.int32, (t_new, G_KW), 0)
        arow = lax.broadcasted_iota(jnp.int32, (t_new, LANES), 0)
        acol = lax.broadcasted_iota(jnp.int32, (t_new, LANES), 1)
        zk = jnp.zeros((LANES - t_new, G_KW), F32)
        zv = jnp.zeros((LANES - t_new, G_VW), F32)
        seqs = [per_body * i0 + j for j in range(per_body)]
        jobs = [(n, h) for n in range(per_body) for h in range(G_HEADS)]

        bs = []
        for i in seqs:
            b = la_ref[i]
            shift = 1
            while shift < t_new:
                b = b + jnp.where(trow >= shift, pltpu.roll(b, shift, 0), 0.0)
                shift *= 2
            bs.append(b)
        qs = [q_ref[i].astype(F32) for i in seqs]
        ks = [k_ref[i].astype(F32) for i in seqs]
        vs = [jnp.concatenate([v_ref[i].astype(F32), zv], axis=0).astype(BF16) for i in seqs]
        b_last = [b[t_new - 1:t_new, :] for b in bs]
        qe = [(q * jnp.exp(b)).astype(BF16) for q, b in zip(qs, bs)]
        ke = [jnp.concatenate([k * jnp.exp(-b), zk], axis=0).astype(BF16) for k, b in zip(ks, bs)]
        kd = [jnp.concatenate([k * jnp.exp(bl - b), zk], axis=0)
              for k, b, bl in zip(ks, bs, b_last)]
        decay = [jnp.exp(bl) for bl in b_last]

        s0 = [s0_ref[seqs[n], h] for n, h in jobs]
        att = [jnp.where(acol <= arow, _dot_nt(hk(qe[n], h), hk(ke[n], h)), 0.0).astype(BF16)
               for n, h in jobs]
        out = [_dot(a, hv(vs[n], h)) + _dot(hk(qe[n], h), s.astype(BF16))
               for a, s, (n, h) in zip(att, s0, jobs)]
        for s, (n, h) in zip(s0, jobs):
            s_ref[seqs[n], h] = (s * _col_bcast(hk(decay[n], h), G_DV)
                                 + _dot(hk(kd[n], h).T.astype(BF16), hv(vs[n], h)))
        for o, (n, h) in zip(out, jobs):
            sl = slice(h * G_DV, (h + 1) * G_DV)
            o_ref[seqs[n], :, sl] = _gla_out(o, rg_ref[seqs[n], :, sl].astype(F32))
        return carry

    assert bb % per_body == 0
    lax.fori_loop(0, bb // per_body, body, 0)


def _gla_sample_rider(qg, kg, vg, la, rg, s0, batch, t_new, steps):
    assert batch % steps == 0
    bb = batch // steps

    def spec(width):
        return pl.BlockSpec((bb, t_new, width), lambda i: (i, 0, 0))
    st_spec = pl.BlockSpec((bb, G_HEADS, G_DK, G_DV), lambda i: (i, 0, 0, 0))
    arrays = [qg.reshape(batch, t_new, G_KW), kg.reshape(batch, t_new, G_KW),
              vg.reshape(batch, t_new, G_VW), la.reshape(batch, t_new, G_KW),
              rg.reshape(batch, t_new, G_VW), s0]
    return (functools.partial(_gla_sample_kernel, bb=bb, t_new=t_new), arrays,
            [spec(G_KW), spec(G_KW), spec(G_VW), spec(G_KW), spec(G_VW), st_spec],
            [spec(G_VW), st_spec],
            [jax.ShapeDtypeStruct((batch, t_new, G_VW), BF16),
             jax.ShapeDtypeStruct((batch, G_HEADS, G_DK, G_DV), F32)])


def _out_kernel(x_ref, a_ref, gb_ref, gate_ref, wa_ref, wb_ref, wo_ref, g_ref, wg_ref, wu_ref,
                wd_ref, o_ref, act_ref, between=None):
    a_out = _dot(a_ref[...], wa_ref[...])
    b_out = _dot(gb_ref[...], wb_ref[...])
    mix = (gate_ref[:, 0:D_MODEL].astype(F32) * a_out
           + gate_ref[:, D_MODEL:2 * D_MODEL].astype(F32) * b_out)
    x2 = x_ref[...] + _dot(mix.astype(BF16), wo_ref[...])
    o_ref[...] = _swiglu_residual(x2, g_ref, wg_ref, wu_ref, wd_ref, act_ref, between)


def _out(x1, attn, gb, gates, wa, wb, wo, g, wg, wu, wd, tm, row0=0):
    t = attn.shape[0]
    assert row0 % tm == 0 and t % tm == 0
    block0 = row0 // tm

    def tok(width):
        return pl.BlockSpec((tm, width), lambda i: (i, 0))
    consts = [wa, wb, wo, g, wg, wu, wd]
    return pl.pallas_call(
        _out_kernel,
        grid=(t // tm,),
        in_specs=[pl.BlockSpec((tm, D_MODEL), lambda i: (i + block0, 0)),
                  tok(A_WIDTH), tok(G_VW), tok(2 * D_MODEL)]
        + [_resident(c.shape) for c in consts],
        out_specs=tok(D_MODEL),
        out_shape=jax.ShapeDtypeStruct((t, D_MODEL), F32),
        scratch_shapes=[pltpu.VMEM((tm, D_FF), BF16)],
        compiler_params=_params(1),
        name="out_ffn2",
    )(x1, attn, gb, gates, *consts)


def _out_with_sample_attn_kernel(*refs, t_new, bb):
    out_in, attn_in = refs[:11], refs[11:17]
    y_ref, a_ref, act_ref = refs[17:]
    stages = _attn_sample_stages(*attn_in, a_ref, t_new=t_new, bb=bb)
    every = 3

    def between(c):
        if c % every == every - 1:
            next(stages, None)

    _out_kernel(*out_in, y_ref, act_ref, between=between)
    for _ in stages:
        pass


def _out_with_sample_attn(x1, attn, gb, gates, wa, wb, wo, g, wg, wu, wd,
                          qa, ka, va, c128, c512, c2048, batch, t_new, tm):
    t = attn.shape[0]
    steps = t // tm
    assert batch % steps == 0
    bb = batch // steps
    w3 = N_GROUPS * A_WIDTH

    def tok(width):
        return pl.BlockSpec((tm, width), lambda i: (i, 0))
    consts = [wa, wb, wo, g, wg, wu, wd]
    new_spec = pl.BlockSpec((bb, t_new, w3), lambda i: (i, 0, 0))
    cache_views = [jnp.transpose(c, (0, 2, 3, 4, 1)).reshape(batch, 2, A_WIDTH, c.shape[1])
                   for c in (c128, c512, c2048)]
    cache_specs = [pl.BlockSpec((bb, 2, A_WIDTH, c.shape[-1]), lambda i: (i, 0, 0, 0))
                   for c in cache_views]
    y, a_s = pl.pallas_call(
        functools.partial(_out_with_sample_attn_kernel, t_new=t_new, bb=bb),
        grid=(steps,),
        in_specs=[tok(D_MODEL), tok(A_WIDTH), tok(G_VW), tok(2 * D_MODEL)]
        + [_resident(c.shape) for c in consts] + [new_spec] * 3 + cache_specs,
        out_specs=[tok(D_MODEL), pl.BlockSpec((bb, t_new, A_WIDTH), lambda i: (i, 0, 0))],
        out_shape=[jax.ShapeDtypeStruct((t, D_MODEL), F32),
                   jax.ShapeDtypeStruct((batch, t_new, A_WIDTH), BF16)],
        scratch_shapes=[pltpu.VMEM((tm, D_FF), BF16)],
        compiler_params=_params(1),
        name="out_ffn2_attn_sample",
    )(x1, attn, gb, gates, *consts,
      qa.reshape(batch, t_new, w3), ka.reshape(batch, t_new, w3), va.reshape(batch, t_new, w3),
      *cache_views)
    return y, a_s.reshape(batch * t_new, A_WIDTH)


def _sample_rows_kernel(k_ref, v_ref, *refs, batch, t_new):
    o_refs, slab_ref = refs[:N_GROUPS], refs[N_GROUPS]
    n_slab = N_GROUPS * A_WIDTH // LANES
    per_group = A_WIDTH // LANES
    for kv, src in enumerate((k_ref, v_ref)):
        for s in range(n_slab):
            slab_ref[kv, s] = src[:, s * LANES:(s + 1) * LANES]
    for kv in range(2):
        for s in range(n_slab):
            g, part = divmod(s, per_group)
            for t in range(t_new):
                rows = slab_ref[kv, s, pl.ds(t, batch, stride=t_new), :]
                o_refs[g][t, kv, part * LANES:(part + 1) * LANES, :] = rows.T


def _sample_rows(ka, va, batch, t_new):
    assert batch == LANES
    n_slab = N_GROUPS * A_WIDTH // LANES
    outs = pl.pallas_call(
        functools.partial(_sample_rows_kernel, batch=batch, t_new=t_new),
        out_shape=[jax.ShapeDtypeStruct((t_new, 2, A_WIDTH, batch), F32)] * N_GROUPS,
        scratch_shapes=[pltpu.VMEM((2, n_slab, batch * t_new, LANES), F32)],
        compiler_params=pltpu.CompilerParams(vmem_limit_bytes=VMEM_LIMIT),
        name="sample_rows",
    )(ka, va)
    return [jnp.transpose(o.reshape(t_new, 2, A_HEADS, A_HEAD_DIM, batch), (4, 0, 1, 2, 3))
            for o in outs]


def _position_major(rows, batch):
    window = rows.shape[-1]
    return jnp.transpose(rows.reshape(batch, 2, A_HEADS, A_HEAD_DIM, window), (0, 4, 1, 2, 3))


def kernel(x_prompt, x_sample, cache_swa128_kv, cache_swa512_kv, cache_swa2048_kv, state_gla,
           norm_ffn1, ffn1_gate, ffn1_up, ffn1_down, norm_mix, w_in, a_q_norm, a_k_norm,
           g_alpha_up, g_alpha_bias, g_out_norm, w_a_out, w_b_out, w_out,
           norm_ffn2, ffn2_gate, ffn2_up, ffn2_down):
    batch, seq, _ = x_prompt.shape
    dec_batch, dec_seq, _ = x_sample.shape

    aw = N_GROUPS * A_WIDTH
    up =jnp.pad(g_alpha_up, ((0, LANES - G_LOWRANK), (0, 0))).astype(BF16)
    bias = g_alpha_bias.reshape(1, G_KW)
    qgain = jnp.tile(a_q_norm, (1, A_HEADS)).reshape(1, aw)
    kgain = jnp.tile(a_k_norm, (1, A_HEADS)).reshape(1, aw)
    lane = jnp.arange(A_WIDTH) // A_HEAD_DIM
    hmean = jnp.where(lane[:, None] == lane[None, :], 1.0 / A_HEAD_DIM, 0.0).astype(BF16)
    ggain = jnp.tile(g_out_norm, G_HEADS).reshape(1, G_VW)
    ffn1 = (norm_ffn1.reshape(1, D_MODEL), ffn1_gate, ffn1_up, ffn1_down)
    later = (w_in.T, ffn2_gate, ffn2_up, ffn2_down, w_a_out, w_b_out, w_out)

    t_p, t_s = batch * seq, dec_batch * dec_seq
    x1, w_in_t, *later = _ffn(x_prompt.reshape(t_p, D_MODEL), x_sample.reshape(t_s, D_MODEL),
                              *ffn1, tm=512, casts=later)
    ffn2 = (norm_ffn2.reshape(1, D_MODEL), *later[:3])
    out_w = tuple(later[3:])
    proj_w = (norm_mix.reshape(1, D_MODEL), w_in_t, up, bias, qgain, kgain, ggain, hmean)

    qa, ka, va, qg_s, kg_s, vg_s, rg_s, la_s, gates_s = _proj(
        x1, *proj_w, tm=512, rows=range(t_p, t_p + t_s))
    rows_s = _sample_rows(ka, va, dec_batch, dec_seq)

    rider = _gla_sample_rider(qg_s, kg_s, vg_s, la_s, rg_s, state_gla, dec_batch, dec_seq,
                              steps=t_p // 512)
    proj_p = _proj(x1, *proj_w, tm=512, rows=range(0, t_p), seq=seq, rider=rider)
    qkv_units, rows_p = proj_p[:9], proj_p[9:12]
    (qg, kg, vg, rg, la, gates), (gbs, state_s) = proj_p[12:18], proj_p[18:]
    a_p = _attn_prompt(qkv_units, batch, seq)
    gbp, state_p = _gla_prompt(qg, kg, vg, la, rg, batch, seq)
    y_prompt, a_s = _out_with_sample_attn(
        x1, a_p, gbp, gates, *out_w, *ffn2, qa, ka, va,
        cache_swa128_kv, cache_swa512_kv, cache_swa2048_kv, dec_batch, dec_seq, tm=256)
    y_prompt = y_prompt.reshape(batch, seq, D_MODEL)
    rows_p = [_position_major(r, batch) for r in rows_p]

    y_sample = _out(x1, a_s, gbs.reshape(t_s, G_VW), gates_s, *out_w, *ffn2, tm=512,
                    row0=t_p).reshape(dec_batch, dec_seq, D_MODEL)

    return (y_prompt, y_sample, rows_p[0], rows_p[1], rows_p[2], state_p,
            rows_s[0], rows_s[1], rows_s[2], state_s)
```
